```python
import jax, jax.numpy as jnp
from jax import lax
import numpy as np

D_MODEL = 2048
BATCH = 1
SEQ = 8192
DEPTH = 1

HEAD_DIM = 64
D_RWKV = D_MODEL // 2
D_ATTN = D_MODEL - D_RWKV
H_RWKV = D_RWKV // HEAD_DIM
H_ATTN = D_ATTN // HEAD_DIM
DECAY_LORA = 64
ICLR_LORA = 64
GATE_LORA = 160
GN_EPS = 64e-5
RMS_EPS = 1e-6
RWKV_SPLITS = (D_RWKV, 2 * D_RWKV, 3 * D_RWKV,
               3 * D_RWKV + DECAY_LORA, 3 * D_RWKV + 2 * DECAY_LORA,
               3 * D_RWKV + 2 * DECAY_LORA + ICLR_LORA,
               3 * D_RWKV + 2 * DECAY_LORA + 2 * ICLR_LORA)
RWKV_COLS = 3 * D_RWKV + 2 * DECAY_LORA + 2 * ICLR_LORA + GATE_LORA
ATTN_COLS = 3 * D_ATTN
N_IN = RWKV_COLS + ATTN_COLS
ROPE_THETA = 500000.0
ROPE_DIM = HEAD_DIM // 4
DILATED_PATTERNS = ((128, 1), (512, 4), (2048, 16))
Q_BLOCK = 128
N_GROUPS = 4
EXPERTS_PER_GROUP = 8
N_EXPERTS = N_GROUPS * EXPERTS_PER_GROUP
EXPERT_TOPK = 2
D_EXPERT = D_MODEL // 2
NEG_BIG = -1e30

kernel_name = "hymba_rwkv7_dilated_hmoe_encoder"


def rms_norm(x, g):
    xf = x.astype(jnp.float32)
    y = xf * lax.rsqrt(jnp.mean(xf * xf, axis=-1, keepdims=True) + RMS_EPS)
    return (y * g.astype(jnp.float32)).astype(x.dtype)


def centred_token_shift(z, mu_prev, mu_next):
    z_prev = jnp.pad(z[:, :-1], ((0, 0), (1, 0), (0, 0)))
    z_next = jnp.pad(z[:, 1:], ((0, 0), (0, 1), (0, 0)))
    return z + mu_prev * (z_prev - z) + mu_next * (z_next - z)


def wkv7_scan(r, w, k, v, a_vec, b_vec, reverse):
    B, S, H, N = r.shape

    def step(state, inp):
        r_t, w_t, k_t, v_t, a_t, b_t = inp
        sa = jnp.einsum('bhvk,bhk->bhv', state, a_t)
        state = (state * w_t[:, :, None, :] + sa[..., None] * b_t[:, :, None, :]
                 + v_t[..., None] * k_t[:, :, None, :])
        y_t = jnp.einsum('bhvk,bhk->bhv', state, r_t)
        return state, y_t

    xs = tuple(jnp.moveaxis(t, 1, 0) for t in (r, w, k, v, a_vec, b_vec))
    s0 = jnp.zeros((B, H, N, N), jnp.float32)
    _, ys = lax.scan(step, s0, xs, reverse=reverse)
    return jnp.moveaxis(ys, 0, 1)


def rwkv7_mixer(z, mu, w0, w2, a0, a2, g2, k_k, k_a, r_k, ln_w, ln_b):
    B, S, _ = z.shape
    out_dtype = z.dtype
    z = centred_token_shift(z, mu[0], mu[1]).astype(jnp.float32)
    r, k, v, wl_f, wl_b, al_f, al_b, gl = jnp.split(z, list(RWKV_SPLITS), axis=-1)

    def heads(t):
        return t.reshape(B, S, H_RWKV, HEAD_DIM)

    kk = heads(k * k_k)
    kk = kk * lax.rsqrt(jnp.sum(kk * kk, axis=-1, keepdims=True) + 1e-12)
    rh, vh = heads(r), heads(v)
    y = jnp.zeros((B, S, H_RWKV, HEAD_DIM), jnp.float32)
    for d, (wl, al, rev) in enumerate(((wl_f, al_f, False), (wl_b, al_b, True))):
        w_log = -jax.nn.softplus(-(w0[d] + jnp.tanh(wl) @ w2[d])) - 0.5
        decay = jnp.exp(-jnp.exp(w_log))
        a = jax.nn.sigmoid(a0[d] + al @ a2[d])
        k_d = k * (1.0 + (a - 1.0) * k_a)
        ah = heads(a)
        y = y + wkv7_scan(rh, heads(decay), heads(k_d), vh, -kk, kk * ah, rev)
    mean = jnp.mean(y, axis=-1, keepdims=True)
    var = jnp.mean(jnp.square(y - mean), axis=-1, keepdims=True)
    yn = ((y - mean) * lax.rsqrt(var + GN_EPS)).reshape(B, S, D_RWKV) * ln_w + ln_b
    bonus = (jnp.sum(rh * heads(k) * r_k, axis=-1, keepdims=True) * vh).reshape(B, S, D_RWKV)
    g = jax.nn.sigmoid(gl) @ g2
    return ((yn + bonus) * g).astype(out_dtype)


def partial_rotary(x, positions):
    half = ROPE_DIM // 2
    inv_freq = jnp.power(ROPE_THETA, -jnp.arange(half, dtype=jnp.float32) * 2.0 / ROPE_DIM)
    ang = positions.astype(jnp.float32)[..., None] * inv_freq
    cos = jnp.cos(ang)[:, :, None, :]
    sin = jnp.sin(ang)[:, :, None, :]
    xf = x.astype(jnp.float32)
    x1 = xf[..., :half]
    x2 = xf[..., half:ROPE_DIM]
    out = jnp.concatenate([x1 * cos - x2 * sin, x2 * cos + x1 * sin, xf[..., ROPE_DIM:]], axis=-1)
    return out.astype(x.dtype)


def banded_attention(q, k, v, radius):
    N, L, Dh = q.shape
    nb = -(-L // Q_BLOCK)
    Lp = nb * Q_BLOCK
    kb_len = Q_BLOCK + 2 * radius
    pad = Lp - L
    qb = jnp.pad(q, ((0, 0), (0, pad), (0, 0))).reshape(N, nb, Q_BLOCK, Dh)
    kp = jnp.pad(k, ((0, 0), (radius, pad + radius), (0, 0)))
    vp = jnp.pad(v, ((0, 0), (radius, pad + radius), (0, 0)))
    key_idx = jnp.arange(nb)[:, None] * Q_BLOCK + jnp.arange(kb_len)[None, :]
    kb = kp[:, key_idx].astype(jnp.float32)
    vb = vp[:, key_idx].astype(jnp.float32)
    q_pos = jnp.arange(Lp).reshape(nb, Q_BLOCK)[:, :, None]
    k_pos = (key_idx - radius)[:, None, :]
    mask = (jnp.abs(q_pos - k_pos) <= radius) & (k_pos >= 0) & (k_pos < L)
    s = jnp.einsum('nbqd,nbkd->nbqk', qb.astype(jnp.float32), kb) * (Dh ** -0.5)
    s = jnp.where(mask, s, NEG_BIG)
    m = jnp.max(s, axis=-1)
    p = jnp.exp(s - m[..., None])
    denom = jnp.sum(p, axis=-1)
    o = jnp.einsum('nbqk,nbkd->nbqd', p, vb) / denom[..., None]
    lse = m + jnp.log(denom)
    return o.reshape(N, Lp, Dh)[:, :L], lse.reshape(N, Lp)[:, :L]


def dilated_attention(q, k, v):
    B, H, S, Dh = q.shape
    outs, lses = [], []
    for window, dil in DILATED_PATTERNS:
        radius = window // (2 * dil)
        L = S // dil

        def to_sub(t):
            return t.reshape(B, H, L, dil, Dh).transpose(0, 1, 3, 2, 4).reshape(B * H * dil, L, Dh)

        o, lse = banded_attention(to_sub(q), to_sub(k), to_sub(v), radius)
        outs.append(o.reshape(B, H, dil, L, Dh).transpose(0, 1, 3, 2, 4).reshape(B, H, S, Dh))
        lses.append(lse.reshape(B, H, dil, L).transpose(0, 1, 3, 2).reshape(B, H, S))
    wts = jax.nn.softmax(jnp.stack(lses, axis=0), axis=0)
    return jnp.sum(wts[..., None] * jnp.stack(outs, axis=0), axis=0)


def hier_moe(h, wg, bg, we, be, w_gate, w_up, w_down):
    T = h.shape[0]
    hf = h.astype(jnp.float32)
    gl = hf @ wg.astype(jnp.float32) + bg
    gsel = jnp.argmax(gl, axis=-1)
    g1 = jnp.take_along_axis(jax.nn.softmax(gl, axis=-1), gsel[:, None], axis=1)[:, 0]
    el_all = jnp.einsum('td,gde->tge', hf, we.astype(jnp.float32)) + be
    el = jnp.take_along_axis(el_all, gsel[:, None, None], axis=1)[:, 0]
    topv, topi = lax.top_k(el, EXPERT_TOPK)
    ww = jax.nn.softmax(topv, axis=-1)
    within = jnp.sum(jax.nn.one_hot(topi, EXPERTS_PER_GROUP, dtype=jnp.float32) * ww[..., None], axis=1)
    combine = (g1[:, None, None]
               * jax.nn.one_hot(gsel, N_GROUPS, dtype=jnp.float32)[:, :, None]
               * within[:, None, :]).reshape(T, N_EXPERTS).astype(h.dtype)
    y = jnp.zeros_like(h)
    for e in range(N_EXPERTS):
        hid = jax.nn.silu(h @ w_gate[e]) * (h @ w_up[e])
        y = y + combine[:, e, None] * (hid @ w_down[e])
    return y


def setup_inputs(seed: int = 0) -> dict:
    key = jax.random.key(seed)
    ks = jax.random.split(key, 26)
    f32 = jnp.float32

    def nrm(k, shape, scale):
        return jax.random.normal(k, shape, f32) * scale

    return {
        "x": nrm(ks[0], (BATCH, SEQ, D_MODEL), 1.0),
        "positions": jnp.tile(jnp.arange(SEQ, dtype=jnp.int32)[None, :], (BATCH, 1)),
        "norm_mix": 1.0 + nrm(ks[1], (DEPTH, D_MODEL), 0.02),
        "w_in": nrm(ks[2], (DEPTH, D_MODEL, N_IN), D_MODEL ** -0.5),
        "mu_shift": jax.random.uniform(ks[3], (DEPTH, 2, RWKV_COLS), f32, 0.0, 0.5),
        "w0": jax.random.uniform(ks[4], (DEPTH, 2, D_RWKV), f32, -4.0, 1.0),
        "w2": nrm(ks[5], (DEPTH, 2, DECAY_LORA, D_RWKV), 0.5 * DECAY_LORA ** -0.5),
        "a0": nrm(ks[6], (DEPTH, 2, D_RWKV), 0.1),
        "a2": nrm(ks[7], (DEPTH, 2, ICLR_LORA, D_RWKV), 0.5 * ICLR_LORA ** -0.5),
        "g2": nrm(ks[8], (DEPTH, GATE_LORA, D_RWKV), GATE_LORA ** -0.5),
        "k_k": 0.85 + nrm(ks[9], (DEPTH, D_RWKV), 0.02),
        "k_a": 1.0 + nrm(ks[10], (DEPTH, D_RWKV), 0.02),
        "r_k": nrm(ks[11], (DEPTH, H_RWKV, HEAD_DIM), 0.1),
        "ln_x_w": 1.0 + nrm(ks[12], (DEPTH, D_RWKV), 0.02),
        "ln_x_b": nrm(ks[13], (DEPTH, D_RWKV), 0.02),
        "w_out": nrm(ks[14], (DEPTH, D_RWKV + D_ATTN, D_MODEL), D_MODEL ** -0.5),
        "norm_ffn": 1.0 + nrm(ks[15], (DEPTH, D_MODEL), 0.02),
        "router_group_w": nrm(ks[16], (DEPTH, D_MODEL, N_GROUPS), D_MODEL ** -0.5),
        "router_group_b": nrm(ks[17], (DEPTH, N_GROUPS), 0.01),
        "router_expert_w": nrm(ks[18], (DEPTH, N_GROUPS, D_MODEL, EXPERTS_PER_GROUP), D_MODEL ** -0.5),
        "router_expert_b": nrm(ks[19], (DEPTH, N_GROUPS, EXPERTS_PER_GROUP), 0.01),
        "w_gate": nrm(ks[20], (DEPTH, N_EXPERTS, D_MODEL, D_EXPERT), D_MODEL ** -0.5),
        "w_up": nrm(ks[21], (DEPTH, N_EXPERTS, D_MODEL, D_EXPERT), D_MODEL ** -0.5),
        "w_down": nrm(ks[22], (DEPTH, N_EXPERTS, D_EXPERT, D_MODEL), D_EXPERT ** -0.5),
        "norm_final": 1.0 + nrm(ks[23], (D_MODEL,), 0.02),
    }


def reference(x, positions, norm_mix, w_in, mu_shift, w0, w2, a0, a2, g2, k_k, k_a, r_k,
              ln_x_w, ln_x_b, w_out, norm_ffn, router_group_w, router_group_b,
              router_expert_w, router_expert_b, w_gate, w_up, w_down, norm_final):
    B, S, D = x.shape
    for l in range(DEPTH):
        h = rms_norm(x, norm_mix[l])
        z = h @ w_in[l]
        z_rwkv, z_attn = z[..., :RWKV_COLS], z[..., RWKV_COLS:]
        y_rwkv = rwkv7_mixer(z_rwkv, mu_shift[l], w0[l], w2[l], a0[l], a2[l], g2[l],
                             k_k[l], k_a[l], r_k[l], ln_x_w[l], ln_x_b[l])
        q, k, v = jnp.split(z_attn.reshape(B, S, 3 * H_ATTN, HEAD_DIM), 3, axis=2)
        q = partial_rotary(q, positions)
        k = partial_rotary(k, positions)
        y_attn = dilated_attention(q.transpose(0, 2, 1, 3), k.transpose(0, 2, 1, 3),
                                   v.transpose(0, 2, 1, 3))
        y_attn = y_attn.transpose(0, 2, 1, 3).reshape(B, S, D_ATTN).astype(x.dtype)
        x = x + jnp.concatenate([y_rwkv, y_attn], axis=-1) @ w_out[l]
        h = rms_norm(x, norm_ffn[l])
        y_moe = hier_moe(h.reshape(B * S, D), router_group_w[l], router_group_b[l],
                         router_expert_w[l], router_expert_b[l],
                         w_gate[l], w_up[l], w_down[l])
        x = x + y_moe.reshape(B, S, D)
    return rms_norm(x, norm_final)
```

```python
import functools

import jax
import jax.numpy as jnp
from jax import lax
from jax.experimental import pallas as pl
from jax.experimental.pallas import tpu as pltpu

HEAD_DIM = 64
LANES = 128
SUBLANES = 8
DECAY_LORA = 64
ICLR_LORA = 64
GATE_LORA = 160
GN_EPS = 64e-5
RMS_EPS = 1e-6
ROPE_THETA = 500000.0
ROPE_DIM = HEAD_DIM // 4
DILATED_PATTERNS = ((128, 1), (512, 4), (2048, 16))
Q_BLOCK = 128
N_GROUPS = 4
EXPERTS_PER_GROUP = 8
N_EXPERTS = N_GROUPS * EXPERTS_PER_GROUP
NEG_BIG = -1e30
CHUNK = 64
VMEM_LIMIT = 56 * 1024 * 1024

BF16 = jnp.bfloat16
F32 = jnp.float32


def _cparams(sem):
    return pltpu.CompilerParams(dimension_semantics=sem, vmem_limit_bytes=VMEM_LIMIT)


def _dot(a, b):
    return jnp.dot(a.astype(BF16), b.astype(BF16), preferred_element_type=F32)


def _dot_nt(a, b):
    return lax.dot_general(a.astype(BF16), b.astype(BF16), (((1,), (1,)), ((), ())),
                           preferred_element_type=F32)


def _dot_tn(a, b):
    return lax.dot_general(a.astype(BF16), b.astype(BF16), (((0,), (0,)), ((), ())),
                           preferred_element_type=F32)


def _split2(x):
    hi = x.astype(BF16)
    lo = (x - hi.astype(F32)).astype(BF16)
    return hi, lo


def _split3(x):
    hi = x.astype(BF16)
    r1 = x - hi.astype(F32)
    mid = r1.astype(BF16)
    lo = (r1 - mid.astype(F32)).astype(BF16)
    return hi, mid, lo


def _head_ones():
    i = lax.broadcasted_iota(jnp.int32, (LANES, LANES), 0) // HEAD_DIM
    j = lax.broadcasted_iota(jnp.int32, (LANES, LANES), 1) // HEAD_DIM
    return (i == j).astype(BF16)


def _head_sum(x, ones_bd):
    hi, lo = _split2(x)
    return (jnp.dot(hi, ones_bd, preferred_element_type=F32)
            + jnp.dot(lo, ones_bd, preferred_element_type=F32))


def _sigmoid(x):
    return 1.0 / (1.0 + jnp.exp(-x))


def _norm_matmul_kernel(x_ref, g_ref, w_ref, o_ref):
    x = x_ref[...]
    ms = jnp.mean(x * x, axis=-1, keepdims=True)
    h = x * lax.rsqrt(ms + RMS_EPS) * g_ref[...]
    o_ref[...] = jnp.dot(h.astype(BF16), w_ref[...], preferred_element_type=F32)


def _norm_matmul(x, g, w, tm):
    t, d = x.shape
    n = w.shape[1]
    return pl.pallas_call(
        _norm_matmul_kernel,
        grid=(t // tm,),
        in_specs=[pl.BlockSpec((tm, d), lambda i: (i, 0)),
                  pl.BlockSpec((1, d), lambda i: (0, 0)),
                  pl.BlockSpec((d, n), lambda i: (0, 0))],
        out_specs=pl.BlockSpec((tm, n), lambda i: (i, 0)),
        out_shape=jax.ShapeDtypeStruct((t, n), F32),
        compiler_params=_cparams(("arbitrary",)),
        name="norm_in_proj",
    )(x, g, w)


def _rwkv_kernel(zf_ref, zfp_ref, zfn_ref, zb_ref, zbp_ref, zbn_ref,
                 mu_ref, w0_ref, a0_ref, w2_ref, a2_ref, g2_ref, kk_ref, ka_ref, rk_ref,
                 yf_ref, yb_ref, bonus_ref, gate_ref,
                 zs_ref, h_ref, *, nc, d_rwkv):
    c = pl.program_id(0)
    C = CHUNK
    n_hp = d_rwkv // LANES

    @pl.when(c == 0)
    def _():
        h_ref[...] = jnp.zeros_like(h_ref)

    row = lax.broadcasted_iota(jnp.int32, (C, 1), 0)
    mu_p = mu_ref[0:1, :]
    mu_n = mu_ref[1:2, :]
    srcs = ((zf_ref, zfp_ref, zfn_ref, c), (zb_ref, zbp_ref, zbn_ref, nc - 1 - c))
    for d, (z_ref, zp_ref, zn_ref, chunk) in enumerate(srcs):
        z = z_ref[...]
        prev_row = jnp.where(chunk == 0, 0.0, zp_ref[SUBLANES - 1:SUBLANES, :])
        next_row = jnp.where(chunk == nc - 1, 0.0, zn_ref[0:1, :])
        z_prev = jnp.where(row == 0, prev_row, pltpu.roll(z, 1, 0))
        z_next = jnp.where(row == C - 1, next_row, pltpu.roll(z, C - 1, 0))
        zs_ref[d] = z + mu_p * (z_prev - z) + mu_n * (z_next - z)

    ones_bd = _head_ones()
    lane = lax.broadcasted_iota(jnp.int32, (1, LANES), 1)
    m_a = lane < HEAD_DIM
    ti = lax.broadcasted_iota(jnp.int32, (C, C), 0)
    tj = lax.broadcasted_iota(jnp.int32, (C, C), 1)
    bi = lax.broadcasted_iota(jnp.int32, (2 * C, 2 * C), 0)
    bj = lax.broadcasted_iota(jnp.int32, (2 * C, 2 * C), 1)
    same_blk = (bi // C) == (bj // C)
    eye = bi == bj
    cum_mat = ((tj <= ti).astype(BF16), (tj >= ti).astype(BF16))
    strict = (same_blk & ((bj % C) < (bi % C)), same_blk & ((bj % C) > (bi % C)))
    incl = (same_blk & ((bj % C) <= (bi % C)), same_blk & ((bj % C) >= (bi % C)))
    last_row = (C - 1, 0)
    off_wl = 3 * d_rwkv
    off_al = off_wl + LANES
    off_gl = off_al + LANES

    def bd(x):
        return jnp.concatenate([jnp.where(m_a, x, 0.0), jnp.where(m_a, 0.0, x)], axis=0)

    def unbd(x):
        return x[0:C] + x[C:2 * C]

    for hp in range(n_hp):
        sl = pl.ds(hp * LANES, LANES)
        for d in range(2):
            r = zs_ref[d, :, pl.ds(hp * LANES, LANES)]
            k = zs_ref[d, :, pl.ds(d_rwkv + hp * LANES, LANES)]
            v = zs_ref[d, :, pl.ds(2 * d_rwkv + hp * LANES, LANES)]
            wl = zs_ref[d, :, pl.ds(off_wl, LANES)]
            al = zs_ref[d, :, pl.ds(off_al, LANES)]

            kkr = k * kk_ref[:, sl]
            kk = kkr * lax.rsqrt(_head_sum(kkr * kkr, ones_bd) + 1e-12)
            u = w0_ref[d:d + 1, sl] + _dot(jnp.tanh(wl), w2_ref[d, :, sl])
            w_log = -(jnp.maximum(-u, 0.0) + jnp.log1p(jnp.exp(-jnp.abs(u)))) - 0.5
            logd = -jnp.exp(w_log)
            a = _sigmoid(a0_ref[d:d + 1, sl] + _dot(al, a2_ref[d, :, sl]))
            kd = k * (1.0 + (a - 1.0) * ka_ref[:, sl])
            b = kk * a

            l_hi, l_mid, l_lo = _split3(logd)
            cm = cum_mat[d]
            cum = (jnp.dot(cm, l_hi, preferred_element_type=F32)
                   + jnp.dot(cm, l_mid, preferred_element_type=F32)
                   + jnp.dot(cm, l_lo, preferred_element_type=F32))
            cum_c = cum[last_row[d]:last_row[d] + 1, :]
            w_t = jnp.exp(cum)
            w_prev = jnp.exp(cum - logd)
            w_inv = jnp.exp(-cum)
            w_end = jnp.exp(cum_c - cum)
            w_c = jnp.exp(cum_c)

            at = bd(-kk * w_prev)
            rt = bd(r * w_t)
            bt = b * w_inv
            kt = kd * w_inv
            bb = bd(b * w_end)
            kb = bd(kd * w_end)
            vbd = bd(v)

            lhs = jnp.concatenate([at, rt], axis=0)
            rhs = jnp.concatenate([bt, bt, kt, kt], axis=0)
            g = _dot_nt(lhs, rhs)
            x = jnp.where(strict[d], g[0:2 * C, 0:2 * C], 0.0)
            a_ak = jnp.where(strict[d], g[0:2 * C, 2 * C:4 * C], 0.0)
            a_rb = jnp.where(incl[d], g[2 * C:4 * C, 0:2 * C], 0.0)
            a_rk = jnp.where(incl[d], g[2 * C:4 * C, 2 * C:4 * C], 0.0)

            zz = jnp.concatenate([at, _dot(a_ak, vbd)], axis=1)
            n_sq = C.bit_length() - 1
            for j in range(n_sq):
                zz = zz + _dot(x, zz)
                if j + 1 < n_sq:
                    x = _dot(x, x)

            y2 = _dot(a_rb, zz)
            r_hat = rt + y2[:, 0:LANES]
            y_loc = y2[:, LANES:2 * LANES] + _dot(a_rk, vbd)
            pq = _dot_tn(bb, zz)
            p_mat = jnp.where(eye, w_c, 0.0) + pq[:, 0:LANES]
            q_mat = pq[:, LANES:2 * LANES] + _dot_tn(kb, vbd)

            h_old = h_ref[d, hp]
            y = unbd(_dot(r_hat, h_old) + y_loc)
            h_ref[d, hp] = _dot(p_mat, h_old) + q_mat
            if d == 0:
                yf_ref[:, sl] = y
                bonus_ref[:, sl] = _head_sum(r * k * rk_ref[:, sl], ones_bd) * v
                gl = zs_ref[0, :, pl.ds(off_gl, 2 * LANES)]
                gate_ref[:, sl] = _dot(_sigmoid(gl), g2_ref[:, sl])
            else:
                yb_ref[:, sl] = y


def _rwkv_mixer(z, mu, w0, a0, w2p, a2p, g2p, k_k, k_a, r_k, d_rwkv):
    t, nr = z.shape
    C = CHUNK
    nc = t // C
    hb = C // SUBLANES
    n_hp = d_rwkv // LANES
    last_hblk = t // SUBLANES - 1

    def cur(fn):
        return pl.BlockSpec((C, nr), lambda c: (fn(c), 0))

    def prev(fn):
        return pl.BlockSpec((SUBLANES, nr), lambda c: (jnp.maximum(fn(c) * hb - 1, 0), 0))

    def nxt(fn):
        return pl.BlockSpec((SUBLANES, nr), lambda c: (jnp.minimum((fn(c) + 1) * hb, last_hblk), 0))

    fwd = lambda c: c
    bwd = lambda c: nc - 1 - c

    def full(a):
        nd = a.ndim
        return pl.BlockSpec(a.shape, lambda c: (0,) * nd)

    out_spec_f = pl.BlockSpec((C, d_rwkv), lambda c: (c, 0))
    out_spec_b = pl.BlockSpec((C, d_rwkv), lambda c: (nc - 1 - c, 0))
    out_sd = jax.ShapeDtypeStruct((t, d_rwkv), F32)
    params = (mu, w0, a0, w2p, a2p, g2p, k_k, k_a, r_k)
    return pl.pallas_call(
        functools.partial(_rwkv_kernel, nc=nc, d_rwkv=d_rwkv),
        grid=(nc,),
        in_specs=[cur(fwd), prev(fwd), nxt(fwd), cur(bwd), prev(bwd), nxt(bwd)]
                 + [full(p) for p in params],
        out_specs=[out_spec_f, out_spec_b, out_spec_f, out_spec_f],
        out_shape=[out_sd, out_sd, out_sd, out_sd],
        scratch_shapes=[pltpu.VMEM((2, C, nr), F32),
                        pltpu.VMEM((2, n_hp, LANES, LANES), F32)],
        compiler_params=_cparams(("arbitrary",)),
        name="rwkv7_chunk_scan",
    )(z, z, z, z, z, z, *params)


def _rope_table_kernel(pos_ref, inv_ref, c_ref, s1_ref, s2_ref):
    half = ROPE_DIM // 2
    pos = pos_ref[...].astype(F32)
    j = lax.broadcasted_iota(jnp.int32, (1, LANES), 1) % HEAD_DIM
    ang = pos * inv_ref[...]
    cs = jnp.cos(ang)
    sn = jnp.sin(ang)
    c_ref[...] = jnp.where(j < ROPE_DIM, cs, 1.0)
    s1_ref[...] = jnp.where(j < half, -sn, 0.0)
    s2_ref[...] = jnp.where((j >= half) & (j < ROPE_DIM), sn, 0.0)


def _rope_tables(positions, tm):
    t = positions.shape[0]
    half = ROPE_DIM // 2
    inv_freq = jnp.power(ROPE_THETA, -jnp.arange(half, dtype=F32) * 2.0 / ROPE_DIM)
    inv_lane = jnp.tile(inv_freq, LANES // half)[None]
    sd = jax.ShapeDtypeStruct((t, LANES), F32)
    spec = pl.BlockSpec((tm, LANES), lambda i: (i, 0))
    return pl.pallas_call(
        _rope_table_kernel,
        grid=(t // tm,),
        in_specs=[pl.BlockSpec((tm, 1), lambda i: (i, 0)),
                  pl.BlockSpec((1, LANES), lambda i: (0, 0))],
        out_specs=[spec, spec, spec],
        out_shape=[sd, sd, sd],
        compiler_params=_cparams(("arbitrary",)),
        name="rope_tables",
    )(positions, inv_lane)


def _rot(x, c, s1, s2):
    half = ROPE_DIM // 2
    return x * c + pltpu.roll(x, LANES - half, 1) * s1 + pltpu.roll(x, half, 1) * s2


def _attn_kernel(q_ref, k_ref, v_ref, c_ref, s1_ref, s2_ref, o_ref, lse_ref, *, radius, qb, seq):
    jb = pl.program_id(2)
    win = Q_BLOCK + 2 * radius
    lane = lax.broadcasted_iota(jnp.int32, (1, LANES), 1)
    m_a = lane < HEAD_DIM
    qi = lax.broadcasted_iota(jnp.int32, (Q_BLOCK, win), 0)
    kj = lax.broadcasted_iota(jnp.int32, (Q_BLOCK, win), 1)
    for s in range(qb // Q_BLOCK):
        q0 = pl.multiple_of(jb * qb + s * Q_BLOCK, Q_BLOCK)
        start = jnp.clip(q0 - radius, 0, seq - win)
        start = pl.multiple_of(start, SUBLANES)
        qs = pl.ds(q0, Q_BLOCK)
        ks = pl.ds(start, win)
        q = _rot(q_ref[pl.ds(s * Q_BLOCK, Q_BLOCK), :], c_ref[qs, :], s1_ref[qs, :], s2_ref[qs, :])
        q = q * (HEAD_DIM ** -0.5)
        kw = _rot(k_ref[ks, :], c_ref[ks, :], s1_ref[ks, :], s2_ref[ks, :])
        vw = v_ref[ks, :].astype(BF16)
        kwb = kw.astype(BF16)
        valid = jnp.abs((q0 + qi) - (start + kj)) <= radius
        outs = []
        lses = []
        for qh in (jnp.where(m_a, q, 0.0), jnp.where(m_a, 0.0, q)):
            sc = lax.dot_general(qh.astype(BF16), kwb, (((1,), (1,)), ((), ())),
                                 preferred_element_type=F32)
            sc = jnp.where(valid, sc, NEG_BIG)
            m = jnp.max(sc, axis=-1, keepdims=True)
            p = jnp.exp(sc - m)
            den = jnp.sum(p, axis=-1, keepdims=True)
            outs.append(jnp.dot(p.astype(BF16), vw, preferred_element_type=F32) / den)
            lses.append(m + jnp.log(den))
        rows = pl.ds(s * Q_BLOCK, Q_BLOCK)
        o_ref[rows, :] = jnp.where(m_a, outs[0], outs[1])
        lse_ref[rows, :] = jnp.where(m_a, lses[0], lses[1])


def _banded_attention(z_attn, tabs, dil, radius, d_attn):
    t = z_attn.shape[0]
    seq = t // dil
    n_hp = d_attn // LANES
    cols = 3 * n_hp
    qb = min(seq, 512)
    zr = z_attn.reshape(seq, dil * 3 * d_attn)
    tr = [a.reshape(seq, dil * LANES) for a in tabs]
    tab_spec = pl.BlockSpec((seq, LANES), lambda r, h, j: (0, r))
    out_spec = pl.BlockSpec((qb, LANES), lambda r, h, j: (j, r * n_hp + h))
    sd = jax.ShapeDtypeStruct((seq, dil * d_attn), F32)
    o, lse = pl.pallas_call(
        functools.partial(_attn_kernel, radius=radius, qb=qb, seq=seq),
        grid=(dil, n_hp, seq // qb),
        in_specs=[pl.BlockSpec((qb, LANES), lambda r, h, j: (j, r * cols + h)),
                  pl.BlockSpec((seq, LANES), lambda r, h, j: (0, r * cols + n_hp + h)),
                  pl.BlockSpec((seq, LANES), lambda r, h, j: (0, r * cols + 2 * n_hp + h)),
                  tab_spec, tab_spec, tab_spec],
        out_specs=[out_spec, out_spec],
        out_shape=[sd, sd],
        compiler_params=_cparams(("arbitrary", "arbitrary", "arbitrary")),
        name=f"banded_attention_d{dil}",
    )(zr, zr, zr, *tr)
    return o.reshape(t, d_attn), lse.reshape(t, d_attn)


def _out_proj_kernel(x_ref, yf_ref, yb_ref, bonus_ref, gate_ref, lnw_ref, lnb_ref,
                     o1_ref, o2_ref, o3_ref, l1_ref, l2_ref, l3_ref, w_ref, out_ref,
                     yr_ref, *, d_rwkv):
    ones_bd = _head_ones()
    for hp in range(d_rwkv // LANES):
        sl = pl.ds(hp * LANES, LANES)
        y = yf_ref[:, sl] + yb_ref[:, sl]
        mean = _head_sum(y, ones_bd) * (1.0 / HEAD_DIM)
        yc = y - mean
        var = _head_sum(yc * yc, ones_bd) * (1.0 / HEAD_DIM)
        yn = yc * lax.rsqrt(var + GN_EPS) * lnw_ref[:, sl] + lnb_ref[:, sl]
        yr_ref[:, sl] = ((yn + bonus_ref[:, sl]) * gate_ref[:, sl]).astype(BF16)
    l1 = l1_ref[...]
    l2 = l2_ref[...]
    l3 = l3_ref[...]
    m = jnp.maximum(jnp.maximum(l1, l2), l3)
    e1 = jnp.exp(l1 - m)
    e2 = jnp.exp(l2 - m)
    e3 = jnp.exp(l3 - m)
    ya = (e1 * o1_ref[...] + e2 * o2_ref[...] + e3 * o3_ref[...]) / (e1 + e2 + e3)
    acc = jnp.dot(yr_ref[...], w_ref[0:d_rwkv, :], preferred_element_type=F32)
    acc += jnp.dot(ya.astype(BF16), w_ref[d_rwkv:, :], preferred_element_type=F32)
    out_ref[...] = x_ref[...] + acc


def _out_proj(x, yf, yb, bonus, gate, ln_w, ln_b, outs, lses, w_out, tm):
    t, d = x.shape
    d_rwkv = yf.shape[1]
    d_attn = outs[0].shape[1]
    row = lambda n: pl.BlockSpec((tm, n), lambda i: (i, 0))
    const = lambda a: pl.BlockSpec(a.shape, lambda i: (0, 0))
    return pl.pallas_call(
        functools.partial(_out_proj_kernel, d_rwkv=d_rwkv),
        grid=(t // tm,),
        in_specs=[row(d)] + [row(d_rwkv)] * 4 + [const(ln_w), const(ln_b)]
                 + [row(d_attn)] * 6 + [const(w_out)],
        out_specs=row(d),
        out_shape=jax.ShapeDtypeStruct((t, d), F32),
        scratch_shapes=[pltpu.VMEM((tm, d_rwkv), BF16)],
        compiler_params=_cparams(("arbitrary",)),
        name="merge_out_proj",
    )(x, yf, yb, bonus, gate, ln_w, ln_b, *outs, *lses, w_out)


def _router_kernel(x_ref, g_ref, w_ref, b_ref, h_ref, idx_ref, wt_ref):
    x = x_ref[...]
    ms = jnp.mean(x * x, axis=-1, keepdims=True)
    h = x * lax.rsqrt(ms + RMS_EPS) * g_ref[...]
    h_ref[...] = h
    h_hi, h_lo = _split2(h)
    w_hi = w_ref[0]
    w_lo = w_ref[1]
    logits = (jnp.dot(h_hi, w_hi, preferred_element_type=F32)
              + jnp.dot(h_hi, w_lo, preferred_element_type=F32)
              + jnp.dot(h_lo, w_hi, preferred_element_type=F32)) + b_ref[...]
    lane = lax.broadcasted_iota(jnp.int32, logits.shape, 1).astype(F32)
    big = jnp.float32(LANES)
    is_g = lane < N_GROUPS
    gl = jnp.where(is_g, logits, NEG_BIG)
    gmax = jnp.max(gl, axis=-1, keepdims=True)
    gsel = jnp.min(jnp.where(is_g & (gl == gmax), lane, big), axis=-1, keepdims=True)
    g1 = 1.0 / jnp.sum(jnp.where(is_g, jnp.exp(gl - gmax), 0.0), axis=-1, keepdims=True)
    lo = N_GROUPS + gsel * EXPERTS_PER_GROUP
    in_grp = (lane >= lo) & (lane < lo + EXPERTS_PER_GROUP)
    el = jnp.where(in_grp, logits, NEG_BIG)
    v1 = jnp.max(el, axis=-1, keepdims=True)
    i1 = jnp.min(jnp.where(in_grp & (el == v1), lane, big), axis=-1, keepdims=True)
    rest = in_grp & (lane != i1)
    el2 = jnp.where(rest, logits, NEG_BIG)
    v2 = jnp.max(el2, axis=-1, keepdims=True)
    i2 = jnp.min(jnp.where(rest & (el2 == v2), lane, big), axis=-1, keepdims=True)
    e2 = jnp.exp(v2 - v1)
    ww1 = 1.0 / (1.0 + e2)
    ww2 = e2 / (1.0 + e2)
    idx = jnp.where(lane == 0, i1 - N_GROUPS, jnp.where(lane == 1, i2 - N_GROUPS, 0.0))
    idx_ref[...] = idx.astype(jnp.int32)
    wt_ref[...] = jnp.where(lane == 0, g1 * ww1, jnp.where(lane == 1, g1 * ww2, 0.0))


def _router(x, g, w_split, b_pad, tm):
    t, d = x.shape
    return pl.pallas_call(
        _router_kernel,
        grid=(t // tm,),
        in_specs=[pl.BlockSpec((tm, d), lambda i: (i, 0)),
                  pl.BlockSpec((1, d), lambda i: (0, 0)),
                  pl.BlockSpec((2, d, LANES), lambda i: (0, 0, 0)),
                  pl.BlockSpec((1, LANES), lambda i: (0, 0))],
        out_specs=[pl.BlockSpec((tm, d), lambda i: (i, 0)),
                   pl.BlockSpec((tm, LANES), lambda i: (i, 0)),
                   pl.BlockSpec((tm, LANES), lambda i: (i, 0))],
        out_shape=[jax.ShapeDtypeStruct((t, d), F32),
                   jax.ShapeDtypeStruct((t, LANES), jnp.int32),
                   jax.ShapeDtypeStruct((t, LANES), F32)],
        compiler_params=_cparams(("arbitrary",)),
        name="router",
    )(x, g, w_split, b_pad)


def _moe_kernel(te_ref, nt_ref, rows_ref, h_hbm, wg_ref, wu_ref, wd_ref, o_ref,
                xg_ref, xb_ref, sem, *, tm):
    i = pl.program_id(0)
    f = pl.program_id(1)
    nf = pl.num_programs(1)
    active = i < nt_ref[0]

    def row_copy(j):
        src = h_hbm.at[pl.ds(rows_ref[i * tm + j], 1), :]
        return pltpu.make_async_copy(src, xg_ref.at[pl.ds(j, 1), :], sem)

    @pl.when(active & (f == 0))
    def _():
        def start(j, carry):
            row_copy(j).start()
            return carry
        lax.fori_loop(0, tm, start, 0)

        def wait(j, carry):
            row_copy(j).wait()
            return carry
        lax.fori_loop(0, tm, wait, 0)
        xb_ref[...] = xg_ref[...].astype(BF16)

    @pl.when(active)
    def _():
        xb = xb_ref[...]
        gate = jnp.dot(xb, wg_ref[...].astype(BF16), preferred_element_type=F32)
        up = jnp.dot(xb, wu_ref[...].astype(BF16), preferred_element_type=F32)
        hid = (gate * _sigmoid(gate)) * up
        part = jnp.dot(hid.astype(BF16), wd_ref[...].astype(BF16), preferred_element_type=F32)

        @pl.when(f == 0)
        def _():
            o_ref[...] = part

        @pl.when(f != 0)
        def _():
            o_ref[...] += part

    @pl.when(jnp.logical_not(active) & (f == 0))
    def _():
        o_ref[...] = jnp.zeros_like(o_ref)


def _moe(h, tile_expert, n_tiles, rows, w_gate, w_up, w_down, tm, tf):
    t, d = h.shape
    n_e, _, d_e = w_gate.shape
    nt_max = tile_expert.shape[0]
    nf = d_e // tf

    def fidx(i, f, nt):
        return jnp.where(i < nt[0], f, nf - 1)

    grid_spec = pltpu.PrefetchScalarGridSpec(
        num_scalar_prefetch=3,
        grid=(nt_max, d_e // tf),
        in_specs=[pl.BlockSpec(memory_space=pl.ANY),
                  pl.BlockSpec((None, d, tf), lambda i, f, te, nt, rw: (te[i], 0, fidx(i, f, nt))),
                  pl.BlockSpec((None, d, tf), lambda i, f, te, nt, rw: (te[i], 0, fidx(i, f, nt))),
                  pl.BlockSpec((None, tf, d), lambda i, f, te, nt, rw: (te[i], fidx(i, f, nt), 0))],
        out_specs=pl.BlockSpec((tm, d), lambda i, f, te, nt, rw: (i, 0)),
        scratch_shapes=[pltpu.VMEM((tm, d), F32), pltpu.VMEM((tm, d), BF16),
                        pltpu.SemaphoreType.DMA(())],
    )
    return pl.pallas_call(
        functools.partial(_moe_kernel, tm=tm),
        grid_spec=grid_spec,
        out_shape=jax.ShapeDtypeStruct((nt_max * tm, d), F32),
        compiler_params=_cparams(("arbitrary", "arbitrary")),
        name="grouped_expert_mlp",
    )(tile_expert, n_tiles, rows, h, w_gate, w_up, w_down)


def _final_kernel(s0_ref, s1_ref, x_ref, wt_ref, g_ref, y_hbm, o_ref, y0_ref, y1_ref, sem, *, tm):
    i = pl.program_id(0)

    def copies(j):
        c0 = pltpu.make_async_copy(y_hbm.at[pl.ds(s0_ref[i * tm + j], 1), :],
                                   y0_ref.at[pl.ds(j, 1), :], sem.at[0])
        c1 = pltpu.make_async_copy(y_hbm.at[pl.ds(s1_ref[i * tm + j], 1), :],
                                   y1_ref.at[pl.ds(j, 1), :], sem.at[1])
        return c0, c1

    def start(j, carry):
        c0, c1 = copies(j)
        c0.start()
        c1.start()
        return carry
    lax.fori_loop(0, tm, start, 0)

    def wait(j, carry):
        c0, c1 = copies(j)
        c0.wait()
        c1.wait()
        return carry
    lax.fori_loop(0, tm, wait, 0)

    wt = wt_ref[...]
    x = x_ref[...] + wt[:, 0:1] * y0_ref[...] + wt[:, 1:2] * y1_ref[...]
    ms = jnp.mean(x * x, axis=-1, keepdims=True)
    o_ref[...] = x * lax.rsqrt(ms + RMS_EPS) * g_ref[...]


def _final(slot0, slot1, x, wts, g, y_sorted, tm):
    t, d = x.shape
    grid_spec = pltpu.PrefetchScalarGridSpec(
        num_scalar_prefetch=2,
        grid=(t // tm,),
        in_specs=[pl.BlockSpec((tm, d), lambda i, s0, s1: (i, 0)),
                  pl.BlockSpec((tm, LANES), lambda i, s0, s1: (i, 0)),
                  pl.BlockSpec((1, d), lambda i, s0, s1: (0, 0)),
                  pl.BlockSpec(memory_space=pl.ANY)],
        out_specs=pl.BlockSpec((tm, d), lambda i, s0, s1: (i, 0)),
        scratch_shapes=[pltpu.VMEM((tm, d), F32), pltpu.VMEM((tm, d), F32),
                        pltpu.SemaphoreType.DMA((2,))],
    )
    return pl.pallas_call(
        functools.partial(_final_kernel, tm=tm),
        grid_spec=grid_spec,
        out_shape=jax.ShapeDtypeStruct((t, d), F32),
        compiler_params=_cparams(("arbitrary",)),
        name="combine_final_norm",
    )(slot0, slot1, x, wts, g, y_sorted)


def _dispatch_plan(experts, tm):
    t = experts.shape[0]
    ef = experts.T.reshape(-1)
    onehot = (ef[:, None] == jnp.arange(N_EXPERTS, dtype=jnp.int32)[None, :]).astype(jnp.int32)
    csum = jnp.cumsum(onehot, axis=0)
    rank = jnp.take_along_axis(csum, ef[:, None], axis=1)[:, 0] - 1
    counts = csum[-1]
    tiles_e = (counts + tm - 1) // tm
    tile_end = jnp.cumsum(tiles_e)
    tile_start = tile_end - tiles_e
    n_tiles = tile_end[-1]
    nt_max = (2 * t) // tm + N_EXPERTS
    slot = tile_start[ef] * tm + rank
    tid = jnp.minimum(jnp.arange(nt_max, dtype=jnp.int32), n_tiles - 1)
    tile_expert = jnp.searchsorted(tile_end, tid, side="right").astype(jnp.int32)
    tok = jnp.tile(jnp.arange(t, dtype=jnp.int32), 2)
    rows = jnp.zeros((nt_max * tm,), jnp.int32).at[slot].set(tok)
    return slot[:t], slot[t:], rows, tile_expert, n_tiles.reshape(1).astype(jnp.int32)


def kernel(x, positions, norm_mix, w_in, mu_shift, w0, w2, a0, a2, g2, k_k, k_a, r_k, ln_x_w, ln_x_b, w_out, norm_ffn, router_group_w, router_group_b, router_expert_w, router_expert_b, w_gate, w_up, w_down, norm_final):
    bsz, seq, d = x.shape
    assert bsz == 1
    depth = w_in.shape[0]
    d_rwkv = k_k.shape[1]
    d_attn = w_out.shape[1] - d_rwkv
    rwkv_cols = mu_shift.shape[2]
    nr = -(-rwkv_cols // (2 * LANES)) * (2 * LANES)
    assert 3 * d_rwkv + 2 * LANES + GATE_LORA == rwkv_cols and nr == 3 * d_rwkv + 4 * LANES
    tm_moe = 512
    tf_moe = 256

    xt = x[0]
    tabs = _rope_tables(positions[0][:, None], 512)
    for l in range(depth):
        w_r = jnp.pad(w_in[l][:, :rwkv_cols], ((0, 0), (0, nr - rwkv_cols))).astype(BF16)
        w_a = w_in[l][:, rwkv_cols:].astype(BF16)
        mu = jnp.pad(mu_shift[l], ((0, 0), (0, nr - rwkv_cols)))
        zl = jnp.zeros((DECAY_LORA, d_rwkv), F32)
        w2p = jnp.stack([jnp.concatenate([w2[l, 0], zl]), jnp.concatenate([zl, w2[l, 1]])]).astype(BF16)
        a2p = jnp.stack([jnp.concatenate([a2[l, 0], zl]), jnp.concatenate([zl, a2[l, 1]])]).astype(BF16)
        g2p = jnp.pad(g2[l], ((0, 2 * LANES - GATE_LORA), (0, 0))).astype(BF16)
        rk = r_k[l].reshape(1, d_rwkv)

        z_r = _norm_matmul(xt, norm_mix[l][None], w_r, 256)
        z_a = _norm_matmul(xt, norm_mix[l][None], w_a, 256)
        yf, yb, bonus, gate = _rwkv_mixer(z_r, mu, w0[l], a0[l], w2p, a2p, g2p,
                                          k_k[l][None], k_a[l][None], rk, d_rwkv)
        outs, lses = [], []
        for window, dil in DILATED_PATTERNS:
            o, lse = _banded_attention(z_a, tabs, dil, window // (2 * dil), d_attn)
            outs.append(o)
            lses.append(lse)
        x2 = _out_proj(xt, yf, yb, bonus, gate, ln_x_w[l][None], ln_x_b[l][None],
                       outs, lses, w_out[l].astype(BF16), 256)

        w_rt = jnp.concatenate(
            [router_group_w[l], router_expert_w[l].transpose(1, 0, 2).reshape(d, N_EXPERTS)], axis=1)
        w_rt = jnp.pad(w_rt, ((0, 0), (0, LANES - w_rt.shape[1])))
        rt_hi = w_rt.astype(BF16)
        rt_lo = (w_rt - rt_hi.astype(F32)).astype(BF16)
        b_rt = jnp.concatenate([router_group_b[l], router_expert_b[l].reshape(-1)])
        b_rt = jnp.pad(b_rt, (0, LANES - b_rt.shape[0]))[None]
        h, idx, wts = _router(x2, norm_ffn[l][None], jnp.stack([rt_hi, rt_lo]), b_rt, 256)

        slot0, slot1, rows, tile_expert, n_tiles = _dispatch_plan(idx[:, :2], tm_moe)
        y_sorted = _moe(h, tile_expert, n_tiles, rows, w_gate[l], w_up[l], w_down[l], tm_moe, tf_moe)
        is_last = l == depth - 1
        assert is_last, "the combine kernel applies the final norm; depth must be 1"
        xt = _final(slot0, slot1, x2, wts, norm_final[None], y_sorted, 256)
    return xt[None]
```

```python
import functools

import jax
import jax.numpy as jnp
from jax import lax
from jax.experimental import pallas as pl
from jax.experimental.pallas import tpu as pltpu

HEAD_DIM = 64
LANES = 128
SUBLANES = 8
DECAY_LORA = 64
ICLR_LORA = 64
GATE_LORA = 160
GN_EPS = 64e-5
RMS_EPS = 1e-6
ROPE_THETA = 500000.0
ROPE_DIM = HEAD_DIM // 4
DILATED_PATTERNS = ((128, 1), (512, 4), (2048, 16))
Q_BLOCK = 128
N_GROUPS = 4
EXPERTS_PER_GROUP = 8
N_EXPERTS = N_GROUPS * EXPERTS_PER_GROUP
NEG_BIG = -1e30
CHUNK = 64
VMEM_LIMIT = 56 * 1024 * 1024

BF16 = jnp.bfloat16
F32 = jnp.float32


def _cparams(sem):
    return pltpu.CompilerParams(dimension_semantics=sem, vmem_limit_bytes=VMEM_LIMIT)


def _dot(a, b):
    return jnp.dot(a.astype(BF16), b.astype(BF16), preferred_element_type=F32)


def _dot_nt(a, b):
    return lax.dot_general(a.astype(BF16), b.astype(BF16), (((1,), (1,)), ((), ())),
                           preferred_element_type=F32)


def _dot_tn(a, b):
    return lax.dot_general(a.astype(BF16), b.astype(BF16), (((0,), (0,)), ((), ())),
                           preferred_element_type=F32)


def _split2(x):
    hi = x.astype(BF16)
    lo = (x - hi.astype(F32)).astype(BF16)
    return hi, lo


def _split3(x):
    hi = x.astype(BF16)
    r1 = x - hi.astype(F32)
    mid = r1.astype(BF16)
    lo = (r1 - mid.astype(F32)).astype(BF16)
    return hi, mid, lo


def _head_ones():
    i = lax.broadcasted_iota(jnp.int32, (LANES, LANES), 0) // HEAD_DIM
    j = lax.broadcasted_iota(jnp.int32, (LANES, LANES), 1) // HEAD_DIM
    return (i == j).astype(BF16)


def _head_sum(x, ones_bd):
    hi, lo = _split2(x)
    return (jnp.dot(hi, ones_bd, preferred_element_type=F32)
            + jnp.dot(lo, ones_bd, preferred_element_type=F32))


def _sigmoid(x):
    return 1.0 / (1.0 + jnp.exp(-x))


def _norm_matmul_kernel(x_ref, g_ref, w_ref, o_ref):
    x = x_ref[...]
    ms = jnp.mean(x * x, axis=-1, keepdims=True)
    h = x * lax.rsqrt(ms + RMS_EPS) * g_ref[...]
    o_ref[...] = jnp.dot(h.astype(BF16), w_ref[...], preferred_element_type=F32)


def _norm_matmul(x, g, w, tm):
    t, d = x.shape
    n = w.shape[1]
    return pl.pallas_call(
        _norm_matmul_kernel,
        grid=(t // tm,),
        in_specs=[pl.BlockSpec((tm, d), lambda i: (i, 0)),
                  pl.BlockSpec((1, d), lambda i: (0, 0)),
                  pl.BlockSpec((d, n), lambda i: (0, 0))],
        out_specs=pl.BlockSpec((tm, n), lambda i: (i, 0)),
        out_shape=jax.ShapeDtypeStruct((t, n), F32),
        compiler_params=_cparams(("arbitrary",)),
        name="norm_in_proj",
    )(x, g, w)


def _rwkv_kernel(zf_ref, zfp_ref, zfn_ref, zb_ref, zbp_ref, zbn_ref,
                 mu_ref, w0_ref, a0_ref, w2_ref, a2_ref, g2_ref, kk_ref, ka_ref, rk_ref,
                 yf_ref, yb_ref, bonus_ref, gate_ref,
                 zs_ref, prep_ref, h_ref, *, nc, d_rwkv):
    c = pl.program_id(0)
    C = CHUNK
    n_hp = d_rwkv // LANES

    @pl.when(c == 0)
    def _():
        h_ref[...] = jnp.zeros_like(h_ref)

    row = lax.broadcasted_iota(jnp.int32, (C, 1), 0)
    mu_p = mu_ref[0:1, :]
    mu_n = mu_ref[1:2, :]
    srcs = ((zf_ref, zfp_ref, zfn_ref, c), (zb_ref, zbp_ref, zbn_ref, nc - 1 - c))
    for d, (z_ref, zp_ref, zn_ref, chunk) in enumerate(srcs):
        z = z_ref[...]
        prev_row = jnp.where(chunk == 0, 0.0, zp_ref[SUBLANES - 1:SUBLANES, :])
        next_row = jnp.where(chunk == nc - 1, 0.0, zn_ref[0:1, :])
        z_prev = jnp.where(row == 0, prev_row, pltpu.roll(z, 1, 0))
        z_next = jnp.where(row == C - 1, next_row, pltpu.roll(z, C - 1, 0))
        zs_ref[d] = z + mu_p * (z_prev - z) + mu_n * (z_next - z)

    ones_bd = _head_ones()
    lane = lax.broadcasted_iota(jnp.int32, (1, LANES), 1)
    m_a = lane < HEAD_DIM
    ti = lax.broadcasted_iota(jnp.int32, (C, C), 0)
    tj = lax.broadcasted_iota(jnp.int32, (C, C), 1)
    bi = lax.broadcasted_iota(jnp.int32, (2 * C, 2 * C), 0)
    bj = lax.broadcasted_iota(jnp.int32, (2 * C, 2 * C), 1)
    same_blk = (bi // C) == (bj // C)
    eye = bi == bj
    cum_mat = ((tj <= ti).astype(BF16), (tj >= ti).astype(BF16))
    strict = (same_blk & ((bj % C) < (bi % C)), same_blk & ((bj % C) > (bi % C)))
    incl = (same_blk & ((bj % C) <= (bi % C)), same_blk & ((bj % C) >= (bi % C)))
    last_row = (C - 1, 0)
    off_wl = 3 * d_rwkv
    off_al = off_wl + LANES
    off_gl = off_al + LANES

    def to_rows(x):
        return jnp.concatenate([x[:, s * LANES:(s + 1) * LANES] for s in range(n_hp)], axis=0)

    def to_lanes(x):
        return jnp.concatenate([x[s * C:(s + 1) * C] for s in range(n_hp)], axis=1)

    def bd(x):
        return jnp.concatenate([jnp.where(m_a, x, 0), jnp.where(m_a, 0, x)], axis=0)

    def unbd(x):
        return x[0:C] + x[C:2 * C]

    w_c = []
    for d in range(2):
        r = zs_ref[d, :, 0:d_rwkv]
        k = zs_ref[d, :, d_rwkv:2 * d_rwkv]
        kkr = k * kk_ref[...]
        sums = [to_rows(kkr * kkr)]
        if d == 0:
            sums.append(to_rows(r * k * rk_ref[...]))
        hs = _head_sum(jnp.concatenate(sums, axis=0), ones_bd)
        kk = kkr * lax.rsqrt(to_lanes(hs[0:n_hp * C]) + 1e-12)
        if d == 0:
            bonus_ref[...] = to_lanes(hs[n_hp * C:2 * n_hp * C]) * zs_ref[0, :, 2 * d_rwkv:3 * d_rwkv]
            gate_ref[...] = _dot(_sigmoid(zs_ref[0, :, off_gl:off_gl + 2 * LANES]), g2_ref[...])
        u = w0_ref[d:d + 1, :] + _dot(jnp.tanh(zs_ref[d, :, off_wl:off_wl + LANES]), w2_ref[d])
        w_log = -(jnp.maximum(-u, 0.0) + jnp.log1p(jnp.exp(-jnp.abs(u)))) - 0.5
        logd = -jnp.exp(w_log)
        a = _sigmoid(a0_ref[d:d + 1, :] + _dot(zs_ref[d, :, off_al:off_al + LANES], a2_ref[d]))
        kd = k * (1.0 + (a - 1.0) * ka_ref[...])
        b = kk * a
        l_hi, l_mid, l_lo = _split3(logd)
        cm = cum_mat[d]
        cum = (jnp.dot(cm, l_hi, preferred_element_type=F32)
               + jnp.dot(cm, l_mid, preferred_element_type=F32)
               + jnp.dot(cm, l_lo, preferred_element_type=F32))
        cum_c = cum[last_row[d]:last_row[d] + 1, :]
        w_inv = jnp.exp(-cum)
        w_c.append(jnp.exp(cum_c))
        w_end = w_c[d] * w_inv
        prep_ref[d, 0] = -kk * jnp.exp(cum - logd)
        prep_ref[d, 1] = r * jnp.exp(cum)
        prep_ref[d, 2] = b * w_inv
        prep_ref[d, 3] = kd * w_inv
        prep_ref[d, 4] = b * w_end
        prep_ref[d, 5] = kd * w_end

    blocks = [(d, hp) for hp in range(n_hp) for d in range(2)]

    def slab(d, i, hp):
        return prep_ref[d, i, :, hp * LANES:(hp + 1) * LANES]

    incl2 = tuple(jnp.concatenate([m, m], axis=1) for m in incl)
    eye_f = eye.astype(F32)
    xs, aks, lows, z0s, vbds = [], [], [], [], []
    for d, hp in blocks:
        at = bd(slab(d, 0, hp).astype(BF16))
        rt = bd(slab(d, 1, hp).astype(BF16))
        bt = slab(d, 2, hp).astype(BF16)
        kt = slab(d, 3, hp).astype(BF16)
        lhs = jnp.concatenate([at, rt], axis=0)
        rhs = jnp.concatenate([bt, bt, kt, kt], axis=0)
        g = lax.dot_general(lhs, rhs, (((1,), (1,)), ((), ())), preferred_element_type=F32)
        xs.append(jnp.where(strict[d], g[0:2 * C, 0:2 * C], 0.0))
        aks.append(jnp.where(strict[d], g[0:2 * C, 2 * C:4 * C], 0.0).astype(BF16))
        lows.append(jnp.where(incl2[d], g[2 * C:4 * C, :], 0.0).astype(BF16))
    for i, (d, hp) in enumerate(blocks):
        vbd = bd(zs_ref[d, :, 2 * d_rwkv + hp * LANES:2 * d_rwkv + (hp + 1) * LANES].astype(BF16))
        vbds.append(vbd)
        akv = jnp.dot(aks[i], vbd, preferred_element_type=F32)
        z0s.append(jnp.concatenate([bd(slab(d, 0, hp).astype(BF16)), akv.astype(BF16)], axis=1))

    n_sq = C.bit_length() - 1
    ts = [eye_f + x for x in xs]
    ps = [x.astype(BF16) for x in xs]
    for j in range(1, n_sq):
        last = j == n_sq - 1
        for i in range(len(blocks)):
            if j == 1:
                ps[i] = jnp.dot(ps[i], ps[i], preferred_element_type=F32).astype(BF16)
            tb = ts[i].astype(BF16)
            if last:
                ts[i] = ts[i] + jnp.dot(ps[i], tb, preferred_element_type=F32)
            else:
                res = jnp.dot(ps[i], jnp.concatenate([ps[i], tb], axis=1), preferred_element_type=F32)
                ts[i] = ts[i] + res[:, LANES:2 * LANES]
                ps[i] = res[:, 0:LANES].astype(BF16)

    wts = []
    for i in range(len(blocks)):
        zz = jnp.dot(ts[i].astype(BF16), z0s[i], preferred_element_type=F32)
        bottom = jnp.concatenate([jnp.zeros((2 * C, LANES), BF16), vbds[i]], axis=1)
        wts.append(jnp.concatenate([zz.astype(BF16), bottom], axis=0))
    outs = []
    for i, (d, hp) in enumerate(blocks):
        bb_t = bd(slab(d, 4, hp)).T.astype(BF16)
        kb_t = bd(slab(d, 5, hp)).T.astype(BF16)
        lhs = jnp.concatenate([lows[i], jnp.concatenate([bb_t, kb_t], axis=1)], axis=0)
        outs.append(jnp.dot(lhs, wts[i], preferred_element_type=F32))
    for i, (d, hp) in enumerate(blocks):
        sl = pl.ds(hp * LANES, LANES)
        o = outs[i]
        r_hat = bd(slab(d, 1, hp)) + o[0:2 * C, 0:LANES]
        p_mat = eye_f * w_c[d][:, hp * LANES:(hp + 1) * LANES] + o[2 * C:4 * C, 0:LANES]
        lhs = jnp.concatenate([r_hat, p_mat], axis=0).astype(BF16)
        res = jnp.dot(lhs, h_ref[d, hp].astype(BF16), preferred_element_type=F32)
        y = unbd(res[0:2 * C] + o[0:2 * C, LANES:2 * LANES])
        h_ref[d, hp] = res[2 * C:4 * C] + o[2 * C:4 * C, LANES:2 * LANES]
        if d == 0:
            yf_ref[:, sl] = y
        else:
            yb_ref[:, sl] = y


def _rwkv_mixer(z, mu, w0, a0, w2p, a2p, g2p, k_k, k_a, r_k, d_rwkv):
    t, nr = z.shape
    C = CHUNK
    nc = t // C
    hb = C // SUBLANES
    n_hp = d_rwkv // LANES
    last_hblk = t // SUBLANES - 1

    def cur(fn):
        return pl.BlockSpec((C, nr), lambda c: (fn(c), 0))

    def prev(fn):
        return pl.BlockSpec((SUBLANES, nr), lambda c: (jnp.maximum(fn(c) * hb - 1, 0), 0))

    def nxt(fn):
        return pl.BlockSpec((SUBLANES, nr), lambda c: (jnp.minimum((fn(c) + 1) * hb, last_hblk), 0))

    fwd = lambda c: c
    bwd = lambda c: nc - 1 - c

    def full(a):
        nd = a.ndim
        return pl.BlockSpec(a.shape, lambda c: (0,) * nd)

    out_spec_f = pl.BlockSpec((C, d_rwkv), lambda c: (c, 0))
    out_spec_b = pl.BlockSpec((C, d_rwkv), lambda c: (nc - 1 - c, 0))
    out_sd = jax.ShapeDtypeStruct((t, d_rwkv), F32)
    params = (mu, w0, a0, w2p, a2p, g2p, k_k, k_a, r_k)
    return pl.pallas_call(
        functools.partial(_rwkv_kernel, nc=nc, d_rwkv=d_rwkv),
        grid=(nc,),
        in_specs=[cur(fwd), prev(fwd), nxt(fwd), cur(bwd), prev(bwd), nxt(bwd)]
                 + [full(p) for p in params],
        out_specs=[out_spec_f, out_spec_b, out_spec_f, out_spec_f],
        out_shape=[out_sd, out_sd, out_sd, out_sd],
        scratch_shapes=[pltpu.VMEM((2, C, nr), F32),
                        pltpu.VMEM((2, 6, C, d_rwkv), F32),
                        pltpu.VMEM((2, n_hp, LANES, LANES), F32)],
        compiler_params=_cparams(("arbitrary",)),
        name="rwkv7_chunk_scan",
    )(z, z, z, z, z, z, *params)


def _rope_table_kernel(pos_ref, inv_ref, c_ref, s1_ref, s2_ref):
    half = ROPE_DIM // 2
    pos = pos_ref[...].astype(F32)
    j = lax.broadcasted_iota(jnp.int32, (1, LANES), 1) % HEAD_DIM
    ang = pos * inv_ref[...]
    cs = jnp.cos(ang)
    sn = jnp.sin(ang)
    c_ref[...] = jnp.where(j < ROPE_DIM, cs, 1.0)
    s1_ref[...] = jnp.where(j < half, -sn, 0.0)
    s2_ref[...] = jnp.where((j >= half) & (j < ROPE_DIM), sn, 0.0)


def _rope_tables(positions, tm):
    t = positions.shape[0]
    half = ROPE_DIM // 2
    inv_freq = jnp.power(ROPE_THETA, -jnp.arange(half, dtype=F32) * 2.0 / ROPE_DIM)
    inv_lane = jnp.tile(inv_freq, LANES // half)[None]
    sd = jax.ShapeDtypeStruct((t, LANES), F32)
    spec = pl.BlockSpec((tm, LANES), lambda i: (i, 0))
    return pl.pallas_call(
        _rope_table_kernel,
        grid=(t // tm,),
        in_specs=[pl.BlockSpec((tm, 1), lambda i: (i, 0)),
                  pl.BlockSpec((1, LANES), lambda i: (0, 0))],
        out_specs=[spec, spec, spec],
        out_shape=[sd, sd, sd],
        compiler_params=_cparams(("arbitrary",)),
        name="rope_tables",
    )(positions, inv_lane)


def _rot(x, c, s1, s2):
    half = ROPE_DIM // 2
    return x * c + pltpu.roll(x, LANES - half, 1) * s1 + pltpu.roll(x, half, 1) * s2


def _attn_kernel(q_ref, k_ref, v_ref, c_ref, s1_ref, s2_ref, o_ref, lse_ref, *, radius, qb, seq):
    jb = pl.program_id(2)
    win = Q_BLOCK + 2 * radius
    lane = lax.broadcasted_iota(jnp.int32, (1, LANES), 1)
    m_a = lane < HEAD_DIM
    qi = lax.broadcasted_iota(jnp.int32, (Q_BLOCK, win), 0)
    kj = lax.broadcasted_iota(jnp.int32, (Q_BLOCK, win), 1)
    for s in range(qb // Q_BLOCK):
        q0 = pl.multiple_of(jb * qb + s * Q_BLOCK, Q_BLOCK)
        start = jnp.clip(q0 - radius, 0, seq - win)
        start = pl.multiple_of(start, SUBLANES)
        qs = pl.ds(q0, Q_BLOCK)
        ks = pl.ds(start, win)
        q = _rot(q_ref[pl.ds(s * Q_BLOCK, Q_BLOCK), :], c_ref[qs, :], s1_ref[qs, :], s2_ref[qs, :])
        q = q * (HEAD_DIM ** -0.5)
        kw = _rot(k_ref[ks, :], c_ref[ks, :], s1_ref[ks, :], s2_ref[ks, :])
        vw = v_ref[ks, :].astype(BF16)
        kwb = kw.astype(BF16)
        valid = jnp.abs((q0 + qi) - (start + kj)) <= radius
        outs = []
        lses = []
        for qh in (jnp.where(m_a, q, 0.0), jnp.where(m_a, 0.0, q)):
            sc = lax.dot_general(qh.astype(BF16), kwb, (((1,), (1,)), ((), ())),
                                 preferred_element_type=F32)
            sc = jnp.where(valid, sc, NEG_BIG)
            m = jnp.max(sc, axis=-1, keepdims=True)
            p = jnp.exp(sc - m)
            den = jnp.sum(p, axis=-1, keepdims=True)
            outs.append(jnp.dot(p.astype(BF16), vw, preferred_element_type=F32) / den)
            lses.append(m + jnp.log(den))
        rows = pl.ds(s * Q_BLOCK, Q_BLOCK)
        o_ref[rows, :] = jnp.where(m_a, outs[0], outs[1])
        lse_ref[rows, :] = jnp.where(m_a, lses[0], lses[1])


def _banded_attention(z_attn, tabs, dil, radius, d_attn):
    t = z_attn.shape[0]
    seq = t // dil
    n_hp = d_attn // LANES
    cols = 3 * n_hp
    qb = min(seq, 512)
    zr = z_attn.reshape(seq, dil * 3 * d_attn)
    tr = [a.reshape(seq, dil * LANES) for a in tabs]
    tab_spec = pl.BlockSpec((seq, LANES), lambda r, h, j: (0, r))
    out_spec = pl.BlockSpec((qb, LANES), lambda r, h, j: (j, r * n_hp + h))
    sd = jax.ShapeDtypeStruct((seq, dil * d_attn), F32)
    o, lse = pl.pallas_call(
        functools.partial(_attn_kernel, radius=radius, qb=qb, seq=seq),
        grid=(dil, n_hp, seq // qb),
        in_specs=[pl.BlockSpec((qb, LANES), lambda r, h, j: (j, r * cols + h)),
                  pl.BlockSpec((seq, LANES), lambda r, h, j: (0, r * cols + n_hp + h)),
                  pl.BlockSpec((seq, LANES), lambda r, h, j: (0, r * cols + 2 * n_hp + h)),
                  tab_spec, tab_spec, tab_spec],
        out_specs=[out_spec, out_spec],
        out_shape=[sd, sd],
        compiler_params=_cparams(("arbitrary", "arbitrary", "arbitrary")),
        name=f"banded_attention_d{dil}",
    )(zr, zr, zr, *tr)
    return o.reshape(t, d_attn), lse.reshape(t, d_attn)


def _out_proj_kernel(x_ref, yf_ref, yb_ref, bonus_ref, gate_ref, lnw_ref, lnb_ref,
                     o1_ref, o2_ref, o3_ref, l1_ref, l2_ref, l3_ref, w_ref, out_ref,
                     yr_ref, *, d_rwkv):
    ones_bd = _head_ones()
    for hp in range(d_rwkv // LANES):
        sl = pl.ds(hp * LANES, LANES)
        y = yf_ref[:, sl] + yb_ref[:, sl]
        mean = _head_sum(y, ones_bd) * (1.0 / HEAD_DIM)
        yc = y - mean
        var = _head_sum(yc * yc, ones_bd) * (1.0 / HEAD_DIM)
        yn = yc * lax.rsqrt(var + GN_EPS) * lnw_ref[:, sl] + lnb_ref[:, sl]
        yr_ref[:, sl] = ((yn + bonus_ref[:, sl]) * gate_ref[:, sl]).astype(BF16)
    l1 = l1_ref[...]
    l2 = l2_ref[...]
    l3 = l3_ref[...]
    m = jnp.maximum(jnp.maximum(l1, l2), l3)
    e1 = jnp.exp(l1 - m)
    e2 = jnp.exp(l2 - m)
    e3 = jnp.exp(l3 - m)
    ya = (e1 * o1_ref[...] + e2 * o2_ref[...] + e3 * o3_ref[...]) / (e1 + e2 + e3)
    acc = jnp.dot(yr_ref[...], w_ref[0:d_rwkv, :], preferred_element_type=F32)
    acc += jnp.dot(ya.astype(BF16), w_ref[d_rwkv:, :], preferred_element_type=F32)
    out_ref[...] = x_ref[...] + acc


def _out_proj(x, yf, yb, bonus, gate, ln_w, ln_b, outs, lses, w_out, tm):
    t, d = x.shape
    d_rwkv = yf.shape[1]
    d_attn = outs[0].shape[1]
    row = lambda n: pl.BlockSpec((tm, n), lambda i: (i, 0))
    const = lambda a: pl.BlockSpec(a.shape, lambda i: (0, 0))
    return pl.pallas_call(
        functools.partial(_out_proj_kernel, d_rwkv=d_rwkv),
        grid=(t // tm,),
        in_specs=[row(d)] + [row(d_rwkv)] * 4 + [const(ln_w), const(ln_b)]
                 + [row(d_attn)] * 6 + [const(w_out)],
        out_specs=row(d),
        out_shape=jax.ShapeDtypeStruct((t, d), F32),
        scratch_shapes=[pltpu.VMEM((tm, d_rwkv), BF16)],
        compiler_params=_cparams(("arbitrary",)),
        name="merge_out_proj",
    )(x, yf, yb, bonus, gate, ln_w, ln_b, *outs, *lses, w_out)


def _router_kernel(x_ref, g_ref, w_ref, b_ref, h_ref, idx_ref, wt_ref):
    x = x_ref[...]
    ms = jnp.mean(x * x, axis=-1, keepdims=True)
    h = x * lax.rsqrt(ms + RMS_EPS) * g_ref[...]
    h_ref[...] = h
    h_hi, h_lo = _split2(h)
    w_hi = w_ref[0]
    w_lo = w_ref[1]
    logits = (jnp.dot(h_hi, w_hi, preferred_element_type=F32)
              + jnp.dot(h_hi, w_lo, preferred_element_type=F32)
              + jnp.dot(h_lo, w_hi, preferred_element_type=F32)) + b_ref[...]
    lane = lax.broadcasted_iota(jnp.int32, logits.shape, 1).astype(F32)
    big = jnp.float32(LANES)
    is_g = lane < N_GROUPS
    gl = jnp.where(is_g, logits, NEG_BIG)
    gmax = jnp.max(gl, axis=-1, keepdims=True)
    gsel = jnp.min(jnp.where(is_g & (gl == gmax), lane, big), axis=-1, keepdims=True)
    g1 = 1.0 / jnp.sum(jnp.where(is_g, jnp.exp(gl - gmax), 0.0), axis=-1, keepdims=True)
    lo = N_GROUPS + gsel * EXPERTS_PER_GROUP
    in_grp = (lane >= lo) & (lane < lo + EXPERTS_PER_GROUP)
    el = jnp.where(in_grp, logits, NEG_BIG)
    v1 = jnp.max(el, axis=-1, keepdims=True)
    i1 = jnp.min(jnp.where(in_grp & (el == v1), lane, big), axis=-1, keepdims=True)
    rest = in_grp & (lane != i1)
    el2 = jnp.where(rest, logits, NEG_BIG)
    v2 = jnp.max(el2, axis=-1, keepdims=True)
    i2 = jnp.min(jnp.where(rest & (el2 == v2), lane, big), axis=-1, keepdims=True)
    e2 = jnp.exp(v2 - v1)
    ww1 = 1.0 / (1.0 + e2)
    ww2 = e2 / (1.0 + e2)
    idx = jnp.where(lane == 0, i1 - N_GROUPS, jnp.where(lane == 1, i2 - N_GROUPS, 0.0))
    idx_ref[...] = idx.astype(jnp.int32)
    wt_ref[...] = jnp.where(lane == 0, g1 * ww1, jnp.where(lane == 1, g1 * ww2, 0.0))


def _router(x, g, w_split, b_pad, tm):
    t, d = x.shape
    return pl.pallas_call(
        _router_kernel,
        grid=(t // tm,),
        in_specs=[pl.BlockSpec((tm, d), lambda i: (i, 0)),
                  pl.BlockSpec((1, d), lambda i: (0, 0)),
                  pl.BlockSpec((2, d, LANES), lambda i: (0, 0, 0)),
                  pl.BlockSpec((1, LANES), lambda i: (0, 0))],
        out_specs=[pl.BlockSpec((tm, d), lambda i: (i, 0)),
                   pl.BlockSpec((tm, LANES), lambda i: (i, 0)),
                   pl.BlockSpec((tm, LANES), lambda i: (i, 0))],
        out_shape=[jax.ShapeDtypeStruct((t, d), F32),
                   jax.ShapeDtypeStruct((t, LANES), jnp.int32),
                   jax.ShapeDtypeStruct((t, LANES), F32)],
        compiler_params=_cparams(("arbitrary",)),
        name="router",
    )(x, g, w_split, b_pad)


def _moe_kernel(te_ref, nt_ref, rows_ref, h_hbm, wg_ref, wu_ref, wd_ref, o_ref,
                xg_ref, xb_ref, sem, *, tm):
    i = pl.program_id(0)
    f = pl.program_id(1)
    nf = pl.num_programs(1)
    active = i < nt_ref[0]

    def row_copy(j):
        src = h_hbm.at[pl.ds(rows_ref[i * tm + j], 1), :]
        return pltpu.make_async_copy(src, xg_ref.at[pl.ds(j, 1), :], sem)

    @pl.when(active & (f == 0))
    def _():
        def start(j, carry):
            row_copy(j).start()
            return carry
        lax.fori_loop(0, tm, start, 0)

        def wait(j, carry):
            row_copy(j).wait()
            return carry
        lax.fori_loop(0, tm, wait, 0)
        xb_ref[...] = xg_ref[...].astype(BF16)

    @pl.when(active)
    def _():
        xb = xb_ref[...]
        gate = jnp.dot(xb, wg_ref[...].astype(BF16), preferred_element_type=F32)
        up = jnp.dot(xb, wu_ref[...].astype(BF16), preferred_element_type=F32)
        hid = (gate * _sigmoid(gate)) * up
        part = jnp.dot(hid.astype(BF16), wd_ref[...].astype(BF16), preferred_element_type=F32)

        @pl.when(f == 0)
        def _():
            o_ref[...] = part

        @pl.when(f != 0)
        def _():
            o_ref[...] += part

    @pl.when(jnp.logical_not(active) & (f == 0))
    def _():
        o_ref[...] = jnp.zeros_like(o_ref)


def _moe(h, tile_expert, n_tiles, rows, w_gate, w_up, w_down, tm, tf):
    t, d = h.shape
    n_e, _, d_e = w_gate.shape
    nt_max = tile_expert.shape[0]
    nf = d_e // tf

    def fidx(i, f, nt):
        return jnp.where(i < nt[0], f, nf - 1)

    grid_spec = pltpu.PrefetchScalarGridSpec(
        num_scalar_prefetch=3,
        grid=(nt_max, d_e // tf),
        in_specs=[pl.BlockSpec(memory_space=pl.ANY),
                  pl.BlockSpec((None, d, tf), lambda i, f, te, nt, rw: (te[i], 0, fidx(i, f, nt))),
                  pl.BlockSpec((None, d, tf), lambda i, f, te, nt, rw: (te[i], 0, fidx(i, f, nt))),
                  pl.BlockSpec((None, tf, d), lambda i, f, te, nt, rw: (te[i], fidx(i, f, nt), 0))],
        out_specs=pl.BlockSpec((tm, d), lambda i, f, te, nt, rw: (i, 0)),
        scratch_shapes=[pltpu.VMEM((tm, d), F32), pltpu.VMEM((tm, d), BF16),
                        pltpu.SemaphoreType.DMA(())],
    )
    return pl.pallas_call(
        functools.partial(_moe_kernel, tm=tm),
        grid_spec=grid_spec,
        out_shape=jax.ShapeDtypeStruct((nt_max * tm, d), F32),
        compiler_params=_cparams(("arbitrary", "arbitrary")),
        name="grouped_expert_mlp",
    )(tile_expert, n_tiles, rows, h, w_gate, w_up, w_down)


def _final_kernel(s0_ref, s1_ref, x_ref, wt_ref, g_ref, y_hbm, o_ref, y0_ref, y1_ref, sem, *, tm):
    i = pl.program_id(0)

    def copies(j):
        c0 = pltpu.make_async_copy(y_hbm.at[pl.ds(s0_ref[i * tm + j], 1), :],
                                   y0_ref.at[pl.ds(j, 1), :], sem.at[0])
        c1 = pltpu.make_async_copy(y_hbm.at[pl.ds(s1_ref[i * tm + j], 1), :],
                                   y1_ref.at[pl.ds(j, 1), :], sem.at[1])
        return c0, c1

    def start(j, carry):
        c0, c1 = copies(j)
        c0.start()
        c1.start()
        return carry
    lax.fori_loop(0, tm, start, 0)

    def wait(j, carry):
        c0, c1 = copies(j)
        c0.wait()
        c1.wait()
        return carry
    lax.fori_loop(0, tm, wait, 0)

    wt = wt_ref[...]
    x = x_ref[...] + wt[:, 0:1] * y0_ref[...] + wt[:, 1:2] * y1_ref[...]
    ms = jnp.mean(x * x, axis=-1, keepdims=True)
    o_ref[...] = x * lax.rsqrt(ms + RMS_EPS) * g_ref[...]


def _final(slot0, slot1, x, wts, g, y_sorted, tm):
    t, d = x.shape
    grid_spec = pltpu.PrefetchScalarGridSpec(
        num_scalar_prefetch=2,
        grid=(t // tm,),
        in_specs=[pl.BlockSpec((tm, d), lambda i, s0, s1: (i, 0)),
                  pl.BlockSpec((tm, LANES), lambda i, s0, s1: (i, 0)),
                  pl.BlockSpec((1, d), lambda i, s0, s1: (0, 0)),
                  pl.BlockSpec(memory_space=pl.ANY)],
        out_specs=pl.BlockSpec((tm, d), lambda i, s0, s1: (i, 0)),
        scratch_shapes=[pltpu.VMEM((tm, d), F32), pltpu.VMEM((tm, d), F32),
                        pltpu.SemaphoreType.DMA((2,))],
    )
    return pl.pallas_call(
        functools.partial(_final_kernel, tm=tm),
        grid_spec=grid_spec,
        out_shape=jax.ShapeDtypeStruct((t, d), F32),
        compiler_params=_cparams(("arbitrary",)),
        name="combine_final_norm",
    )(slot0, slot1, x, wts, g, y_sorted)


def _dispatch_plan(experts, tm):
    t = experts.shape[0]
    ef = experts.T.reshape(-1)
    onehot = (ef[:, None] == jnp.arange(N_EXPERTS, dtype=jnp.int32)[None, :]).astype(jnp.int32)
    csum = jnp.cumsum(onehot, axis=0)
    rank = jnp.take_along_axis(csum, ef[:, None], axis=1)[:, 0] - 1
    counts = csum[-1]
    tiles_e = (counts + tm - 1) // tm
    tile_end = jnp.cumsum(tiles_e)
    tile_start = tile_end - tiles_e
    n_tiles = tile_end[-1]
    nt_max = (2 * t) // tm + N_EXPERTS
    slot = tile_start[ef] * tm + rank
    tid = jnp.minimum(jnp.arange(nt_max, dtype=jnp.int32), n_tiles - 1)
    tile_expert = jnp.sum((tile_end[None, :] <= tid[:, None]).astype(jnp.int32), axis=1)
    tok = jnp.tile(jnp.arange(t, dtype=jnp.int32), 2)
    rows = jnp.zeros((nt_max * tm,), jnp.int32).at[slot].set(tok)
    return slot[:t], slot[t:], rows, tile_expert, n_tiles.reshape(1).astype(jnp.int32)


def kernel(x, positions, norm_mix, w_in, mu_shift, w0, w2, a0, a2, g2, k_k, k_a, r_k, ln_x_w, ln_x_b, w_out, norm_ffn, router_group_w, router_group_b, router_expert_w, router_expert_b, w_gate, w_up, w_down, norm_final):
    bsz, seq, d = x.shape
    assert bsz == 1
    depth = w_in.shape[0]
    d_rwkv = k_k.shape[1]
    d_attn = w_out.shape[1] - d_rwkv
    rwkv_cols = mu_shift.shape[2]
    nr = -(-rwkv_cols // (2 * LANES)) * (2 * LANES)
    assert 3 * d_rwkv + 2 * LANES + GATE_LORA == rwkv_cols and nr == 3 * d_rwkv + 4 * LANES
    tm_moe = 512
    tf_moe = 256

    xt = x[0]
    tabs = _rope_tables(positions[0][:, None], 512)
    for l in range(depth):
        w_r = jnp.pad(w_in[l][:, :rwkv_cols], ((0, 0), (0, nr - rwkv_cols))).astype(BF16)
        w_a = w_in[l][:, rwkv_cols:].astype(BF16)
        mu = jnp.pad(mu_shift[l], ((0, 0), (0, nr - rwkv_cols)))
        zl = jnp.zeros((DECAY_LORA, d_rwkv), F32)
        w2p = jnp.stack([jnp.concatenate([w2[l, 0], zl]), jnp.concatenate([zl, w2[l, 1]])]).astype(BF16)
        a2p = jnp.stack([jnp.concatenate([a2[l, 0], zl]), jnp.concatenate([zl, a2[l, 1]])]).astype(BF16)
        g2p = jnp.pad(g2[l], ((0, 2 * LANES - GATE_LORA), (0, 0))).astype(BF16)
        rk = r_k[l].reshape(1, d_rwkv)

        z_r = _norm_matmul(xt, norm_mix[l][None], w_r, 256)
        z_a = _norm_matmul(xt, norm_mix[l][None], w_a, 256)
        yf, yb, bonus, gate = _rwkv_mixer(z_r, mu, w0[l], a0[l], w2p, a2p, g2p,
                                          k_k[l][None], k_a[l][None], rk, d_rwkv)
        outs, lses = [], []
        for window, dil in DILATED_PATTERNS:
            o, lse = _banded_attention(z_a, tabs, dil, window // (2 * dil), d_attn)
            outs.append(o)
            lses.append(lse)
        x2 = _out_proj(xt, yf, yb, bonus, gate, ln_x_w[l][None], ln_x_b[l][None],
                       outs, lses, w_out[l].astype(BF16), 256)

        w_rt = jnp.concatenate(
            [router_group_w[l], router_expert_w[l].transpose(1, 0, 2).reshape(d, N_EXPERTS)], axis=1)
        w_rt = jnp.pad(w_rt, ((0, 0), (0, LANES - w_rt.shape[1])))
        rt_hi = w_rt.astype(BF16)
        rt_lo = (w_rt - rt_hi.astype(F32)).astype(BF16)
        b_rt = jnp.concatenate([router_group_b[l], router_expert_b[l].reshape(-1)])
        b_rt = jnp.pad(b_rt, (0, LANES - b_rt.shape[0]))[None]
        h, idx, wts = _router(x2, norm_ffn[l][None], jnp.stack([rt_hi, rt_lo]), b_rt, 256)

        slot0, slot1, rows, tile_expert, n_tiles = _dispatch_plan(idx[:, :2], tm_moe)
        y_sorted = _moe(h, tile_expert, n_tiles, rows, w_gate[l], w_up[l], w_down[l], tm_moe, tf_moe)
        is_last = l == depth - 1
        assert is_last, "the combine kernel applies the final norm; depth must be 1"
        xt = _final(slot0, slot1, x2, wts, norm_final[None], y_sorted, 256)
    return xt[None]
```

```python
import functools

import jax
import jax.numpy as jnp
from jax import lax
from jax.experimental import pallas as pl
from jax.experimental.pallas import tpu as pltpu

HEAD_DIM = 64
LANES = 128
SUBLANES = 8
DECAY_LORA = 64
ICLR_LORA = 64
GATE_LORA = 160
GN_EPS = 64e-5
RMS_EPS = 1e-6
ROPE_THETA = 500000.0
ROPE_DIM = HEAD_DIM // 4
DILATED_PATTERNS = ((128, 1), (512, 4), (2048, 16))
Q_BLOCK = 128
N_GROUPS = 4
EXPERTS_PER_GROUP = 8
N_EXPERTS = N_GROUPS * EXPERTS_PER_GROUP
NEG_BIG = -1e30
CHUNK = 64
VMEM_LIMIT = 56 * 1024 * 1024

BF16 = jnp.bfloat16
F32 = jnp.float32


def _cparams(sem):
    return pltpu.CompilerParams(dimension_semantics=sem, vmem_limit_bytes=VMEM_LIMIT)


def _dot(a, b):
    return jnp.dot(a.astype(BF16), b.astype(BF16), preferred_element_type=F32)


def _dot_nt(a, b):
    return lax.dot_general(a.astype(BF16), b.astype(BF16), (((1,), (1,)), ((), ())),
                           preferred_element_type=F32)


def _dot_tn(a, b):
    return lax.dot_general(a.astype(BF16), b.astype(BF16), (((0,), (0,)), ((), ())),
                           preferred_element_type=F32)


def _split2(x):
    hi = x.astype(BF16)
    lo = (x - hi.astype(F32)).astype(BF16)
    return hi, lo


def _split3(x):
    hi = x.astype(BF16)
    r1 = x - hi.astype(F32)
    mid = r1.astype(BF16)
    lo = (r1 - mid.astype(F32)).astype(BF16)
    return hi, mid, lo


def _head_ones():
    i = lax.broadcasted_iota(jnp.int32, (LANES, LANES), 0) // HEAD_DIM
    j = lax.broadcasted_iota(jnp.int32, (LANES, LANES), 1) // HEAD_DIM
    return (i == j).astype(BF16)


def _head_sum(x, ones_bd):
    hi, lo = _split2(x)
    return (jnp.dot(hi, ones_bd, preferred_element_type=F32)
            + jnp.dot(lo, ones_bd, preferred_element_type=F32))


def _sigmoid(x):
    return 1.0 / (1.0 + jnp.exp(-x))


def _norm_matmul_kernel(x_ref, g_ref, w_ref, o_ref):
    x = x_ref[...]
    ms = jnp.mean(x * x, axis=-1, keepdims=True)
    h = x * lax.rsqrt(ms + RMS_EPS) * g_ref[...]
    o_ref[...] = jnp.dot(h.astype(BF16), w_ref[...], preferred_element_type=F32)


def _norm_matmul(x, g, w, tm):
    t, d = x.shape
    n = w.shape[1]
    return pl.pallas_call(
        _norm_matmul_kernel,
        grid=(t // tm,),
        in_specs=[pl.BlockSpec((tm, d), lambda i: (i, 0)),
                  pl.BlockSpec((1, d), lambda i: (0, 0)),
                  pl.BlockSpec((d, n), lambda i: (0, 0))],
        out_specs=pl.BlockSpec((tm, n), lambda i: (i, 0)),
        out_shape=jax.ShapeDtypeStruct((t, n), F32),
        compiler_params=_cparams(("arbitrary",)),
        name="norm_in_proj",
    )(x, g, w)


def _rwkv_kernel(zf_ref, zfp_ref, zfn_ref, zb_ref, zbp_ref, zbn_ref,
                 mu_ref, w0_ref, a0_ref, w2_ref, a2_ref, g2_ref, kk_ref, ka_ref, rk_ref,
                 yf_ref, yb_ref, bonus_ref, gate_ref,
                 zs_ref, prep_ref, h_ref, *, nc, d_rwkv):
    c = pl.program_id(0)
    C = CHUNK
    n_hp = d_rwkv // LANES

    @pl.when(c == 0)
    def _():
        h_ref[...] = jnp.zeros_like(h_ref)

    row = lax.broadcasted_iota(jnp.int32, (C, 1), 0)
    mu_p = mu_ref[0:1, :]
    mu_n = mu_ref[1:2, :]
    srcs = ((zf_ref, zfp_ref, zfn_ref, c), (zb_ref, zbp_ref, zbn_ref, nc - 1 - c))
    for d, (z_ref, zp_ref, zn_ref, chunk) in enumerate(srcs):
        z = z_ref[...]
        prev_row = jnp.where(chunk == 0, 0.0, zp_ref[SUBLANES - 1:SUBLANES, :])
        next_row = jnp.where(chunk == nc - 1, 0.0, zn_ref[0:1, :])
        z_prev = jnp.where(row == 0, prev_row, pltpu.roll(z, 1, 0))
        z_next = jnp.where(row == C - 1, next_row, pltpu.roll(z, C - 1, 0))
        zs_ref[d] = z + mu_p * (z_prev - z) + mu_n * (z_next - z)

    ones_bd = _head_ones()
    lane = lax.broadcasted_iota(jnp.int32, (1, LANES), 1)
    m_a = lane < HEAD_DIM
    ti = lax.broadcasted_iota(jnp.int32, (C, C), 0)
    tj = lax.broadcasted_iota(jnp.int32, (C, C), 1)
    bi = lax.broadcasted_iota(jnp.int32, (2 * C, 2 * C), 0)
    bj = lax.broadcasted_iota(jnp.int32, (2 * C, 2 * C), 1)
    same_blk = (bi // C) == (bj // C)
    eye = bi == bj
    cum_mat = ((tj <= ti).astype(BF16), (tj >= ti).astype(BF16))
    strict = (same_blk & ((bj % C) < (bi % C)), same_blk & ((bj % C) > (bi % C)))
    incl = (same_blk & ((bj % C) <= (bi % C)), same_blk & ((bj % C) >= (bi % C)))
    last_row = (C - 1, 0)
    off_wl = 3 * d_rwkv
    off_al = off_wl + LANES
    off_gl = off_al + LANES

    def to_rows(x):
        return jnp.concatenate([x[:, s * LANES:(s + 1) * LANES] for s in range(n_hp)], axis=0)

    def to_lanes(x):
        return jnp.concatenate([x[s * C:(s + 1) * C] for s in range(n_hp)], axis=1)

    def bd(x):
        return jnp.concatenate([jnp.where(m_a, x, 0), jnp.where(m_a, 0, x)], axis=0)

    def unbd(x):
        return x[0:C] + x[C:2 * C]

    w_c = []
    for d in range(2):
        r = zs_ref[d, :, 0:d_rwkv]
        k = zs_ref[d, :, d_rwkv:2 * d_rwkv]
        kkr = k * kk_ref[...]
        sums = [to_rows(kkr * kkr)]
        if d == 0:
            sums.append(to_rows(r * k * rk_ref[...]))
        hs = _head_sum(jnp.concatenate(sums, axis=0), ones_bd)
        kk = kkr * lax.rsqrt(to_lanes(hs[0:n_hp * C]) + 1e-12)
        if d == 0:
            bonus_ref[...] = to_lanes(hs[n_hp * C:2 * n_hp * C]) * zs_ref[0, :, 2 * d_rwkv:3 * d_rwkv]
            gate_ref[...] = _dot(_sigmoid(zs_ref[0, :, off_gl:off_gl + 2 * LANES]), g2_ref[...])
        u = w0_ref[d:d + 1, :] + _dot(jnp.tanh(zs_ref[d, :, off_wl:off_wl + LANES]), w2_ref[d])
        w_log = -(jnp.maximum(-u, 0.0) + jnp.log1p(jnp.exp(-jnp.abs(u)))) - 0.5
        logd = -jnp.exp(w_log)
        a = _sigmoid(a0_ref[d:d + 1, :] + _dot(zs_ref[d, :, off_al:off_al + LANES], a2_ref[d]))
        kd = k * (1.0 + (a - 1.0) * ka_ref[...])
        b = kk * a
        l_hi, l_mid, l_lo = _split3(logd)
        cm = cum_mat[d]
        cum = (jnp.dot(cm, l_hi, preferred_element_type=F32)
               + jnp.dot(cm, l_mid, preferred_element_type=F32)
               + jnp.dot(cm, l_lo, preferred_element_type=F32))
        cum_c = cum[last_row[d]:last_row[d] + 1, :]
        w_inv = jnp.exp(-cum)
        w_c.append(jnp.exp(cum_c))
        w_end = w_c[d] * w_inv
        prep_ref[d, 0] = -kk * jnp.exp(cum - logd)
        prep_ref[d, 1] = r * jnp.exp(cum)
        prep_ref[d, 2] = b * w_inv
        prep_ref[d, 3] = kd * w_inv
        prep_ref[d, 4] = b * w_end
        prep_ref[d, 5] = kd * w_end

    blocks = [(d, hp) for hp in range(n_hp) for d in range(2)]

    def slab(d, i, hp):
        return prep_ref[d, i, :, hp * LANES:(hp + 1) * LANES]

    incl2 = tuple(jnp.concatenate([m, m], axis=1) for m in incl)
    eye_f = eye.astype(F32)
    xs, aks, lows, z0s, vbds = [], [], [], [], []
    for d, hp in blocks:
        at = bd(slab(d, 0, hp).astype(BF16))
        rt = bd(slab(d, 1, hp).astype(BF16))
        bt = slab(d, 2, hp).astype(BF16)
        kt = slab(d, 3, hp).astype(BF16)
        lhs = jnp.concatenate([at, rt], axis=0)
        rhs = jnp.concatenate([bt, bt, kt, kt], axis=0)
        g = lax.dot_general(lhs, rhs, (((1,), (1,)), ((), ())), preferred_element_type=F32)
        xs.append(jnp.where(strict[d], g[0:2 * C, 0:2 * C], 0.0))
        aks.append(jnp.where(strict[d], g[0:2 * C, 2 * C:4 * C], 0.0).astype(BF16))
        lows.append(jnp.where(incl2[d], g[2 * C:4 * C, :], 0.0).astype(BF16))
    for i, (d, hp) in enumerate(blocks):
        vbd = bd(zs_ref[d, :, 2 * d_rwkv + hp * LANES:2 * d_rwkv + (hp + 1) * LANES].astype(BF16))
        vbds.append(vbd)
        akv = jnp.dot(aks[i], vbd, preferred_element_type=F32)
        z0s.append(jnp.concatenate([bd(slab(d, 0, hp).astype(BF16)), akv.astype(BF16)], axis=1))

    n_sq = C.bit_length() - 1
    ts = [eye_f + x for x in xs]
    ps = [x.astype(BF16) for x in xs]
    for j in range(1, n_sq):
        last = j == n_sq - 1
        for i in range(len(blocks)):
            if j == 1:
                ps[i] = jnp.dot(ps[i], ps[i], preferred_element_type=F32).astype(BF16)
            tb = ts[i].astype(BF16)
            if last:
                ts[i] = ts[i] + jnp.dot(ps[i], tb, preferred_element_type=F32)
            else:
                res = jnp.dot(ps[i], jnp.concatenate([ps[i], tb], axis=1), preferred_element_type=F32)
                ts[i] = ts[i] + res[:, LANES:2 * LANES]
                ps[i] = res[:, 0:LANES].astype(BF16)

    wts = []
    for i in range(len(blocks)):
        zz = jnp.dot(ts[i].astype(BF16), z0s[i], preferred_element_type=F32)
        bottom = jnp.concatenate([jnp.zeros((2 * C, LANES), BF16), vbds[i]], axis=1)
        wts.append(jnp.concatenate([zz.astype(BF16), bottom], axis=0))
    outs = []
    for i, (d, hp) in enumerate(blocks):
        bb_t = bd(slab(d, 4, hp)).T.astype(BF16)
        kb_t = bd(slab(d, 5, hp)).T.astype(BF16)
        lhs = jnp.concatenate([lows[i], jnp.concatenate([bb_t, kb_t], axis=1)], axis=0)
        outs.append(jnp.dot(lhs, wts[i], preferred_element_type=F32))
    for i, (d, hp) in enumerate(blocks):
        sl = pl.ds(hp * LANES, LANES)
        o = outs[i]
        r_hat = bd(slab(d, 1, hp)) + o[0:2 * C, 0:LANES]
        p_mat = eye_f * w_c[d][:, hp * LANES:(hp + 1) * LANES] + o[2 * C:4 * C, 0:LANES]
        lhs = jnp.concatenate([r_hat, p_mat], axis=0).astype(BF16)
        res = jnp.dot(lhs, h_ref[d, hp].astype(BF16), preferred_element_type=F32)
        y = unbd(res[0:2 * C] + o[0:2 * C, LANES:2 * LANES])
        h_ref[d, hp] = res[2 * C:4 * C] + o[2 * C:4 * C, LANES:2 * LANES]
        if d == 0:
            yf_ref[:, sl] = y
        else:
            yb_ref[:, sl] = y


def _rwkv_mixer(z, mu, w0, a0, w2p, a2p, g2p, k_k, k_a, r_k, d_rwkv):
    t, nr = z.shape
    C = CHUNK
    nc = t // C
    hb = C // SUBLANES
    n_hp = d_rwkv // LANES
    last_hblk = t // SUBLANES - 1

    def cur(fn):
        return pl.BlockSpec((C, nr), lambda c: (fn(c), 0))

    def prev(fn):
        return pl.BlockSpec((SUBLANES, nr), lambda c: (jnp.maximum(fn(c) * hb - 1, 0), 0))

    def nxt(fn):
        return pl.BlockSpec((SUBLANES, nr), lambda c: (jnp.minimum((fn(c) + 1) * hb, last_hblk), 0))

    fwd = lambda c: c
    bwd = lambda c: nc - 1 - c

    def full(a):
        nd = a.ndim
        return pl.BlockSpec(a.shape, lambda c: (0,) * nd)

    out_spec_f = pl.BlockSpec((C, d_rwkv), lambda c: (c, 0))
    out_spec_b = pl.BlockSpec((C, d_rwkv), lambda c: (nc - 1 - c, 0))
    out_sd = jax.ShapeDtypeStruct((t, d_rwkv), F32)
    params = (mu, w0, a0, w2p, a2p, g2p, k_k, k_a, r_k)
    return pl.pallas_call(
        functools.partial(_rwkv_kernel, nc=nc, d_rwkv=d_rwkv),
        grid=(nc,),
        in_specs=[cur(fwd), prev(fwd), nxt(fwd), cur(bwd), prev(bwd), nxt(bwd)]
                 + [full(p) for p in params],
        out_specs=[out_spec_f, out_spec_b, out_spec_f, out_spec_f],
        out_shape=[out_sd, out_sd, out_sd, out_sd],
        scratch_shapes=[pltpu.VMEM((2, C, nr), F32),
                        pltpu.VMEM((2, 6, C, d_rwkv), F32),
                        pltpu.VMEM((2, n_hp, LANES, LANES), F32)],
        compiler_params=_cparams(("arbitrary",)),
        name="rwkv7_chunk_scan",
    )(z, z, z, z, z, z, *params)


def _rope_table_kernel(pos_ref, inv_ref, c_ref, s1_ref, s2_ref):
    half = ROPE_DIM // 2
    pos = pos_ref[...].astype(F32)
    j = lax.broadcasted_iota(jnp.int32, (1, LANES), 1) % HEAD_DIM
    ang = pos * inv_ref[...]
    cs = jnp.cos(ang)
    sn = jnp.sin(ang)
    c_ref[...] = jnp.where(j < ROPE_DIM, cs, 1.0)
    s1_ref[...] = jnp.where(j < half, -sn, 0.0)
    s2_ref[...] = jnp.where((j >= half) & (j < ROPE_DIM), sn, 0.0)


def _rope_tables(positions, tm):
    t = positions.shape[0]
    half = ROPE_DIM // 2
    inv_freq = jnp.power(ROPE_THETA, -jnp.arange(half, dtype=F32) * 2.0 / ROPE_DIM)
    inv_lane = jnp.tile(inv_freq, LANES // half)[None]
    sd = jax.ShapeDtypeStruct((t, LANES), F32)
    spec = pl.BlockSpec((tm, LANES), lambda i: (i, 0))
    return pl.pallas_call(
        _rope_table_kernel,
        grid=(t // tm,),
        in_specs=[pl.BlockSpec((tm, 1), lambda i: (i, 0)),
                  pl.BlockSpec((1, LANES), lambda i: (0, 0))],
        out_specs=[spec, spec, spec],
        out_shape=[sd, sd, sd],
        compiler_params=_cparams(("arbitrary",)),
        name="rope_tables",
    )(positions, inv_lane)


def _rot(x, c, s1, s2):
    half = ROPE_DIM // 2
    return x * c + pltpu.roll(x, LANES - half, 1) * s1 + pltpu.roll(x, half, 1) * s2


def _norm_matmul_rope_kernel(x_ref, g_ref, w_ref, c_ref, s1_ref, s2_ref, o_ref, *, n_rot):
    x = x_ref[...]
    ms = jnp.mean(x * x, axis=-1, keepdims=True)
    h = x * lax.rsqrt(ms + RMS_EPS) * g_ref[...]
    z = jnp.dot(h.astype(BF16), w_ref[...], preferred_element_type=F32)
    c = c_ref[...]
    s1 = s1_ref[...]
    s2 = s2_ref[...]
    for s in range(z.shape[1] // LANES):
        sl = slice(s * LANES, (s + 1) * LANES)
        zs = z[:, sl]
        if s < n_rot:
            zs = _rot(zs, c, s1, s2)
            if s < n_rot // 2:
                zs = zs * (HEAD_DIM ** -0.5)
        o_ref[:, sl] = zs


def _norm_matmul_rope(x, g, w, tabs, tm):
    t, d = x.shape
    n = w.shape[1]
    tab = pl.BlockSpec((tm, LANES), lambda i: (i, 0))
    return pl.pallas_call(
        functools.partial(_norm_matmul_rope_kernel, n_rot=2 * (n // 3) // LANES),
        grid=(t // tm,),
        in_specs=[pl.BlockSpec((tm, d), lambda i: (i, 0)),
                  pl.BlockSpec((1, d), lambda i: (0, 0)),
                  pl.BlockSpec((d, n), lambda i: (0, 0)), tab, tab, tab],
        out_specs=pl.BlockSpec((tm, n), lambda i: (i, 0)),
        out_shape=jax.ShapeDtypeStruct((t, n), F32),
        compiler_params=_cparams(("arbitrary",)),
        name="norm_in_proj_rope",
    )(x, g, w, *tabs)


def _attn_kernel(q_ref, k_ref, v_ref, o_ref, op_ref, lp_ref, *, seq_t, sb):
    jsb = pl.program_id(1)
    lane = lax.broadcasted_iota(jnp.int32, (1, LANES), 1)
    m_a = lane < HEAD_DIM
    for p, (window, dil) in enumerate(DILATED_PATTERNS):
        radius = window // (2 * dil)
        win = Q_BLOCK + 2 * radius
        seq = seq_t // dil
        nq = sb // (dil * Q_BLOCK)
        qi = lax.broadcasted_iota(jnp.int32, (Q_BLOCK, win), 0)
        kj = lax.broadcasted_iota(jnp.int32, (Q_BLOCK, win), 1)

        def rows(start, size, dil=dil):
            return pl.ds(start, size) if dil == 1 else pl.ds(start, size, stride=dil)

        def body(idx, carry, dil=dil, radius=radius, win=win, seq=seq, nq=nq, qi=qi, kj=kj,
                 rows=rows, p=p):
            r = idx // nq
            jj = idx % nq
            q_l0 = (jsb * nq + jj) * Q_BLOCK
            start_l = jnp.clip(q_l0 - radius, 0, seq - win)
            q_rows = rows(r + dil * Q_BLOCK * jj, Q_BLOCK)
            k_rows = rows(r + dil * start_l, win)
            q = q_ref[q_rows, :].astype(BF16)
            kw = k_ref[k_rows, :].astype(BF16)
            vw = v_ref[k_rows, :].astype(BF16)
            valid = jnp.abs((q_l0 + qi) - (start_l + kj)) <= radius
            one = jnp.ones((), BF16)
            halves = ((jnp.where(m_a, q, 0), jnp.where(m_a, vw, one)),
                      (jnp.where(m_a, 0, q), jnp.where(m_a, one, vw)))
            res, mx = [], []
            for qh, vh in halves:
                sc = lax.dot_general(qh, kw, (((1,), (1,)), ((), ())), preferred_element_type=F32)
                sc = jnp.where(valid, sc, NEG_BIG)
                m = jnp.max(sc, axis=-1, keepdims=True)
                pexp = jnp.exp(sc - m).astype(BF16)
                res.append(jnp.dot(pexp, vh, preferred_element_type=F32))
                mx.append(m)
            num = jnp.where(m_a, res[0], res[1])
            den = pltpu.roll(jnp.where(m_a, res[1], res[0]), HEAD_DIM, 1)
            op_ref[p, q_rows, :] = num / den
            lp_ref[p, q_rows, :] = jnp.where(m_a, mx[0], mx[1]) + jnp.log(den)
            return carry

        lax.fori_loop(0, dil * nq, body, 0, unroll=4)

    l1 = lp_ref[0]
    l2 = lp_ref[1]
    l3 = lp_ref[2]
    m = jnp.maximum(jnp.maximum(l1, l2), l3)
    e1 = jnp.exp(l1 - m)
    e2 = jnp.exp(l2 - m)
    e3 = jnp.exp(l3 - m)
    o_ref[...] = (e1 * op_ref[0] + e2 * op_ref[1] + e3 * op_ref[2]) / (e1 + e2 + e3)


def _dilated_attention(z_attn, d_attn, sb):
    t = z_attn.shape[0]
    n_hp = d_attn // LANES
    n_pat = len(DILATED_PATTERNS)
    return pl.pallas_call(
        functools.partial(_attn_kernel, seq_t=t, sb=sb),
        grid=(n_hp, t // sb),
        in_specs=[pl.BlockSpec((sb, LANES), lambda h, j: (j, h)),
                  pl.BlockSpec((t, LANES), lambda h, j: (0, n_hp + h)),
                  pl.BlockSpec((t, LANES), lambda h, j: (0, 2 * n_hp + h))],
        out_specs=pl.BlockSpec((sb, LANES), lambda h, j: (j, h)),
        out_shape=jax.ShapeDtypeStruct((t, d_attn), F32),
        scratch_shapes=[pltpu.VMEM((n_pat, sb, LANES), F32), pltpu.VMEM((n_pat, sb, LANES), F32)],
        compiler_params=_cparams(("arbitrary", "arbitrary")),
        name="dilated_attention",
    )(z_attn, z_attn, z_attn)


def _out_proj_kernel(x_ref, yf_ref, yb_ref, bonus_ref, gate_ref, lnw_ref, lnb_ref,
                     ya_ref, w_ref, out_ref, yr_ref, *, d_rwkv):
    ones_bd = _head_ones()
    for hp in range(d_rwkv // LANES):
        sl = pl.ds(hp * LANES, LANES)
        y = yf_ref[:, sl] + yb_ref[:, sl]
        mean = _head_sum(y, ones_bd) * (1.0 / HEAD_DIM)
        yc = y - mean
        var = _head_sum(yc * yc, ones_bd) * (1.0 / HEAD_DIM)
        yn = yc * lax.rsqrt(var + GN_EPS) * lnw_ref[:, sl] + lnb_ref[:, sl]
        yr_ref[:, sl] = ((yn + bonus_ref[:, sl]) * gate_ref[:, sl]).astype(BF16)
    acc = jnp.dot(yr_ref[...], w_ref[0:d_rwkv, :], preferred_element_type=F32)
    acc += jnp.dot(ya_ref[...].astype(BF16), w_ref[d_rwkv:, :], preferred_element_type=F32)
    out_ref[...] = x_ref[...] + acc


def _out_proj(x, yf, yb, bonus, gate, ln_w, ln_b, y_attn, w_out, tm):
    t, d = x.shape
    d_rwkv = yf.shape[1]
    d_attn = y_attn.shape[1]
    row = lambda n: pl.BlockSpec((tm, n), lambda i: (i, 0))
    const = lambda a: pl.BlockSpec(a.shape, lambda i: (0, 0))
    return pl.pallas_call(
        functools.partial(_out_proj_kernel, d_rwkv=d_rwkv),
        grid=(t // tm,),
        in_specs=[row(d)] + [row(d_rwkv)] * 4 + [const(ln_w), const(ln_b)]
                 + [row(d_attn), const(w_out)],
        out_specs=row(d),
        out_shape=jax.ShapeDtypeStruct((t, d), F32),
        scratch_shapes=[pltpu.VMEM((tm, d_rwkv), BF16)],
        compiler_params=_cparams(("arbitrary",)),
        name="merge_out_proj",
    )(x, yf, yb, bonus, gate, ln_w, ln_b, y_attn, w_out)


def _router_kernel(x_ref, g_ref, w_ref, b_ref, h_ref, idx_ref, wt_ref):
    x = x_ref[...]
    ms = jnp.mean(x * x, axis=-1, keepdims=True)
    h = x * lax.rsqrt(ms + RMS_EPS) * g_ref[...]
    h_ref[...] = h
    h_hi, h_lo = _split2(h)
    w_hi = w_ref[0]
    w_lo = w_ref[1]
    logits = (jnp.dot(h_hi, w_hi, preferred_element_type=F32)
              + jnp.dot(h_hi, w_lo, preferred_element_type=F32)
              + jnp.dot(h_lo, w_hi, preferred_element_type=F32)) + b_ref[...]
    lane = lax.broadcasted_iota(jnp.int32, logits.shape, 1).astype(F32)
    big = jnp.float32(LANES)
    is_g = lane < N_GROUPS
    gl = jnp.where(is_g, logits, NEG_BIG)
    gmax = jnp.max(gl, axis=-1, keepdims=True)
    gsel = jnp.min(jnp.where(is_g & (gl == gmax), lane, big), axis=-1, keepdims=True)
    g1 = 1.0 / jnp.sum(jnp.where(is_g, jnp.exp(gl - gmax), 0.0), axis=-1, keepdims=True)
    lo = N_GROUPS + gsel * EXPERTS_PER_GROUP
    in_grp = (lane >= lo) & (lane < lo + EXPERTS_PER_GROUP)
    el = jnp.where(in_grp, logits, NEG_BIG)
    v1 = jnp.max(el, axis=-1, keepdims=True)
    i1 = jnp.min(jnp.where(in_grp & (el == v1), lane, big), axis=-1, keepdims=True)
    rest = in_grp & (lane != i1)
    el2 = jnp.where(rest, logits, NEG_BIG)
    v2 = jnp.max(el2, axis=-1, keepdims=True)
    i2 = jnp.min(jnp.where(rest & (el2 == v2), lane, big), axis=-1, keepdims=True)
    e2 = jnp.exp(v2 - v1)
    ww1 = 1.0 / (1.0 + e2)
    ww2 = e2 / (1.0 + e2)
    idx = jnp.where(lane == 0, i1 - N_GROUPS, jnp.where(lane == 1, i2 - N_GROUPS, 0.0))
    idx_ref[...] = idx.astype(jnp.int32)
    wt_ref[...] = jnp.where(lane == 0, g1 * ww1, jnp.where(lane == 1, g1 * ww2, 0.0))


def _router(x, g, w_split, b_pad, tm):
    t, d = x.shape
    return pl.pallas_call(
        _router_kernel,
        grid=(t // tm,),
        in_specs=[pl.BlockSpec((tm, d), lambda i: (i, 0)),
                  pl.BlockSpec((1, d), lambda i: (0, 0)),
                  pl.BlockSpec((2, d, LANES), lambda i: (0, 0, 0)),
                  pl.BlockSpec((1, LANES), lambda i: (0, 0))],
        out_specs=[pl.BlockSpec((tm, d), lambda i: (i, 0)),
                   pl.BlockSpec((tm, LANES), lambda i: (i, 0)),
                   pl.BlockSpec((tm, LANES), lambda i: (i, 0))],
        out_shape=[jax.ShapeDtypeStruct((t, d), F32),
                   jax.ShapeDtypeStruct((t, LANES), jnp.int32),
                   jax.ShapeDtypeStruct((t, LANES), F32)],
        compiler_params=_cparams(("arbitrary",)),
        name="router",
    )(x, g, w_split, b_pad)


def _moe_kernel(te_ref, nt_ref, rows_ref, h_hbm, wg_ref, wu_ref, wd_ref, o_ref,
                xg_ref, xb_ref, sem, *, tm, nf, nt_max):
    i = pl.program_id(0)
    f = pl.program_id(1)
    nt = nt_ref[0]
    active = i < nt
    slot = i % 2
    share = tm // nf

    def row_copy(tile, j, buf):
        src = h_hbm.at[pl.ds(rows_ref[tile * tm + j], 1), :]
        return pltpu.make_async_copy(src, xg_ref.at[buf, pl.ds(j, 1), :], sem.at[buf])

    def wait_tile(buf):
        pltpu.make_async_copy(h_hbm.at[pl.ds(0, tm), :], xg_ref.at[buf], sem.at[buf]).wait()

    @pl.when((i == 0) & (f == 0))
    def _():
        def start(j, carry):
            row_copy(0, j, 0).start()
            return carry
        lax.fori_loop(0, tm, start, 0)

    @pl.when((f == 0) & (i <= nt))
    def _():
        wait_tile(slot)

    @pl.when(active & (f == 0))
    def _():
        xb_ref[...] = xg_ref[slot].astype(BF16)

    @pl.when(active)
    def _():
        for jj in range(share):
            row_copy(i + 1, f * share + jj, 1 - slot).start()
        xb = xb_ref[...]
        gate = jnp.dot(xb, wg_ref[...].astype(BF16), preferred_element_type=F32)
        up = jnp.dot(xb, wu_ref[...].astype(BF16), preferred_element_type=F32)
        hid = (gate * _sigmoid(gate)) * up
        part = jnp.dot(hid.astype(BF16), wd_ref[...].astype(BF16), preferred_element_type=F32)

        @pl.when(f == 0)
        def _():
            o_ref[...] = part

        @pl.when(f != 0)
        def _():
            o_ref[...] += part

    @pl.when(jnp.logical_not(active) & (f == 0))
    def _():
        o_ref[...] = jnp.zeros_like(o_ref)

    @pl.when((i == nt_max - 1) & (f == nf - 1) & (nt == nt_max))
    def _():
        wait_tile(nt_max % 2)


def _moe(h, tile_expert, n_tiles, rows, w_gate, w_up, w_down, tm, tf):
    t, d = h.shape
    n_e, _, d_e = w_gate.shape
    nt_max = tile_expert.shape[0]
    nf = d_e // tf
    assert rows.shape[0] == (nt_max + 1) * tm

    def fidx(i, f, nt):
        return jnp.where(i < nt[0], f, nf - 1)

    grid_spec = pltpu.PrefetchScalarGridSpec(
        num_scalar_prefetch=3,
        grid=(nt_max, d_e // tf),
        in_specs=[pl.BlockSpec(memory_space=pl.ANY),
                  pl.BlockSpec((None, d, tf), lambda i, f, te, nt, rw: (te[i], 0, fidx(i, f, nt))),
                  pl.BlockSpec((None, d, tf), lambda i, f, te, nt, rw: (te[i], 0, fidx(i, f, nt))),
                  pl.BlockSpec((None, tf, d), lambda i, f, te, nt, rw: (te[i], fidx(i, f, nt), 0))],
        out_specs=pl.BlockSpec((tm, d), lambda i, f, te, nt, rw: (i, 0)),
        scratch_shapes=[pltpu.VMEM((2, tm, d), F32), pltpu.VMEM((tm, d), BF16),
                        pltpu.SemaphoreType.DMA((2,))],
    )
    return pl.pallas_call(
        functools.partial(_moe_kernel, tm=tm, nf=nf, nt_max=nt_max),
        grid_spec=grid_spec,
        out_shape=jax.ShapeDtypeStruct((nt_max * tm, d), F32),
        compiler_params=_cparams(("arbitrary", "arbitrary")),
        name="grouped_expert_mlp",
    )(tile_expert, n_tiles, rows, h, w_gate, w_up, w_down)


def _final_kernel(s0_ref, s1_ref, x_ref, wt_ref, g_ref, y_hbm, o_ref, y0_ref, y1_ref, sem, *, tm):
    i = pl.program_id(0)

    def copies(j):
        c0 = pltpu.make_async_copy(y_hbm.at[pl.ds(s0_ref[i * tm + j], 1), :],
                                   y0_ref.at[pl.ds(j, 1), :], sem.at[0])
        c1 = pltpu.make_async_copy(y_hbm.at[pl.ds(s1_ref[i * tm + j], 1), :],
                                   y1_ref.at[pl.ds(j, 1), :], sem.at[1])
        return c0, c1

    def start(j, carry):
        c0, c1 = copies(j)
        c0.start()
        c1.start()
        return carry
    lax.fori_loop(0, tm, start, 0)

    def wait(j, carry):
        c0, c1 = copies(j)
        c0.wait()
        c1.wait()
        return carry
    lax.fori_loop(0, tm, wait, 0)

    wt = wt_ref[...]
    x = x_ref[...] + wt[:, 0:1] * y0_ref[...] + wt[:, 1:2] * y1_ref[...]
    ms = jnp.mean(x * x, axis=-1, keepdims=True)
    o_ref[...] = x * lax.rsqrt(ms + RMS_EPS) * g_ref[...]


def _final(slot0, slot1, x, wts, g, y_sorted, tm):
    t, d = x.shape
    grid_spec = pltpu.PrefetchScalarGridSpec(
        num_scalar_prefetch=2,
        grid=(t // tm,),
        in_specs=[pl.BlockSpec((tm, d), lambda i, s0, s1: (i, 0)),
                  pl.BlockSpec((tm, LANES), lambda i, s0, s1: (i, 0)),
                  pl.BlockSpec((1, d), lambda i, s0, s1: (0, 0)),
                  pl.BlockSpec(memory_space=pl.ANY)],
        out_specs=pl.BlockSpec((tm, d), lambda i, s0, s1: (i, 0)),
        scratch_shapes=[pltpu.VMEM((tm, d), F32), pltpu.VMEM((tm, d), F32),
                        pltpu.SemaphoreType.DMA((2,))],
    )
    return pl.pallas_call(
        functools.partial(_final_kernel, tm=tm),
        grid_spec=grid_spec,
        out_shape=jax.ShapeDtypeStruct((t, d), F32),
        compiler_params=_cparams(("arbitrary",)),
        name="combine_final_norm",
    )(slot0, slot1, x, wts, g, y_sorted)


def _dispatch_plan(experts, tm):
    t = experts.shape[0]
    ef = experts.T.reshape(-1)
    onehot = (ef[:, None] == jnp.arange(N_EXPERTS, dtype=jnp.int32)[None, :]).astype(jnp.int32)
    csum = jnp.cumsum(onehot, axis=0)
    rank = jnp.take_along_axis(csum, ef[:, None], axis=1)[:, 0] - 1
    counts = csum[-1]
    tiles_e = (counts + tm - 1) // tm
    tile_end = jnp.cumsum(tiles_e)
    tile_start = tile_end - tiles_e
    n_tiles = tile_end[-1]
    nt_max = (2 * t) // tm + N_EXPERTS
    slot = tile_start[ef] * tm + rank
    tid = jnp.minimum(jnp.arange(nt_max, dtype=jnp.int32), n_tiles - 1)
    tile_expert = jnp.sum((tile_end[None, :] <= tid[:, None]).astype(jnp.int32), axis=1)
    tok = jnp.tile(jnp.arange(t, dtype=jnp.int32), 2)
    rows = jnp.zeros(((nt_max + 1) * tm,), jnp.int32).at[slot].set(tok)
    return slot[:t], slot[t:], rows, tile_expert, n_tiles.reshape(1).astype(jnp.int32)


def kernel(x, positions, norm_mix, w_in, mu_shift, w0, w2, a0, a2, g2, k_k, k_a, r_k, ln_x_w, ln_x_b, w_out, norm_ffn, router_group_w, router_group_b, router_expert_w, router_expert_b, w_gate, w_up, w_down, norm_final):
    bsz, seq, d = x.shape
    assert bsz == 1
    depth = w_in.shape[0]
    d_rwkv = k_k.shape[1]
    d_attn = w_out.shape[1] - d_rwkv
    rwkv_cols = mu_shift.shape[2]
    nr = -(-rwkv_cols // (2 * LANES)) * (2 * LANES)
    assert 3 * d_rwkv + 2 * LANES + GATE_LORA == rwkv_cols and nr == 3 * d_rwkv + 4 * LANES
    tm_moe = 512
    tf_moe = 512
    sb_attn = max(dil for _, dil in DILATED_PATTERNS) * Q_BLOCK
    assert seq % sb_attn == 0

    xt = x[0]
    tabs = _rope_tables(positions[0][:, None], 512)
    for l in range(depth):
        w_r = jnp.pad(w_in[l][:, :rwkv_cols], ((0, 0), (0, nr - rwkv_cols))).astype(BF16)
        w_a = w_in[l][:, rwkv_cols:].astype(BF16)
        mu = jnp.pad(mu_shift[l], ((0, 0), (0, nr - rwkv_cols)))
        zl = jnp.zeros((DECAY_LORA, d_rwkv), F32)
        w2p = jnp.stack([jnp.concatenate([w2[l, 0], zl]), jnp.concatenate([zl, w2[l, 1]])]).astype(BF16)
        a2p = jnp.stack([jnp.concatenate([a2[l, 0], zl]), jnp.concatenate([zl, a2[l, 1]])]).astype(BF16)
        g2p = jnp.pad(g2[l], ((0, 2 * LANES - GATE_LORA), (0, 0))).astype(BF16)
        rk = r_k[l].reshape(1, d_rwkv)

        z_r = _norm_matmul(xt, norm_mix[l][None], w_r, 256)
        z_a = _norm_matmul_rope(xt, norm_mix[l][None], w_a, tabs, 256)
        yf, yb, bonus, gate = _rwkv_mixer(z_r, mu, w0[l], a0[l], w2p, a2p, g2p,
                                          k_k[l][None], k_a[l][None], rk, d_rwkv)
        y_attn = _dilated_attention(z_a, d_attn, sb_attn)
        x2 = _out_proj(xt, yf, yb, bonus, gate, ln_x_w[l][None], ln_x_b[l][None],
                       y_attn, w_out[l].astype(BF16), 256)

        w_rt = jnp.concatenate(
            [router_group_w[l], router_expert_w[l].transpose(1, 0, 2).reshape(d, N_EXPERTS)], axis=1)
        w_rt = jnp.pad(w_rt, ((0, 0), (0, LANES - w_rt.shape[1])))
        rt_hi = w_rt.astype(BF16)
        rt_lo = (w_rt - rt_hi.astype(F32)).astype(BF16)
        b_rt = jnp.concatenate([router_group_b[l], router_expert_b[l].reshape(-1)])
        b_rt = jnp.pad(b_rt, (0, LANES - b_rt.shape[0]))[None]
        h, idx, wts = _router(x2, norm_ffn[l][None], jnp.stack([rt_hi, rt_lo]), b_rt, 256)

        slot0, slot1, rows, tile_expert, n_tiles = _dispatch_plan(idx[:, :2], tm_moe)
        y_sorted = _moe(h, tile_expert, n_tiles, rows, w_gate[l], w_up[l], w_down[l], tm_moe, tf_moe)
        is_last = l == depth - 1
        assert is_last, "the combine kernel applies the final norm; depth must be 1"
        xt = _final(slot0, slot1, x2, wts, norm_final[None], y_sorted, 256)
    return xt[None]
```

```python
import functools

import jax
import jax.numpy as jnp
from jax import lax
from jax.experimental import pallas as pl
from jax.experimental.pallas import tpu as pltpu

HEAD_DIM = 64
LANES = 128
SUBLANES = 8
DECAY_LORA = 64
ICLR_LORA = 64
GATE_LORA = 160
GN_EPS = 64e-5
RMS_EPS = 1e-6
ROPE_THETA = 500000.0
ROPE_DIM = HEAD_DIM // 4
DILATED_PATTERNS = ((128, 1), (512, 4), (2048, 16))
Q_BLOCK = 128
N_GROUPS = 4
EXPERTS_PER_GROUP = 8
N_EXPERTS = N_GROUPS * EXPERTS_PER_GROUP
NEG_BIG = -1e30
CHUNK = 64
VMEM_LIMIT = 56 * 1024 * 1024

BF16 = jnp.bfloat16
F32 = jnp.float32


def _cparams(sem):
    return pltpu.CompilerParams(dimension_semantics=sem, vmem_limit_bytes=VMEM_LIMIT)


def _dot(a, b):
    return jnp.dot(a.astype(BF16), b.astype(BF16), preferred_element_type=F32)


def _dot_nt(a, b):
    return lax.dot_general(a.astype(BF16), b.astype(BF16), (((1,), (1,)), ((), ())),
                           preferred_element_type=F32)


def _dot_tn(a, b):
    return lax.dot_general(a.astype(BF16), b.astype(BF16), (((0,), (0,)), ((), ())),
                           preferred_element_type=F32)


def _split2(x):
    hi = x.astype(BF16)
    lo = (x - hi.astype(F32)).astype(BF16)
    return hi, lo


def _split3(x):
    hi = x.astype(BF16)
    r1 = x - hi.astype(F32)
    mid = r1.astype(BF16)
    lo = (r1 - mid.astype(F32)).astype(BF16)
    return hi, mid, lo


def _head_ones():
    i = lax.broadcasted_iota(jnp.int32, (LANES, LANES), 0) // HEAD_DIM
    j = lax.broadcasted_iota(jnp.int32, (LANES, LANES), 1) // HEAD_DIM
    return (i == j).astype(BF16)


def _head_sum(x, ones_bd):
    hi, lo = _split2(x)
    return (jnp.dot(hi, ones_bd, preferred_element_type=F32)
            + jnp.dot(lo, ones_bd, preferred_element_type=F32))


def _sigmoid(x):
    return 1.0 / (1.0 + jnp.exp(-x))


def _norm_matmul_kernel(x_ref, g_ref, w_ref, o_ref):
    x = x_ref[...]
    ms = jnp.mean(x * x, axis=-1, keepdims=True)
    h = x * lax.rsqrt(ms + RMS_EPS) * g_ref[...]
    o_ref[...] = jnp.dot(h.astype(BF16), w_ref[...], preferred_element_type=F32)


def _norm_matmul(x, g, w, tm):
    t, d = x.shape
    n = w.shape[1]
    return pl.pallas_call(
        _norm_matmul_kernel,
        grid=(t // tm,),
        in_specs=[pl.BlockSpec((tm, d), lambda i: (i, 0)),
                  pl.BlockSpec((1, d), lambda i: (0, 0)),
                  pl.BlockSpec((d, n), lambda i: (0, 0))],
        out_specs=pl.BlockSpec((tm, n), lambda i: (i, 0)),
        out_shape=jax.ShapeDtypeStruct((t, n), F32),
        compiler_params=_cparams(("arbitrary",)),
        name="norm_in_proj",
    )(x, g, w)


def _rwkv_kernel(zf_ref, zfp_ref, zfn_ref, zb_ref, zbp_ref, zbn_ref,
                 mu_ref, w0_ref, a0_ref, w2_ref, a2_ref, g2_ref, kk_ref, ka_ref, rk_ref,
                 yf_ref, yb_ref, bonus_ref, gate_ref,
                 zs_ref, prep_ref, h_ref, *, nc, d_rwkv):
    c = pl.program_id(0)
    C = CHUNK
    n_hp = d_rwkv // LANES

    @pl.when(c == 0)
    def _():
        h_ref[...] = jnp.zeros_like(h_ref)

    row = lax.broadcasted_iota(jnp.int32, (C, 1), 0)
    mu_p = mu_ref[0:1, :]
    mu_n = mu_ref[1:2, :]
    srcs = ((zf_ref, zfp_ref, zfn_ref, c), (zb_ref, zbp_ref, zbn_ref, nc - 1 - c))
    for d, (z_ref, zp_ref, zn_ref, chunk) in enumerate(srcs):
        z = z_ref[...]
        prev_row = jnp.where(chunk == 0, 0.0, zp_ref[SUBLANES - 1:SUBLANES, :])
        next_row = jnp.where(chunk == nc - 1, 0.0, zn_ref[0:1, :])
        z_prev = jnp.where(row == 0, prev_row, pltpu.roll(z, 1, 0))
        z_next = jnp.where(row == C - 1, next_row, pltpu.roll(z, C - 1, 0))
        zs_ref[d] = z + mu_p * (z_prev - z) + mu_n * (z_next - z)

    ones_bd = _head_ones()
    lane = lax.broadcasted_iota(jnp.int32, (1, LANES), 1)
    m_a = lane < HEAD_DIM
    ti = lax.broadcasted_iota(jnp.int32, (C, C), 0)
    tj = lax.broadcasted_iota(jnp.int32, (C, C), 1)
    bi = lax.broadcasted_iota(jnp.int32, (2 * C, 2 * C), 0)
    bj = lax.broadcasted_iota(jnp.int32, (2 * C, 2 * C), 1)
    same_blk = (bi // C) == (bj // C)
    eye = bi == bj
    cum_mat = ((tj <= ti).astype(BF16), (tj >= ti).astype(BF16))
    strict = (same_blk & ((bj % C) < (bi % C)), same_blk & ((bj % C) > (bi % C)))
    incl = (same_blk & ((bj % C) <= (bi % C)), same_blk & ((bj % C) >= (bi % C)))
    last_row = (C - 1, 0)
    off_wl = 3 * d_rwkv
    off_al = off_wl + LANES
    off_gl = off_al + LANES

    def to_rows(x):
        return jnp.concatenate([x[:, s * LANES:(s + 1) * LANES] for s in range(n_hp)], axis=0)

    def to_lanes(x):
        return jnp.concatenate([x[s * C:(s + 1) * C] for s in range(n_hp)], axis=1)

    def bd(x):
        return jnp.concatenate([jnp.where(m_a, x, 0), jnp.where(m_a, 0, x)], axis=0)

    def unbd(x):
        return x[0:C] + x[C:2 * C]

    w_c = []
    for d in range(2):
        r = zs_ref[d, :, 0:d_rwkv]
        k = zs_ref[d, :, d_rwkv:2 * d_rwkv]
        kkr = k * kk_ref[...]
        sums = [to_rows(kkr * kkr)]
        if d == 0:
            sums.append(to_rows(r * k * rk_ref[...]))
        hs = _head_sum(jnp.concatenate(sums, axis=0), ones_bd)
        kk = kkr * lax.rsqrt(to_lanes(hs[0:n_hp * C]) + 1e-12)
        if d == 0:
            bonus_ref[...] = to_lanes(hs[n_hp * C:2 * n_hp * C]) * zs_ref[0, :, 2 * d_rwkv:3 * d_rwkv]
            gate_ref[...] = _dot(_sigmoid(zs_ref[0, :, off_gl:off_gl + 2 * LANES]), g2_ref[...])
        u = w0_ref[d:d + 1, :] + _dot(jnp.tanh(zs_ref[d, :, off_wl:off_wl + LANES]), w2_ref[d])
        w_log = -(jnp.maximum(-u, 0.0) + jnp.log1p(jnp.exp(-jnp.abs(u)))) - 0.5
        logd = -jnp.exp(w_log)
        a = _sigmoid(a0_ref[d:d + 1, :] + _dot(zs_ref[d, :, off_al:off_al + LANES], a2_ref[d]))
        kd = k * (1.0 + (a - 1.0) * ka_ref[...])
        b = kk * a
        l_hi, l_mid, l_lo = _split3(logd)
        cm = cum_mat[d]
        cum = (jnp.dot(cm, l_hi, preferred_element_type=F32)
               + jnp.dot(cm, l_mid, preferred_element_type=F32)
               + jnp.dot(cm, l_lo, preferred_element_type=F32))
        cum_c = cum[last_row[d]:last_row[d] + 1, :]
        w_inv = jnp.exp(-cum)
        w_c.append(jnp.exp(cum_c))
        w_end = w_c[d] * w_inv
        prep_ref[d, 0] = -kk * jnp.exp(cum - logd)
        prep_ref[d, 1] = r * jnp.exp(cum)
        prep_ref[d, 2] = b * w_inv
        prep_ref[d, 3] = kd * w_inv
        prep_ref[d, 4] = b * w_end
        prep_ref[d, 5] = kd * w_end

    blocks = [(d, hp) for hp in range(n_hp) for d in range(2)]

    def slab(d, i, hp):
        return prep_ref[d, i, :, hp * LANES:(hp + 1) * LANES]

    incl2 = tuple(jnp.concatenate([m, m], axis=1) for m in incl)
    eye_f = eye.astype(F32)
    xs, aks, lows, z0s, vbds = [], [], [], [], []
    for d, hp in blocks:
        at = bd(slab(d, 0, hp).astype(BF16))
        rt = bd(slab(d, 1, hp).astype(BF16))
        bt = slab(d, 2, hp).astype(BF16)
        kt = slab(d, 3, hp).astype(BF16)
        lhs = jnp.concatenate([at, rt], axis=0)
        rhs = jnp.concatenate([bt, bt, kt, kt], axis=0)
        g = lax.dot_general(lhs, rhs, (((1,), (1,)), ((), ())), preferred_element_type=F32)
        xs.append(jnp.where(strict[d], g[0:2 * C, 0:2 * C], 0.0))
        aks.append(jnp.where(strict[d], g[0:2 * C, 2 * C:4 * C], 0.0).astype(BF16))
        lows.append(jnp.where(incl2[d], g[2 * C:4 * C, :], 0.0).astype(BF16))
    for i, (d, hp) in enumerate(blocks):
        vbd = bd(zs_ref[d, :, 2 * d_rwkv + hp * LANES:2 * d_rwkv + (hp + 1) * LANES].astype(BF16))
        vbds.append(vbd)
        akv = jnp.dot(aks[i], vbd, preferred_element_type=F32)
        z0s.append(jnp.concatenate([bd(slab(d, 0, hp).astype(BF16)), akv.astype(BF16)], axis=1))

    n_sq = C.bit_length() - 1
    ts = [eye_f + x for x in xs]
    ps = [x.astype(BF16) for x in xs]
    for j in range(1, n_sq):
        last = j == n_sq - 1
        for i in range(len(blocks)):
            if j == 1:
                ps[i] = jnp.dot(ps[i], ps[i], preferred_element_type=F32).astype(BF16)
            tb = ts[i].astype(BF16)
            if last:
                ts[i] = ts[i] + jnp.dot(ps[i], tb, preferred_element_type=F32)
            else:
                res = jnp.dot(ps[i], jnp.concatenate([ps[i], tb], axis=1), preferred_element_type=F32)
                ts[i] = ts[i] + res[:, LANES:2 * LANES]
                ps[i] = res[:, 0:LANES].astype(BF16)

    wts = []
    for i in range(len(blocks)):
        zz = jnp.dot(ts[i].astype(BF16), z0s[i], preferred_element_type=F32)
        bottom = jnp.concatenate([jnp.zeros((2 * C, LANES), BF16), vbds[i]], axis=1)
        wts.append(jnp.concatenate([zz.astype(BF16), bottom], axis=0))
    outs = []
    for i, (d, hp) in enumerate(blocks):
        bb_t = bd(slab(d, 4, hp)).T.astype(BF16)
        kb_t = bd(slab(d, 5, hp)).T.astype(BF16)
        lhs = jnp.concatenate([lows[i], jnp.concatenate([bb_t, kb_t], axis=1)], axis=0)
        outs.append(jnp.dot(lhs, wts[i], preferred_element_type=F32))
    for i, (d, hp) in enumerate(blocks):
        sl = pl.ds(hp * LANES, LANES)
        o = outs[i]
        r_hat = bd(slab(d, 1, hp)) + o[0:2 * C, 0:LANES]
        p_mat = eye_f * w_c[d][:, hp * LANES:(hp + 1) * LANES] + o[2 * C:4 * C, 0:LANES]
        lhs = jnp.concatenate([r_hat, p_mat], axis=0).astype(BF16)
        res = jnp.dot(lhs, h_ref[d, hp].astype(BF16), preferred_element_type=F32)
        y = unbd(res[0:2 * C] + o[0:2 * C, LANES:2 * LANES])
        h_ref[d, hp] = res[2 * C:4 * C] + o[2 * C:4 * C, LANES:2 * LANES]
        if d == 0:
            yf_ref[:, sl] = y
        else:
            yb_ref[:, sl] = y


def _rwkv_mixer(z, mu, w0, a0, w2p, a2p, g2p, k_k, k_a, r_k, d_rwkv):
    t, nr = z.shape
    C = CHUNK
    nc = t // C
    hb = C // SUBLANES
    n_hp = d_rwkv // LANES
    last_hblk = t // SUBLANES - 1

    def cur(fn):
        return pl.BlockSpec((C, nr), lambda c: (fn(c), 0))

    def prev(fn):
        return pl.BlockSpec((SUBLANES, nr), lambda c: (jnp.maximum(fn(c) * hb - 1, 0), 0))

    def nxt(fn):
        return pl.BlockSpec((SUBLANES, nr), lambda c: (jnp.minimum((fn(c) + 1) * hb, last_hblk), 0))

    fwd = lambda c: c
    bwd = lambda c: nc - 1 - c

    def full(a):
        nd = a.ndim
        return pl.BlockSpec(a.shape, lambda c: (0,) * nd)

    out_spec_f = pl.BlockSpec((C, d_rwkv), lambda c: (c, 0))
    out_spec_b = pl.BlockSpec((C, d_rwkv), lambda c: (nc - 1 - c, 0))
    out_sd = jax.ShapeDtypeStruct((t, d_rwkv), F32)
    params = (mu, w0, a0, w2p, a2p, g2p, k_k, k_a, r_k)
    return pl.pallas_call(
        functools.partial(_rwkv_kernel, nc=nc, d_rwkv=d_rwkv),
        grid=(nc,),
        in_specs=[cur(fwd), prev(fwd), nxt(fwd), cur(bwd), prev(bwd), nxt(bwd)]
                 + [full(p) for p in params],
        out_specs=[out_spec_f, out_spec_b, out_spec_f, out_spec_f],
        out_shape=[out_sd, out_sd, out_sd, out_sd],
        scratch_shapes=[pltpu.VMEM((2, C, nr), F32),
                        pltpu.VMEM((2, 6, C, d_rwkv), F32),
                        pltpu.VMEM((2, n_hp, LANES, LANES), F32)],
        compiler_params=_cparams(("arbitrary",)),
        name="rwkv7_chunk_scan",
    )(z, z, z, z, z, z, *params)


def _rope_table_kernel(pos_ref, inv_ref, c_ref, s1_ref, s2_ref):
    half = ROPE_DIM // 2
    pos = pos_ref[...].astype(F32)
    j = lax.broadcasted_iota(jnp.int32, (1, LANES), 1) % HEAD_DIM
    ang = pos * inv_ref[...]
    cs = jnp.cos(ang)
    sn = jnp.sin(ang)
    c_ref[...] = jnp.where(j < ROPE_DIM, cs, 1.0)
    s1_ref[...] = jnp.where(j < half, -sn, 0.0)
    s2_ref[...] = jnp.where((j >= half) & (j < ROPE_DIM), sn, 0.0)


def _rope_tables(positions, tm):
    t = positions.shape[0]
    half = ROPE_DIM // 2
    inv_freq = jnp.power(ROPE_THETA, -jnp.arange(half, dtype=F32) * 2.0 / ROPE_DIM)
    inv_lane = jnp.tile(inv_freq, LANES // half)[None]
    sd = jax.ShapeDtypeStruct((t, LANES), F32)
    spec = pl.BlockSpec((tm, LANES), lambda i: (i, 0))
    return pl.pallas_call(
        _rope_table_kernel,
        grid=(t // tm,),
        in_specs=[pl.BlockSpec((tm, 1), lambda i: (i, 0)),
                  pl.BlockSpec((1, LANES), lambda i: (0, 0))],
        out_specs=[spec, spec, spec],
        out_shape=[sd, sd, sd],
        compiler_params=_cparams(("arbitrary",)),
        name="rope_tables",
    )(positions, inv_lane)


def _rot(x, c, s1, s2):
    half = ROPE_DIM // 2
    return x * c + pltpu.roll(x, LANES - half, 1) * s1 + pltpu.roll(x, half, 1) * s2


def _norm_matmul_rope_kernel(x_ref, g_ref, w_ref, c_ref, s1_ref, s2_ref, o_ref, *, n_rot):
    x = x_ref[...]
    ms = jnp.mean(x * x, axis=-1, keepdims=True)
    h = x * lax.rsqrt(ms + RMS_EPS) * g_ref[...]
    z = jnp.dot(h.astype(BF16), w_ref[...], preferred_element_type=F32)
    c = c_ref[...]
    s1 = s1_ref[...]
    s2 = s2_ref[...]
    for s in range(z.shape[1] // LANES):
        sl = slice(s * LANES, (s + 1) * LANES)
        zs = z[:, sl]
        if s < n_rot:
            zs = _rot(zs, c, s1, s2)
            if s < n_rot // 2:
                zs = zs * (HEAD_DIM ** -0.5)
        o_ref[:, sl] = zs


def _norm_matmul_rope(x, g, w, tabs, tm):
    t, d = x.shape
    n = w.shape[1]
    tab = pl.BlockSpec((tm, LANES), lambda i: (i, 0))
    return pl.pallas_call(
        functools.partial(_norm_matmul_rope_kernel, n_rot=2 * (n // 3) // LANES),
        grid=(t // tm,),
        in_specs=[pl.BlockSpec((tm, d), lambda i: (i, 0)),
                  pl.BlockSpec((1, d), lambda i: (0, 0)),
                  pl.BlockSpec((d, n), lambda i: (0, 0)), tab, tab, tab],
        out_specs=pl.BlockSpec((tm, n), lambda i: (i, 0)),
        out_shape=jax.ShapeDtypeStruct((t, n), F32),
        compiler_params=_cparams(("arbitrary",)),
        name="norm_in_proj_rope",
    )(x, g, w, *tabs)


def _attn_kernel(q_ref, k_ref, v_ref, o_ref, op_ref, lp_ref, *, seq_t, sb):
    jsb = pl.program_id(1)
    lane = lax.broadcasted_iota(jnp.int32, (1, LANES), 1)
    m_a = lane < HEAD_DIM
    for p, (window, dil) in enumerate(DILATED_PATTERNS):
        radius = window // (2 * dil)
        win = Q_BLOCK + 2 * radius
        seq = seq_t // dil
        nq = sb // (dil * Q_BLOCK)
        qi = lax.broadcasted_iota(jnp.int32, (Q_BLOCK, win), 0)
        kj = lax.broadcasted_iota(jnp.int32, (Q_BLOCK, win), 1)

        def rows(start, size, dil=dil):
            return pl.ds(start, size) if dil == 1 else pl.ds(start, size, stride=dil)

        def body(idx, carry, dil=dil, radius=radius, win=win, seq=seq, nq=nq, qi=qi, kj=kj,
                 rows=rows, p=p):
            r = idx // nq
            jj = idx % nq
            q_l0 = (jsb * nq + jj) * Q_BLOCK
            start_l = jnp.clip(q_l0 - radius, 0, seq - win)
            q_rows = rows(r + dil * Q_BLOCK * jj, Q_BLOCK)
            k_rows = rows(r + dil * start_l, win)
            q = q_ref[q_rows, :].astype(BF16)
            kw = k_ref[k_rows, :].astype(BF16)
            vw = v_ref[k_rows, :].astype(BF16)
            valid = jnp.abs((q_l0 + qi) - (start_l + kj)) <= radius
            one = jnp.ones((), BF16)
            halves = ((jnp.where(m_a, q, 0), jnp.where(m_a, vw, one)),
                      (jnp.where(m_a, 0, q), jnp.where(m_a, one, vw)))
            res, mx = [], []
            for qh, vh in halves:
                sc = lax.dot_general(qh, kw, (((1,), (1,)), ((), ())), preferred_element_type=F32)
                sc = jnp.where(valid, sc, NEG_BIG)
                m = jnp.max(sc, axis=-1, keepdims=True)
                pexp = jnp.exp(sc - m).astype(BF16)
                res.append(jnp.dot(pexp, vh, preferred_element_type=F32))
                mx.append(m)
            num = jnp.where(m_a, res[0], res[1])
            den = pltpu.roll(jnp.where(m_a, res[1], res[0]), HEAD_DIM, 1)
            op_ref[p, q_rows, :] = num / den
            lp_ref[p, q_rows, :] = jnp.where(m_a, mx[0], mx[1]) + jnp.log(den)
            return carry

        lax.fori_loop(0, dil * nq, body, 0, unroll=4)

    l1 = lp_ref[0]
    l2 = lp_ref[1]
    l3 = lp_ref[2]
    m = jnp.maximum(jnp.maximum(l1, l2), l3)
    e1 = jnp.exp(l1 - m)
    e2 = jnp.exp(l2 - m)
    e3 = jnp.exp(l3 - m)
    o_ref[...] = (e1 * op_ref[0] + e2 * op_ref[1] + e3 * op_ref[2]) / (e1 + e2 + e3)


def _dilated_attention(z_attn, d_attn, sb):
    t = z_attn.shape[0]
    n_hp = d_attn // LANES
    n_pat = len(DILATED_PATTERNS)
    return pl.pallas_call(
        functools.partial(_attn_kernel, seq_t=t, sb=sb),
        grid=(n_hp, t // sb),
        in_specs=[pl.BlockSpec((sb, LANES), lambda h, j: (j, h)),
                  pl.BlockSpec((t, LANES), lambda h, j: (0, n_hp + h)),
                  pl.BlockSpec((t, LANES), lambda h, j: (0, 2 * n_hp + h))],
        out_specs=pl.BlockSpec((sb, LANES), lambda h, j: (j, h)),
        out_shape=jax.ShapeDtypeStruct((t, d_attn), F32),
        scratch_shapes=[pltpu.VMEM((n_pat, sb, LANES), F32), pltpu.VMEM((n_pat, sb, LANES), F32)],
        compiler_params=_cparams(("arbitrary", "arbitrary")),
        name="dilated_attention",
    )(z_attn, z_attn, z_attn)


def _out_proj_kernel(x_ref, yf_ref, yb_ref, bonus_ref, gate_ref, lnw_ref, lnb_ref,
                     ya_ref, w_ref, out_ref, yr_ref, *, d_rwkv):
    ones_bd = _head_ones()
    for hp in range(d_rwkv // LANES):
        sl = pl.ds(hp * LANES, LANES)
        y = yf_ref[:, sl] + yb_ref[:, sl]
        mean = _head_sum(y, ones_bd) * (1.0 / HEAD_DIM)
        yc = y - mean
        var = _head_sum(yc * yc, ones_bd) * (1.0 / HEAD_DIM)
        yn = yc * lax.rsqrt(var + GN_EPS) * lnw_ref[:, sl] + lnb_ref[:, sl]
        yr_ref[:, sl] = ((yn + bonus_ref[:, sl]) * gate_ref[:, sl]).astype(BF16)
    acc = jnp.dot(yr_ref[...], w_ref[0:d_rwkv, :], preferred_element_type=F32)
    acc += jnp.dot(ya_ref[...].astype(BF16), w_ref[d_rwkv:, :], preferred_element_type=F32)
    out_ref[...] = x_ref[...] + acc


def _out_proj(x, yf, yb, bonus, gate, ln_w, ln_b, y_attn, w_out, tm):
    t, d = x.shape
    d_rwkv = yf.shape[1]
    d_attn = y_attn.shape[1]
    row = lambda n: pl.BlockSpec((tm, n), lambda i: (i, 0))
    const = lambda a: pl.BlockSpec(a.shape, lambda i: (0, 0))
    return pl.pallas_call(
        functools.partial(_out_proj_kernel, d_rwkv=d_rwkv),
        grid=(t // tm,),
        in_specs=[row(d)] + [row(d_rwkv)] * 4 + [const(ln_w), const(ln_b)]
                 + [row(d_attn), const(w_out)],
        out_specs=row(d),
        out_shape=jax.ShapeDtypeStruct((t, d), F32),
        scratch_shapes=[pltpu.VMEM((tm, d_rwkv), BF16)],
        compiler_params=_cparams(("arbitrary",)),
        name="merge_out_proj",
    )(x, yf, yb, bonus, gate, ln_w, ln_b, y_attn, w_out)


def _router_kernel(x_ref, g_ref, w_ref, b_ref, h_ref, idx_ref, wt_ref):
    x = x_ref[...]
    ms = jnp.mean(x * x, axis=-1, keepdims=True)
    h = x * lax.rsqrt(ms + RMS_EPS) * g_ref[...]
    for c in range(h.shape[1] // LANES):
        h_ref[:, c, :] = h[:, c * LANES:(c + 1) * LANES]
    h_hi, h_lo = _split2(h)
    w_hi = w_ref[0]
    w_lo = w_ref[1]
    logits = (jnp.dot(h_hi, w_hi, preferred_element_type=F32)
              + jnp.dot(h_hi, w_lo, preferred_element_type=F32)
              + jnp.dot(h_lo, w_hi, preferred_element_type=F32)) + b_ref[...]
    lane = lax.broadcasted_iota(jnp.int32, logits.shape, 1).astype(F32)
    big = jnp.float32(LANES)
    is_g = lane < N_GROUPS
    gl = jnp.where(is_g, logits, NEG_BIG)
    gmax = jnp.max(gl, axis=-1, keepdims=True)
    gsel = jnp.min(jnp.where(is_g & (gl == gmax), lane, big), axis=-1, keepdims=True)
    g1 = 1.0 / jnp.sum(jnp.where(is_g, jnp.exp(gl - gmax), 0.0), axis=-1, keepdims=True)
    lo = N_GROUPS + gsel * EXPERTS_PER_GROUP
    in_grp = (lane >= lo) & (lane < lo + EXPERTS_PER_GROUP)
    el = jnp.where(in_grp, logits, NEG_BIG)
    v1 = jnp.max(el, axis=-1, keepdims=True)
    i1 = jnp.min(jnp.where(in_grp & (el == v1), lane, big), axis=-1, keepdims=True)
    rest = in_grp & (lane != i1)
    el2 = jnp.where(rest, logits, NEG_BIG)
    v2 = jnp.max(el2, axis=-1, keepdims=True)
    i2 = jnp.min(jnp.where(rest & (el2 == v2), lane, big), axis=-1, keepdims=True)
    e2 = jnp.exp(v2 - v1)
    ww1 = 1.0 / (1.0 + e2)
    ww2 = e2 / (1.0 + e2)
    idx = jnp.where(lane == 0, i1 - N_GROUPS, jnp.where(lane == 1, i2 - N_GROUPS, 0.0))
    idx_ref[...] = idx.astype(jnp.int32)
    wt_ref[...] = jnp.where(lane == 0, g1 * ww1, jnp.where(lane == 1, g1 * ww2, 0.0))


def _router(x, g, w_split, b_pad, tm):
    t, d = x.shape
    return pl.pallas_call(
        _router_kernel,
        grid=(t // tm,),
        in_specs=[pl.BlockSpec((tm, d), lambda i: (i, 0)),
                  pl.BlockSpec((1, d), lambda i: (0, 0)),
                  pl.BlockSpec((2, d, LANES), lambda i: (0, 0, 0)),
                  pl.BlockSpec((1, LANES), lambda i: (0, 0))],
        out_specs=[pl.BlockSpec((tm, d // LANES, LANES), lambda i: (i, 0, 0)),
                   pl.BlockSpec((tm, LANES), lambda i: (i, 0)),
                   pl.BlockSpec((tm, LANES), lambda i: (i, 0))],
        out_shape=[jax.ShapeDtypeStruct((t, d // LANES, LANES), F32),
                   jax.ShapeDtypeStruct((t, LANES), jnp.int32),
                   jax.ShapeDtypeStruct((t, LANES), F32)],
        compiler_params=_cparams(("arbitrary",)),
        name="router",
    )(x, g, w_split, b_pad)


def _moe_kernel(te_ref, nt_ref, rows_ref, h_hbm, wg_ref, wu_ref, wd_ref, o_ref,
                xg_ref, xb_ref, acc_ref, sem, *, tm, nf, nt_max):
    i = pl.program_id(0)
    f = pl.program_id(1)
    nt = nt_ref[0]
    active = i < nt
    slot = i % 2
    share = tm // nf

    n_slab = xg_ref.shape[2]

    def row_copy(tile, j, buf):
        src = h_hbm.at[pl.ds(rows_ref[tile * tm + j], 1)]
        return pltpu.make_async_copy(src, xg_ref.at[buf, pl.ds(j, 1)], sem.at[buf])

    def wait_tile(buf):
        pltpu.make_async_copy(h_hbm.at[pl.ds(0, tm)], xg_ref.at[buf], sem.at[buf]).wait()

    @pl.when((i == 0) & (f == 0))
    def _():
        def start(j, carry):
            row_copy(0, j, 0).start()
            return carry
        lax.fori_loop(0, tm, start, 0)

    @pl.when((f == 0) & (i <= nt))
    def _():
        wait_tile(slot)

    @pl.when(active & (f == 0))
    def _():
        for c in range(n_slab):
            xb_ref[:, c * LANES:(c + 1) * LANES] = xg_ref[slot, :, c, :].astype(BF16)

    @pl.when(active)
    def _():
        for jj in range(share):
            row_copy(i + 1, f * share + jj, 1 - slot).start()
        xb = xb_ref[...]
        gate = jnp.dot(xb, wg_ref[...].astype(BF16), preferred_element_type=F32)
        up = jnp.dot(xb, wu_ref[...].astype(BF16), preferred_element_type=F32)
        hid = (gate * _sigmoid(gate)) * up
        part = jnp.dot(hid.astype(BF16), wd_ref[...].astype(BF16), preferred_element_type=F32)

        @pl.when(f == 0)
        def _():
            acc_ref[...] = part

        @pl.when((f != 0) & (f != nf - 1))
        def _():
            acc_ref[...] += part

        @pl.when(f == nf - 1)
        def _():
            for c in range(n_slab):
                sl = slice(c * LANES, (c + 1) * LANES)
                o_ref[:, c, :] = acc_ref[:, sl] + part[:, sl]

    @pl.when(jnp.logical_not(active) & (f == 0))
    def _():
        o_ref[...] = jnp.zeros_like(o_ref)

    @pl.when((i == nt_max - 1) & (f == nf - 1) & (nt == nt_max))
    def _():
        wait_tile(nt_max % 2)


def _moe(h, tile_expert, n_tiles, rows, w_gate, w_up, w_down, tm, tf):
    t, n_slab, _ = h.shape
    n_e, d, d_e = w_gate.shape
    nt_max = tile_expert.shape[0]
    nf = d_e // tf
    assert nf >= 2 and n_slab * LANES == d
    assert rows.shape[0] == (nt_max + 1) * tm

    def fidx(i, f, nt):
        return jnp.where(i < nt[0], f, nf - 1)

    grid_spec = pltpu.PrefetchScalarGridSpec(
        num_scalar_prefetch=3,
        grid=(nt_max, d_e // tf),
        in_specs=[pl.BlockSpec(memory_space=pl.ANY),
                  pl.BlockSpec((None, d, tf), lambda i, f, te, nt, rw: (te[i], 0, fidx(i, f, nt))),
                  pl.BlockSpec((None, d, tf), lambda i, f, te, nt, rw: (te[i], 0, fidx(i, f, nt))),
                  pl.BlockSpec((None, tf, d), lambda i, f, te, nt, rw: (te[i], fidx(i, f, nt), 0))],
        out_specs=pl.BlockSpec((tm, n_slab, LANES), lambda i, f, te, nt, rw: (i, 0, 0)),
        scratch_shapes=[pltpu.VMEM((2, tm, n_slab, LANES), F32), pltpu.VMEM((tm, d), BF16),
                        pltpu.VMEM((tm, d), F32), pltpu.SemaphoreType.DMA((2,))],
    )
    return pl.pallas_call(
        functools.partial(_moe_kernel, tm=tm, nf=nf, nt_max=nt_max),
        grid_spec=grid_spec,
        out_shape=jax.ShapeDtypeStruct((nt_max * tm, n_slab, LANES), F32),
        compiler_params=_cparams(("arbitrary", "arbitrary")),
        name="grouped_expert_mlp",
    )(tile_expert, n_tiles, rows, h, w_gate, w_up, w_down)


def _final_kernel(s0_ref, s1_ref, x_ref, wt_ref, g_ref, y_hbm, o_ref, yg_ref, sem, *, tm, nt):
    i = pl.program_id(0)
    slot = i % 2
    n_slab = yg_ref.shape[3]
    d = n_slab * LANES

    def issue(tile, buf):
        def body(j, carry):
            for k, s_ref in enumerate((s0_ref, s1_ref)):
                pltpu.make_async_copy(y_hbm.at[pl.ds(s_ref[tile * tm + j], 1)],
                                      yg_ref.at[buf, k, pl.ds(j, 1)], sem.at[buf]).start()
            return carry
        lax.fori_loop(0, tm, body, 0, unroll=8)

    @pl.when(i == 0)
    def _():
        issue(0, 0)

    @pl.when(i + 1 < nt)
    def _():
        issue(i + 1, 1 - slot)

    for k in range(2):
        pltpu.make_async_copy(y_hbm.at[pl.ds(0, tm)], yg_ref.at[slot, k], sem.at[slot]).wait()

    wt = wt_ref[...]
    w0 = wt[:, 0:1]
    w1 = wt[:, 1:2]
    ss = jnp.zeros((tm, 1), F32)
    for c in range(n_slab):
        sl = slice(c * LANES, (c + 1) * LANES)
        xs = x_ref[:, sl] + w0 * yg_ref[slot, 0, :, c, :] + w1 * yg_ref[slot, 1, :, c, :]
        o_ref[:, sl] = xs
        ss = ss + jnp.sum(xs * xs, axis=-1, keepdims=True)
    o_ref[...] = o_ref[...] * lax.rsqrt(ss * (1.0 / d) + RMS_EPS) * g_ref[...]


def _final(slot0, slot1, x, wts, g, y_sorted, tm):
    t, d = x.shape
    grid_spec = pltpu.PrefetchScalarGridSpec(
        num_scalar_prefetch=2,
        grid=(t // tm,),
        in_specs=[pl.BlockSpec((tm, d), lambda i, s0, s1: (i, 0)),
                  pl.BlockSpec((tm, LANES), lambda i, s0, s1: (i, 0)),
                  pl.BlockSpec((1, d), lambda i, s0, s1: (0, 0)),
                  pl.BlockSpec(memory_space=pl.ANY)],
        out_specs=pl.BlockSpec((tm, d), lambda i, s0, s1: (i, 0)),
        scratch_shapes=[pltpu.VMEM((2, 2, tm, d // LANES, LANES), F32),
                        pltpu.SemaphoreType.DMA((2,))],
    )
    return pl.pallas_call(
        functools.partial(_final_kernel, tm=tm, nt=t // tm),
        grid_spec=grid_spec,
        out_shape=jax.ShapeDtypeStruct((t, d), F32),
        compiler_params=_cparams(("arbitrary",)),
        name="combine_final_norm",
    )(slot0, slot1, x, wts, g, y_sorted)


def _dispatch_plan(experts, tm):
    t = experts.shape[0]
    ef = experts.T.reshape(-1)
    onehot = (ef[:, None] == jnp.arange(N_EXPERTS, dtype=jnp.int32)[None, :]).astype(jnp.int32)
    csum = jnp.cumsum(onehot, axis=0)
    rank = jnp.take_along_axis(csum, ef[:, None], axis=1)[:, 0] - 1
    counts = csum[-1]
    tiles_e = (counts + tm - 1) // tm
    tile_end = jnp.cumsum(tiles_e)
    tile_start = tile_end - tiles_e
    n_tiles = tile_end[-1]
    nt_max = (2 * t) // tm + N_EXPERTS
    slot = tile_start[ef] * tm + rank
    tid = jnp.minimum(jnp.arange(nt_max, dtype=jnp.int32), n_tiles - 1)
    tile_expert = jnp.sum((tile_end[None, :] <= tid[:, None]).astype(jnp.int32), axis=1)
    tok = jnp.tile(jnp.arange(t, dtype=jnp.int32), 2)
    rows = jnp.zeros(((nt_max + 1) * tm,), jnp.int32).at[slot].set(tok)
    return slot[:t], slot[t:], rows, tile_expert, n_tiles.reshape(1).astype(jnp.int32)


def kernel(x, positions, norm_mix, w_in, mu_shift, w0, w2, a0, a2, g2, k_k, k_a, r_k, ln_x_w, ln_x_b, w_out, norm_ffn, router_group_w, router_group_b, router_expert_w, router_expert_b, w_gate, w_up, w_down, norm_final):
    bsz, seq, d = x.shape
    assert bsz == 1
    depth = w_in.shape[0]
    d_rwkv = k_k.shape[1]
    d_attn = w_out.shape[1] - d_rwkv
    rwkv_cols = mu_shift.shape[2]
    nr = -(-rwkv_cols // (2 * LANES)) * (2 * LANES)
    assert 3 * d_rwkv + 2 * LANES + GATE_LORA == rwkv_cols and nr == 3 * d_rwkv + 4 * LANES
    tm_moe = 512
    tf_moe = 512
    sb_attn = max(dil for _, dil in DILATED_PATTERNS) * Q_BLOCK
    assert seq % sb_attn == 0

    xt = x[0]
    tabs = _rope_tables(positions[0][:, None], 512)
    for l in range(depth):
        w_r = jnp.pad(w_in[l][:, :rwkv_cols], ((0, 0), (0, nr - rwkv_cols))).astype(BF16)
        w_a = w_in[l][:, rwkv_cols:].astype(BF16)
        mu = jnp.pad(mu_shift[l], ((0, 0), (0, nr - rwkv_cols)))
        zl = jnp.zeros((DECAY_LORA, d_rwkv), F32)
        w2p = jnp.stack([jnp.concatenate([w2[l, 0], zl]), jnp.concatenate([zl, w2[l, 1]])]).astype(BF16)
        a2p = jnp.stack([jnp.concatenate([a2[l, 0], zl]), jnp.concatenate([zl, a2[l, 1]])]).astype(BF16)
        g2p = jnp.pad(g2[l], ((0, 2 * LANES - GATE_LORA), (0, 0))).astype(BF16)
        rk = r_k[l].reshape(1, d_rwkv)

        z_r = _norm_matmul(xt, norm_mix[l][None], w_r, 256)
        z_a = _norm_matmul_rope(xt, norm_mix[l][None], w_a, tabs, 256)
        yf, yb, bonus, gate = _rwkv_mixer(z_r, mu, w0[l], a0[l], w2p, a2p, g2p,
                                          k_k[l][None], k_a[l][None], rk, d_rwkv)
        y_attn = _dilated_attention(z_a, d_attn, sb_attn)
        x2 = _out_proj(xt, yf, yb, bonus, gate, ln_x_w[l][None], ln_x_b[l][None],
                       y_attn, w_out[l].astype(BF16), 256)

        w_rt = jnp.concatenate(
            [router_group_w[l], router_expert_w[l].transpose(1, 0, 2).reshape(d, N_EXPERTS)], axis=1)
        w_rt = jnp.pad(w_rt, ((0, 0), (0, LANES - w_rt.shape[1])))
        rt_hi = w_rt.astype(BF16)
        rt_lo = (w_rt - rt_hi.astype(F32)).astype(BF16)
        b_rt = jnp.concatenate([router_group_b[l], router_expert_b[l].reshape(-1)])
        b_rt = jnp.pad(b_rt, (0, LANES - b_rt.shape[0]))[None]
        h, idx, wts = _router(x2, norm_ffn[l][None], jnp.stack([rt_hi, rt_lo]), b_rt, 256)

        slot0, slot1, rows, tile_expert, n_tiles = _dispatch_plan(idx[:, :2], tm_moe)
        y_sorted = _moe(h, tile_expert, n_tiles, rows, w_gate[l], w_up[l], w_down[l], tm_moe, tf_moe)
        is_last = l == depth - 1
        assert is_last, "the combine kernel applies the final norm; depth must be 1"
        xt = _final(slot0, slot1, x2, wts, norm_final[None], y_sorted, 256)
    return xt[None]
```

```python
import functools

import jax
import jax.numpy as jnp
from jax import lax
from jax.experimental import pallas as pl
from jax.experimental.pallas import tpu as pltpu

HEAD_DIM = 64
LANES = 128
SUBLANES = 8
DECAY_LORA = 64
ICLR_LORA = 64
GATE_LORA = 160
GN_EPS = 64e-5
RMS_EPS = 1e-6
ROPE_THETA = 500000.0
ROPE_DIM = HEAD_DIM // 4
DILATED_PATTERNS = ((128, 1), (512, 4), (2048, 16))
Q_BLOCK = 128
N_GROUPS = 4
EXPERTS_PER_GROUP = 8
N_EXPERTS = N_GROUPS * EXPERTS_PER_GROUP
NEG_BIG = -1e30
CHUNK = 64
VMEM_LIMIT = 56 * 1024 * 1024

BF16 = jnp.bfloat16
F32 = jnp.float32


def _cparams(sem):
    return pltpu.CompilerParams(dimension_semantics=sem, vmem_limit_bytes=VMEM_LIMIT)


def _dot(a, b):
    return jnp.dot(a.astype(BF16), b.astype(BF16), preferred_element_type=F32)


def _dot_nt(a, b):
    return lax.dot_general(a.astype(BF16), b.astype(BF16), (((1,), (1,)), ((), ())),
                           preferred_element_type=F32)


def _dot_tn(a, b):
    return lax.dot_general(a.astype(BF16), b.astype(BF16), (((0,), (0,)), ((), ())),
                           preferred_element_type=F32)


def _split2(x):
    hi = x.astype(BF16)
    lo = (x - hi.astype(F32)).astype(BF16)
    return hi, lo


def _split3(x):
    hi = x.astype(BF16)
    r1 = x - hi.astype(F32)
    mid = r1.astype(BF16)
    lo = (r1 - mid.astype(F32)).astype(BF16)
    return hi, mid, lo


def _head_ones():
    i = lax.broadcasted_iota(jnp.int32, (LANES, LANES), 0) // HEAD_DIM
    j = lax.broadcasted_iota(jnp.int32, (LANES, LANES), 1) // HEAD_DIM
    return (i == j).astype(BF16)


def _head_sum(x, ones_bd):
    hi, lo = _split2(x)
    return (jnp.dot(hi, ones_bd, preferred_element_type=F32)
            + jnp.dot(lo, ones_bd, preferred_element_type=F32))


def _sigmoid(x):
    return 1.0 / (1.0 + jnp.exp(-x))


def _norm_matmul_kernel(x_ref, g_ref, w_ref, o_ref):
    x = x_ref[...]
    ms = jnp.mean(x * x, axis=-1, keepdims=True)
    h = x * lax.rsqrt(ms + RMS_EPS) * g_ref[...]
    o_ref[...] = jnp.dot(h.astype(BF16), w_ref[...], preferred_element_type=F32)


def _norm_matmul(x, g, w, tm):
    t, d = x.shape
    n = w.shape[1]
    return pl.pallas_call(
        _norm_matmul_kernel,
        grid=(t // tm,),
        in_specs=[pl.BlockSpec((tm, d), lambda i: (i, 0)),
                  pl.BlockSpec((1, d), lambda i: (0, 0)),
                  pl.BlockSpec((d, n), lambda i: (0, 0))],
        out_specs=pl.BlockSpec((tm, n), lambda i: (i, 0)),
        out_shape=jax.ShapeDtypeStruct((t, n), F32),
        compiler_params=_cparams(("arbitrary",)),
        name="norm_in_proj",
    )(x, g, w)


def _rwkv_kernel(zf_ref, zfp_ref, zfn_ref, zb_ref, zbp_ref, zbn_ref,
                 mu_ref, w0_ref, a0_ref, w2_ref, a2_ref, g2_ref, kk_ref, ka_ref, rk_ref,
                 yf_ref, yb_ref, bonus_ref, gate_ref,
                 zs_ref, prep_ref, h_ref, *, nc, d_rwkv):
    c = pl.program_id(0)
    C = CHUNK
    n_hp = d_rwkv // LANES

    @pl.when(c == 0)
    def _():
        h_ref[...] = jnp.zeros_like(h_ref)

    row = lax.broadcasted_iota(jnp.int32, (C, 1), 0)
    mu_p = mu_ref[0:1, :]
    mu_n = mu_ref[1:2, :]
    srcs = ((zf_ref, zfp_ref, zfn_ref, c), (zb_ref, zbp_ref, zbn_ref, nc - 1 - c))
    for d, (z_ref, zp_ref, zn_ref, chunk) in enumerate(srcs):
        z = z_ref[...]
        prev_row = jnp.where(chunk == 0, 0.0, zp_ref[SUBLANES - 1:SUBLANES, :])
        next_row = jnp.where(chunk == nc - 1, 0.0, zn_ref[0:1, :])
        z_prev = jnp.where(row == 0, prev_row, pltpu.roll(z, 1, 0))
        z_next = jnp.where(row == C - 1, next_row, pltpu.roll(z, C - 1, 0))
        zs_ref[d] = z + mu_p * (z_prev - z) + mu_n * (z_next - z)

    ones_bd = _head_ones()
    lane = lax.broadcasted_iota(jnp.int32, (1, LANES), 1)
    m_a = lane < HEAD_DIM
    ti = lax.broadcasted_iota(jnp.int32, (C, C), 0)
    tj = lax.broadcasted_iota(jnp.int32, (C, C), 1)
    bi = lax.broadcasted_iota(jnp.int32, (2 * C, 2 * C), 0)
    bj = lax.broadcasted_iota(jnp.int32, (2 * C, 2 * C), 1)
    same_blk = (bi // C) == (bj // C)
    eye = bi == bj
    cum_mat = ((tj <= ti).astype(BF16), (tj >= ti).astype(BF16))
    strict = (same_blk & ((bj % C) < (bi % C)), same_blk & ((bj % C) > (bi % C)))
    incl = (same_blk & ((bj % C) <= (bi % C)), same_blk & ((bj % C) >= (bi % C)))
    last_row = (C - 1, 0)
    off_wl = 3 * d_rwkv
    off_al = off_wl + LANES
    off_gl = off_al + LANES

    def to_rows(x):
        return jnp.concatenate([x[:, s * LANES:(s + 1) * LANES] for s in range(n_hp)], axis=0)

    def to_lanes(x):
        return jnp.concatenate([x[s * C:(s + 1) * C] for s in range(n_hp)], axis=1)

    def bd(x):
        return jnp.concatenate([jnp.where(m_a, x, 0), jnp.where(m_a, 0, x)], axis=0)

    def unbd(x):
        return x[0:C] + x[C:2 * C]

    w_c = []
    for d in range(2):
        r = zs_ref[d, :, 0:d_rwkv]
        k = zs_ref[d, :, d_rwkv:2 * d_rwkv]
        kkr = k * kk_ref[...]
        sums = [to_rows(kkr * kkr)]
        if d == 0:
            sums.append(to_rows(r * k * rk_ref[...]))
        hs = _head_sum(jnp.concatenate(sums, axis=0), ones_bd)
        kk = kkr * lax.rsqrt(to_lanes(hs[0:n_hp * C]) + 1e-12)
        if d == 0:
            bonus_ref[...] = to_lanes(hs[n_hp * C:2 * n_hp * C]) * zs_ref[0, :, 2 * d_rwkv:3 * d_rwkv]
            gate_ref[...] = _dot(_sigmoid(zs_ref[0, :, off_gl:off_gl + 2 * LANES]), g2_ref[...])
        u = w0_ref[d:d + 1, :] + _dot(jnp.tanh(zs_ref[d, :, off_wl:off_wl + LANES]), w2_ref[d])
        w_log = -(jnp.maximum(-u, 0.0) + jnp.log1p(jnp.exp(-jnp.abs(u)))) - 0.5
        logd = -jnp.exp(w_log)
        a = _sigmoid(a0_ref[d:d + 1, :] + _dot(zs_ref[d, :, off_al:off_al + LANES], a2_ref[d]))
        kd = k * (1.0 + (a - 1.0) * ka_ref[...])
        b = kk * a
        l_hi, l_mid, l_lo = _split3(logd)
        cm = cum_mat[d]
        cum = (jnp.dot(cm, l_hi, preferred_element_type=F32)
               + jnp.dot(cm, l_mid, preferred_element_type=F32)
               + jnp.dot(cm, l_lo, preferred_element_type=F32))
        cum_c = cum[last_row[d]:last_row[d] + 1, :]
        w_inv = jnp.exp(-cum)
        w_c.append(jnp.exp(cum_c))
        w_end = w_c[d] * w_inv
        prep_ref[d, 0] = -kk * jnp.exp(cum - logd)
        prep_ref[d, 1] = r * jnp.exp(cum)
        prep_ref[d, 2] = b * w_inv
        prep_ref[d, 3] = kd * w_inv
        prep_ref[d, 4] = b * w_end
        prep_ref[d, 5] = kd * w_end

    blocks = [(d, hp) for hp in range(n_hp) for d in range(2)]

    def slab(d, i, hp):
        return prep_ref[d, i, :, hp * LANES:(hp + 1) * LANES]

    incl2 = tuple(jnp.concatenate([m, m], axis=1) for m in incl)
    eye_f = eye.astype(F32)
    xs, aks, lows, z0s, vbds = [], [], [], [], []
    for d, hp in blocks:
        at = bd(slab(d, 0, hp).astype(BF16))
        rt = bd(slab(d, 1, hp).astype(BF16))
        bt = slab(d, 2, hp).astype(BF16)
        kt = slab(d, 3, hp).astype(BF16)
        lhs = jnp.concatenate([at, rt], axis=0)
        rhs = jnp.concatenate([bt, bt, kt, kt], axis=0)
        g = lax.dot_general(lhs, rhs, (((1,), (1,)), ((), ())), preferred_element_type=F32)
        xs.append(jnp.where(strict[d], g[0:2 * C, 0:2 * C], 0.0))
        aks.append(jnp.where(strict[d], g[0:2 * C, 2 * C:4 * C], 0.0).astype(BF16))
        lows.append(jnp.where(incl2[d], g[2 * C:4 * C, :], 0.0).astype(BF16))
    for i, (d, hp) in enumerate(blocks):
        vbd = bd(zs_ref[d, :, 2 * d_rwkv + hp * LANES:2 * d_rwkv + (hp + 1) * LANES].astype(BF16))
        vbds.append(vbd)
        akv = jnp.dot(aks[i], vbd, preferred_element_type=F32)
        z0s.append(jnp.concatenate([bd(slab(d, 0, hp).astype(BF16)), akv.astype(BF16)], axis=1))

    n_sq = C.bit_length() - 1
    ts = [eye_f + x for x in xs]
    ps = [x.astype(BF16) for x in xs]
    for j in range(1, n_sq):
        last = j == n_sq - 1
        for i in range(len(blocks)):
            if j == 1:
                ps[i] = jnp.dot(ps[i], ps[i], preferred_element_type=F32).astype(BF16)
            tb = ts[i].astype(BF16)
            if last:
                ts[i] = ts[i] + jnp.dot(ps[i], tb, preferred_element_type=F32)
            else:
                res = jnp.dot(ps[i], jnp.concatenate([ps[i], tb], axis=1), preferred_element_type=F32)
                ts[i] = ts[i] + res[:, LANES:2 * LANES]
                ps[i] = res[:, 0:LANES].astype(BF16)

    wts = []
    for i in range(len(blocks)):
        zz = jnp.dot(ts[i].astype(BF16), z0s[i], preferred_element_type=F32)
        bottom = jnp.concatenate([jnp.zeros((2 * C, LANES), BF16), vbds[i]], axis=1)
        wts.append(jnp.concatenate([zz.astype(BF16), bottom], axis=0))
    outs = []
    for i, (d, hp) in enumerate(blocks):
        bb_t = bd(slab(d, 4, hp)).T.astype(BF16)
        kb_t = bd(slab(d, 5, hp)).T.astype(BF16)
        lhs = jnp.concatenate([lows[i], jnp.concatenate([bb_t, kb_t], axis=1)], axis=0)
        outs.append(jnp.dot(lhs, wts[i], preferred_element_type=F32))
    for i, (d, hp) in enumerate(blocks):
        sl = pl.ds(hp * LANES, LANES)
        o = outs[i]
        r_hat = bd(slab(d, 1, hp)) + o[0:2 * C, 0:LANES]
        p_mat = eye_f * w_c[d][:, hp * LANES:(hp + 1) * LANES] + o[2 * C:4 * C, 0:LANES]
        lhs = jnp.concatenate([r_hat, p_mat], axis=0).astype(BF16)
        res = jnp.dot(lhs, h_ref[d, hp].astype(BF16), preferred_element_type=F32)
        y = unbd(res[0:2 * C] + o[0:2 * C, LANES:2 * LANES])
        h_ref[d, hp] = res[2 * C:4 * C] + o[2 * C:4 * C, LANES:2 * LANES]
        if d == 0:
            yf_ref[:, sl] = y
        else:
            yb_ref[:, sl] = y


def _rwkv_mixer(z, mu, w0, a0, w2p, a2p, g2p, k_k, k_a, r_k, d_rwkv):
    t, nr = z.shape
    C = CHUNK
    nc = t // C
    hb = C // SUBLANES
    n_hp = d_rwkv // LANES
    last_hblk = t // SUBLANES - 1

    def cur(fn):
        return pl.BlockSpec((C, nr), lambda c: (fn(c), 0))

    def prev(fn):
        return pl.BlockSpec((SUBLANES, nr), lambda c: (jnp.maximum(fn(c) * hb - 1, 0), 0))

    def nxt(fn):
        return pl.BlockSpec((SUBLANES, nr), lambda c: (jnp.minimum((fn(c) + 1) * hb, last_hblk), 0))

    fwd = lambda c: c
    bwd = lambda c: nc - 1 - c

    def full(a):
        nd = a.ndim
        return pl.BlockSpec(a.shape, lambda c: (0,) * nd)

    out_spec_f = pl.BlockSpec((C, d_rwkv), lambda c: (c, 0))
    out_spec_b = pl.BlockSpec((C, d_rwkv), lambda c: (nc - 1 - c, 0))
    out_sd = jax.ShapeDtypeStruct((t, d_rwkv), F32)
    params = (mu, w0, a0, w2p, a2p, g2p, k_k, k_a, r_k)
    return pl.pallas_call(
        functools.partial(_rwkv_kernel, nc=nc, d_rwkv=d_rwkv),
        grid=(nc,),
        in_specs=[cur(fwd), prev(fwd), nxt(fwd), cur(bwd), prev(bwd), nxt(bwd)]
                 + [full(p) for p in params],
        out_specs=[out_spec_f, out_spec_b, out_spec_f, out_spec_f],
        out_shape=[out_sd, out_sd, out_sd, out_sd],
        scratch_shapes=[pltpu.VMEM((2, C, nr), F32),
                        pltpu.VMEM((2, 6, C, d_rwkv), F32),
                        pltpu.VMEM((2, n_hp, LANES, LANES), F32)],
        compiler_params=_cparams(("arbitrary",)),
        name="rwkv7_chunk_scan",
    )(z, z, z, z, z, z, *params)


def _rope_table_kernel(pos_ref, inv_ref, c_ref, s1_ref, s2_ref):
    half = ROPE_DIM // 2
    pos = pos_ref[...].astype(F32)
    j = lax.broadcasted_iota(jnp.int32, (1, LANES), 1) % HEAD_DIM
    ang = pos * inv_ref[...]
    cs = jnp.cos(ang)
    sn = jnp.sin(ang)
    c_ref[...] = jnp.where(j < ROPE_DIM, cs, 1.0)
    s1_ref[...] = jnp.where(j < half, -sn, 0.0)
    s2_ref[...] = jnp.where((j >= half) & (j < ROPE_DIM), sn, 0.0)


def _rope_tables(positions, tm):
    t = positions.shape[0]
    half = ROPE_DIM // 2
    inv_freq = jnp.power(ROPE_THETA, -jnp.arange(half, dtype=F32) * 2.0 / ROPE_DIM)
    inv_lane = jnp.tile(inv_freq, LANES // half)[None]
    sd = jax.ShapeDtypeStruct((t, LANES), F32)
    spec = pl.BlockSpec((tm, LANES), lambda i: (i, 0))
    return pl.pallas_call(
        _rope_table_kernel,
        grid=(t // tm,),
        in_specs=[pl.BlockSpec((tm, 1), lambda i: (i, 0)),
                  pl.BlockSpec((1, LANES), lambda i: (0, 0))],
        out_specs=[spec, spec, spec],
        out_shape=[sd, sd, sd],
        compiler_params=_cparams(("arbitrary",)),
        name="rope_tables",
    )(positions, inv_lane)


def _rot(x, c, s1, s2):
    half = ROPE_DIM // 2
    return x * c + pltpu.roll(x, LANES - half, 1) * s1 + pltpu.roll(x, half, 1) * s2


def _norm_matmul_rope_kernel(x_ref, g_ref, w_ref, c_ref, s1_ref, s2_ref, o_ref, *, n_rot):
    x = x_ref[...]
    ms = jnp.mean(x * x, axis=-1, keepdims=True)
    h = x * lax.rsqrt(ms + RMS_EPS) * g_ref[...]
    z = jnp.dot(h.astype(BF16), w_ref[...], preferred_element_type=F32)
    c = c_ref[...]
    s1 = s1_ref[...]
    s2 = s2_ref[...]
    for s in range(z.shape[1] // LANES):
        sl = slice(s * LANES, (s + 1) * LANES)
        zs = z[:, sl]
        if s < n_rot:
            zs = _rot(zs, c, s1, s2)
            if s < n_rot // 2:
                zs = zs * (HEAD_DIM ** -0.5)
        o_ref[:, sl] = zs


def _norm_matmul_rope(x, g, w, tabs, tm):
    t, d = x.shape
    n = w.shape[1]
    tab = pl.BlockSpec((tm, LANES), lambda i: (i, 0))
    return pl.pallas_call(
        functools.partial(_norm_matmul_rope_kernel, n_rot=2 * (n // 3) // LANES),
        grid=(t // tm,),
        in_specs=[pl.BlockSpec((tm, d), lambda i: (i, 0)),
                  pl.BlockSpec((1, d), lambda i: (0, 0)),
                  pl.BlockSpec((d, n), lambda i: (0, 0)), tab, tab, tab],
        out_specs=pl.BlockSpec((tm, n), lambda i: (i, 0)),
        out_shape=jax.ShapeDtypeStruct((t, n), F32),
        compiler_params=_cparams(("arbitrary",)),
        name="norm_in_proj_rope",
    )(x, g, w, *tabs)


def _attn_kernel(q_ref, k_ref, v_ref, o_ref, op_ref, lp_ref, *, seq_t, sb):
    jsb = pl.program_id(1)
    lane = lax.broadcasted_iota(jnp.int32, (1, LANES), 1)
    m_a = lane < HEAD_DIM
    for p, (window, dil) in enumerate(DILATED_PATTERNS):
        radius = window // (2 * dil)
        win = Q_BLOCK + 2 * radius
        seq = seq_t // dil
        nq = sb // (dil * Q_BLOCK)
        qi = lax.broadcasted_iota(jnp.int32, (Q_BLOCK, win), 0)
        kj = lax.broadcasted_iota(jnp.int32, (Q_BLOCK, win), 1)

        def rows(start, size, dil=dil):
            return pl.ds(start, size) if dil == 1 else pl.ds(start, size, stride=dil)

        def body(idx, carry, dil=dil, radius=radius, win=win, seq=seq, nq=nq, qi=qi, kj=kj,
                 rows=rows, p=p):
            r = idx // nq
            jj = idx % nq
            q_l0 = (jsb * nq + jj) * Q_BLOCK
            start_l = jnp.clip(q_l0 - radius, 0, seq - win)
            q_rows = rows(r + dil * Q_BLOCK * jj, Q_BLOCK)
            k_rows = rows(r + dil * start_l, win)
            q = q_ref[q_rows, :].astype(BF16)
            kw = k_ref[k_rows, :].astype(BF16)
            vw = v_ref[k_rows, :].astype(BF16)
            valid = jnp.abs((q_l0 + qi) - (start_l + kj)) <= radius
            one = jnp.ones((), BF16)
            halves = ((jnp.where(m_a, q, 0), jnp.where(m_a, vw, one)),
                      (jnp.where(m_a, 0, q), jnp.where(m_a, one, vw)))
            res, mx = [], []
            for qh, vh in halves:
                sc = lax.dot_general(qh, kw, (((1,), (1,)), ((), ())), preferred_element_type=F32)
                sc = jnp.where(valid, sc, NEG_BIG)
                m = jnp.max(sc, axis=-1, keepdims=True)
                pexp = jnp.exp(sc - m).astype(BF16)
                res.append(jnp.dot(pexp, vh, preferred_element_type=F32))
                mx.append(m)
            num = jnp.where(m_a, res[0], res[1])
            den = pltpu.roll(jnp.where(m_a, res[1], res[0]), HEAD_DIM, 1)
            op_ref[p, q_rows, :] = num / den
            lp_ref[p, q_rows, :] = jnp.where(m_a, mx[0], mx[1]) + jnp.log(den)
            return carry

        lax.fori_loop(0, dil * nq, body, 0, unroll=4)

    l1 = lp_ref[0]
    l2 = lp_ref[1]
    l3 = lp_ref[2]
    m = jnp.maximum(jnp.maximum(l1, l2), l3)
    e1 = jnp.exp(l1 - m)
    e2 = jnp.exp(l2 - m)
    e3 = jnp.exp(l3 - m)
    o_ref[...] = (e1 * op_ref[0] + e2 * op_ref[1] + e3 * op_ref[2]) / (e1 + e2 + e3)


def _dilated_attention(z_attn, d_attn, sb):
    t = z_attn.shape[0]
    n_hp = d_attn // LANES
    n_pat = len(DILATED_PATTERNS)
    return pl.pallas_call(
        functools.partial(_attn_kernel, seq_t=t, sb=sb),
        grid=(n_hp, t // sb),
        in_specs=[pl.BlockSpec((sb, LANES), lambda h, j: (j, h)),
                  pl.BlockSpec((t, LANES), lambda h, j: (0, n_hp + h)),
                  pl.BlockSpec((t, LANES), lambda h, j: (0, 2 * n_hp + h))],
        out_specs=pl.BlockSpec((sb, LANES), lambda h, j: (j, h)),
        out_shape=jax.ShapeDtypeStruct((t, d_attn), F32),
        scratch_shapes=[pltpu.VMEM((n_pat, sb, LANES), F32), pltpu.VMEM((n_pat, sb, LANES), F32)],
        compiler_params=_cparams(("arbitrary", "arbitrary")),
        name="dilated_attention",
    )(z_attn, z_attn, z_attn)


def _out_proj_kernel(x_ref, yf_ref, yb_ref, bonus_ref, gate_ref, lnw_ref, lnb_ref,
                     ya_ref, w_ref, out_ref, yr_ref, *, d_rwkv):
    ones_bd = _head_ones()
    for hp in range(d_rwkv // LANES):
        sl = pl.ds(hp * LANES, LANES)
        y = yf_ref[:, sl] + yb_ref[:, sl]
        mean = _head_sum(y, ones_bd) * (1.0 / HEAD_DIM)
        yc = y - mean
        var = _head_sum(yc * yc, ones_bd) * (1.0 / HEAD_DIM)
        yn = yc * lax.rsqrt(var + GN_EPS) * lnw_ref[:, sl] + lnb_ref[:, sl]
        yr_ref[:, sl] = ((yn + bonus_ref[:, sl]) * gate_ref[:, sl]).astype(BF16)
    acc = jnp.dot(yr_ref[...], w_ref[0:d_rwkv, :], preferred_element_type=F32)
    acc += jnp.dot(ya_ref[...].astype(BF16), w_ref[d_rwkv:, :], preferred_element_type=F32)
    out_ref[...] = x_ref[...] + acc


def _out_proj(x, yf, yb, bonus, gate, ln_w, ln_b, y_attn, w_out, tm):
    t, d = x.shape
    d_rwkv = yf.shape[1]
    d_attn = y_attn.shape[1]
    row = lambda n: pl.BlockSpec((tm, n), lambda i: (i, 0))
    const = lambda a: pl.BlockSpec(a.shape, lambda i: (0, 0))
    return pl.pallas_call(
        functools.partial(_out_proj_kernel, d_rwkv=d_rwkv),
        grid=(t // tm,),
        in_specs=[row(d)] + [row(d_rwkv)] * 4 + [const(ln_w), const(ln_b)]
                 + [row(d_attn), const(w_out)],
        out_specs=row(d),
        out_shape=jax.ShapeDtypeStruct((t, d), F32),
        scratch_shapes=[pltpu.VMEM((tm, d_rwkv), BF16)],
        compiler_params=_cparams(("arbitrary",)),
        name="merge_out_proj",
    )(x, yf, yb, bonus, gate, ln_w, ln_b, y_attn, w_out)


def _router_kernel(x_ref, g_ref, w_ref, b_ref, h_ref, idx_ref, wt_ref):
    x = x_ref[...]
    ms = jnp.mean(x * x, axis=-1, keepdims=True)
    h = x * lax.rsqrt(ms + RMS_EPS) * g_ref[...]
    h_ref[...] = h
    h_hi, h_lo = _split2(h)
    w_hi = w_ref[0]
    w_lo = w_ref[1]
    logits = (jnp.dot(h_hi, w_hi, preferred_element_type=F32)
              + jnp.dot(h_hi, w_lo, preferred_element_type=F32)
              + jnp.dot(h_lo, w_hi, preferred_element_type=F32)) + b_ref[...]
    lane = lax.broadcasted_iota(jnp.int32, logits.shape, 1).astype(F32)
    big = jnp.float32(LANES)
    is_g = lane < N_GROUPS
    gl = jnp.where(is_g, logits, NEG_BIG)
    gmax = jnp.max(gl, axis=-1, keepdims=True)
    gsel = jnp.min(jnp.where(is_g & (gl == gmax), lane, big), axis=-1, keepdims=True)
    g1 = 1.0 / jnp.sum(jnp.where(is_g, jnp.exp(gl - gmax), 0.0), axis=-1, keepdims=True)
    lo = N_GROUPS + gsel * EXPERTS_PER_GROUP
    in_grp = (lane >= lo) & (lane < lo + EXPERTS_PER_GROUP)
    el = jnp.where(in_grp, logits, NEG_BIG)
    v1 = jnp.max(el, axis=-1, keepdims=True)
    i1 = jnp.min(jnp.where(in_grp & (el == v1), lane, big), axis=-1, keepdims=True)
    rest = in_grp & (lane != i1)
    el2 = jnp.where(rest, logits, NEG_BIG)
    v2 = jnp.max(el2, axis=-1, keepdims=True)
    i2 = jnp.min(jnp.where(rest & (el2 == v2), lane, big), axis=-1, keepdims=True)
    e2 = jnp.exp(v2 - v1)
    ww1 = 1.0 / (1.0 + e2)
    ww2 = e2 / (1.0 + e2)
    idx = jnp.where(lane == 0, i1 - N_GROUPS, jnp.where(lane == 1, i2 - N_GROUPS, 0.0))
    idx_ref[...] = idx.astype(jnp.int32)
    wt_ref[...] = jnp.where(lane == 0, g1 * ww1, jnp.where(lane == 1, g1 * ww2, 0.0))


def _router(x, g, w_split, b_pad, tm):
    t, d = x.shape
    return pl.pallas_call(
        _router_kernel,
        grid=(t // tm,),
        in_specs=[pl.BlockSpec((tm, d), lambda i: (i, 0)),
                  pl.BlockSpec((1, d), lambda i: (0, 0)),
                  pl.BlockSpec((2, d, LANES), lambda i: (0, 0, 0)),
                  pl.BlockSpec((1, LANES), lambda i: (0, 0))],
        out_specs=[pl.BlockSpec((tm, d), lambda i: (i, 0)),
                   pl.BlockSpec((tm, LANES), lambda i: (i, 0)),
                   pl.BlockSpec((tm, LANES), lambda i: (i, 0))],
        out_shape=[jax.ShapeDtypeStruct((t, d), F32),
                   jax.ShapeDtypeStruct((t, LANES), jnp.int32),
                   jax.ShapeDtypeStruct((t, LANES), F32)],
        compiler_params=_cparams(("arbitrary",)),
        name="router",
    )(x, g, w_split, b_pad)


def _moe_kernel(te_ref, nt_ref, rows_ref, h_hbm, wg_ref, wu_ref, wd_ref, o_ref,
                xg_ref, xb_ref, acc_ref, sem, *, tm, nf, nt_max):
    i = pl.program_id(0)
    f = pl.program_id(1)
    nt = nt_ref[0]
    active = i < nt
    slot = i % 2
    share = tm // nf

    n_slab = xg_ref.shape[2] // LANES

    def row_copy(tile, j, buf):
        src = h_hbm.at[pl.ds(rows_ref[tile * tm + j], 1), :]
        return pltpu.make_async_copy(src, xg_ref.at[buf, pl.ds(j, 1), :], sem.at[buf])

    def wait_tile(buf):
        pltpu.make_async_copy(h_hbm.at[pl.ds(0, tm), :], xg_ref.at[buf], sem.at[buf]).wait()

    @pl.when((i == 0) & (f == 0))
    def _():
        def start(j, carry):
            row_copy(0, j, 0).start(priority=1)
            return carry
        lax.fori_loop(0, tm, start, 0, unroll=8)

    @pl.when((f == 0) & (i <= nt))
    def _():
        wait_tile(slot)

    @pl.when(active & (f == 0))
    def _():
        xb_ref[...] = xg_ref[slot].astype(BF16)

    @pl.when(active)
    def _():
        for jj in range(share):
            row_copy(i + 1, f * share + jj, 1 - slot).start(priority=1)
        xb = xb_ref[...]
        gate = jnp.dot(xb, wg_ref[...].astype(BF16), preferred_element_type=F32)
        up = jnp.dot(xb, wu_ref[...].astype(BF16), preferred_element_type=F32)
        hid = (gate * _sigmoid(gate)) * up
        part = jnp.dot(hid.astype(BF16), wd_ref[...].astype(BF16), preferred_element_type=F32)

        @pl.when(f == 0)
        def _():
            acc_ref[...] = part

        @pl.when((f != 0) & (f != nf - 1))
        def _():
            acc_ref[...] += part

        @pl.when(f == nf - 1)
        def _():
            for c in range(n_slab):
                sl = slice(c * LANES, (c + 1) * LANES)
                o_ref[pl.ds(c, tm, stride=n_slab), :] = acc_ref[:, sl] + part[:, sl]

    @pl.when(jnp.logical_not(active) & (f == 0))
    def _():
        o_ref[...] = jnp.zeros_like(o_ref)

    @pl.when((i == nt_max - 1) & (f == nf - 1) & (nt == nt_max))
    def _():
        wait_tile(nt_max % 2)


def _moe(h, tile_expert, n_tiles, rows, w_gate, w_up, w_down, tm, tf):
    t, d = h.shape
    n_e, _, d_e = w_gate.shape
    n_slab = d // LANES
    nt_max = tile_expert.shape[0]
    nf = d_e // tf
    assert nf >= 2
    assert rows.shape[0] == (nt_max + 1) * tm

    def fidx(i, f, nt):
        return jnp.where(i < nt[0], f, nf - 1)

    grid_spec = pltpu.PrefetchScalarGridSpec(
        num_scalar_prefetch=3,
        grid=(nt_max, d_e // tf),
        in_specs=[pl.BlockSpec(memory_space=pl.ANY),
                  pl.BlockSpec((None, d, tf), lambda i, f, te, nt, rw: (te[i], 0, fidx(i, f, nt))),
                  pl.BlockSpec((None, d, tf), lambda i, f, te, nt, rw: (te[i], 0, fidx(i, f, nt))),
                  pl.BlockSpec((None, tf, d), lambda i, f, te, nt, rw: (te[i], fidx(i, f, nt), 0))],
        out_specs=pl.BlockSpec((tm * n_slab, LANES), lambda i, f, te, nt, rw: (i, 0)),
        scratch_shapes=[pltpu.VMEM((2, tm, d), F32), pltpu.VMEM((tm, d), BF16),
                        pltpu.VMEM((tm, d), F32), pltpu.SemaphoreType.DMA((2,))],
    )
    return pl.pallas_call(
        functools.partial(_moe_kernel, tm=tm, nf=nf, nt_max=nt_max),
        grid_spec=grid_spec,
        out_shape=jax.ShapeDtypeStruct((nt_max * tm * n_slab, LANES), F32),
        compiler_params=_cparams(("arbitrary", "arbitrary")),
        name="grouped_expert_mlp",
    )(tile_expert, n_tiles, rows, h, w_gate, w_up, w_down)


def _final_kernel(s0_ref, s1_ref, x_ref, wt_ref, g_ref, y_hbm, o_ref, yg_ref, sem, *, tm, nt):
    i = pl.program_id(0)
    slot = i % 2
    d = x_ref.shape[1]
    n_slab = d // LANES

    def issue(tile, buf):
        def body(j, carry):
            for k, s_ref in enumerate((s0_ref, s1_ref)):
                src = y_hbm.at[pl.ds(pl.multiple_of(s_ref[tile * tm + j] * n_slab, n_slab), n_slab), :]
                dst = yg_ref.at[buf, k, pl.ds(pl.multiple_of(j * n_slab, n_slab), n_slab), :]
                pltpu.make_async_copy(src, dst, sem.at[buf]).start(priority=k)
            return carry
        lax.fori_loop(0, tm, body, 0, unroll=8)

    @pl.when(i == 0)
    def _():
        issue(0, 0)

    @pl.when(i + 1 < nt)
    def _():
        issue(i + 1, 1 - slot)

    for k in range(2):
        pltpu.make_async_copy(y_hbm.at[pl.ds(0, tm * n_slab), :], yg_ref.at[slot, k],
                              sem.at[slot]).wait()

    wt = wt_ref[...]
    w0 = wt[:, 0:1]
    w1 = wt[:, 1:2]
    ss = jnp.zeros((tm, 1), F32)
    for c in range(n_slab):
        sl = slice(c * LANES, (c + 1) * LANES)
        rows = pl.ds(c, tm, stride=n_slab)
        xs = x_ref[:, sl] + w0 * yg_ref[slot, 0, rows, :] + w1 * yg_ref[slot, 1, rows, :]
        o_ref[:, sl] = xs
        ss = ss + jnp.sum(xs * xs, axis=-1, keepdims=True)
    o_ref[...] = o_ref[...] * lax.rsqrt(ss * (1.0 / d) + RMS_EPS) * g_ref[...]


def _final(slot0, slot1, x, wts, g, y_sorted, tm):
    t, d = x.shape
    grid_spec = pltpu.PrefetchScalarGridSpec(
        num_scalar_prefetch=2,
        grid=(t // tm,),
        in_specs=[pl.BlockSpec((tm, d), lambda i, s0, s1: (i, 0)),
                  pl.BlockSpec((tm, LANES), lambda i, s0, s1: (i, 0)),
                  pl.BlockSpec((1, d), lambda i, s0, s1: (0, 0)),
                  pl.BlockSpec(memory_space=pl.ANY)],
        out_specs=pl.BlockSpec((tm, d), lambda i, s0, s1: (i, 0)),
        scratch_shapes=[pltpu.VMEM((2, 2, tm * (d // LANES), LANES), F32),
                        pltpu.SemaphoreType.DMA((2,))],
    )
    return pl.pallas_call(
        functools.partial(_final_kernel, tm=tm, nt=t // tm),
        grid_spec=grid_spec,
        out_shape=jax.ShapeDtypeStruct((t, d), F32),
        compiler_params=_cparams(("arbitrary",)),
        name="combine_final_norm",
    )(slot0, slot1, x, wts, g, y_sorted)


def _dispatch_plan(experts, tm):
    t = experts.shape[0]
    ef = experts.T.reshape(-1)
    onehot = (ef[:, None] == jnp.arange(N_EXPERTS, dtype=jnp.int32)[None, :]).astype(jnp.int32)
    csum = jnp.cumsum(onehot, axis=0)
    rank = jnp.take_along_axis(csum, ef[:, None], axis=1)[:, 0] - 1
    counts = csum[-1]
    tiles_e = (counts + tm - 1) // tm
    tile_end = jnp.cumsum(tiles_e)
    tile_start = tile_end - tiles_e
    n_tiles = tile_end[-1]
    nt_max = (2 * t) // tm + N_EXPERTS
    slot = tile_start[ef] * tm + rank
    tid = jnp.minimum(jnp.arange(nt_max, dtype=jnp.int32), n_tiles - 1)
    tile_expert = jnp.sum((tile_end[None, :] <= tid[:, None]).astype(jnp.int32), axis=1)
    tok = jnp.tile(jnp.arange(t, dtype=jnp.int32), 2)
    rows = jnp.zeros(((nt_max + 1) * tm,), jnp.int32).at[slot].set(tok)
    return slot[:t], slot[t:], rows, tile_expert, n_tiles.reshape(1).astype(jnp.int32)


def kernel(x, positions, norm_mix, w_in, mu_shift, w0, w2, a0, a2, g2, k_k, k_a, r_k, ln_x_w, ln_x_b, w_out, norm_ffn, router_group_w, router_group_b, router_expert_w, router_expert_b, w_gate, w_up, w_down, norm_final):
    bsz, seq, d = x.shape
    assert bsz == 1
    depth = w_in.shape[0]
    d_rwkv = k_k.shape[1]
    d_attn = w_out.shape[1] - d_rwkv
    rwkv_cols = mu_shift.shape[2]
    nr = -(-rwkv_cols // (2 * LANES)) * (2 * LANES)
    assert 3 * d_rwkv + 2 * LANES + GATE_LORA == rwkv_cols and nr == 3 * d_rwkv + 4 * LANES
    tm_moe = 512
    tf_moe = 512
    sb_attn = max(dil for _, dil in DILATED_PATTERNS) * Q_BLOCK
    assert seq % sb_attn == 0

    xt = x.reshape(seq, d)
    tabs = _rope_tables(positions.reshape(seq, 1), 512)
    for l in range(depth):
        w_r = jnp.pad(w_in[l][:, :rwkv_cols], ((0, 0), (0, nr - rwkv_cols))).astype(BF16)
        w_a = w_in[l][:, rwkv_cols:].astype(BF16)
        mu = jnp.pad(mu_shift[l], ((0, 0), (0, nr - rwkv_cols)))
        zl = jnp.zeros((DECAY_LORA, d_rwkv), F32)
        w2p = jnp.stack([jnp.concatenate([w2[l, 0], zl]), jnp.concatenate([zl, w2[l, 1]])]).astype(BF16)
        a2p = jnp.stack([jnp.concatenate([a2[l, 0], zl]), jnp.concatenate([zl, a2[l, 1]])]).astype(BF16)
        g2p = jnp.pad(g2[l], ((0, 2 * LANES - GATE_LORA), (0, 0))).astype(BF16)
        rk = r_k[l].reshape(1, d_rwkv)

        z_r = _norm_matmul(xt, norm_mix[l][None], w_r, 256)
        z_a = _norm_matmul_rope(xt, norm_mix[l][None], w_a, tabs, 256)
        yf, yb, bonus, gate = _rwkv_mixer(z_r, mu, w0[l], a0[l], w2p, a2p, g2p,
                                          k_k[l][None], k_a[l][None], rk, d_rwkv)
        y_attn = _dilated_attention(z_a, d_attn, sb_attn)
        x2 = _out_proj(xt, yf, yb, bonus, gate, ln_x_w[l][None], ln_x_b[l][None],
                       y_attn, w_out[l].astype(BF16), 256)

        w_rt = jnp.concatenate(
            [router_group_w[l], router_expert_w[l].transpose(1, 0, 2).reshape(d, N_EXPERTS)], axis=1)
        w_rt = jnp.pad(w_rt, ((0, 0), (0, LANES - w_rt.shape[1])))
        rt_hi = w_rt.astype(BF16)
        rt_lo = (w_rt - rt_hi.astype(F32)).astype(BF16)
        b_rt = jnp.concatenate([router_group_b[l], router_expert_b[l].reshape(-1)])
        b_rt = jnp.pad(b_rt, (0, LANES - b_rt.shape[0]))[None]
        h, idx, wts = _router(x2, norm_ffn[l][None], jnp.stack([rt_hi, rt_lo]), b_rt, 256)

        slot0, slot1, rows, tile_expert, n_tiles = _dispatch_plan(idx[:, :2], tm_moe)
        y_sorted = _moe(h, tile_expert, n_tiles, rows, w_gate[l], w_up[l], w_down[l], tm_moe, tf_moe)
        is_last = l == depth - 1
        assert is_last, "the combine kernel applies the final norm; depth must be 1"
        xt = _final(slot0, slot1, x2, wts, norm_final[None], y_sorted, 256)
    return xt.reshape(bsz, seq, d)
```

```python
import functools

import jax
import jax.numpy as jnp
from jax import lax
from jax.experimental import pallas as pl
from jax.experimental.pallas import tpu as pltpu

HEAD_DIM = 64
LANES = 128
SUBLANES = 8
DECAY_LORA = 64
ICLR_LORA = 64
GATE_LORA = 160
GN_EPS = 64e-5
RMS_EPS = 1e-6
ROPE_THETA = 500000.0
ROPE_DIM = HEAD_DIM // 4
DILATED_PATTERNS = ((128, 1), (512, 4), (2048, 16))
Q_BLOCK = 128
N_GROUPS = 4
EXPERTS_PER_GROUP = 8
N_EXPERTS = N_GROUPS * EXPERTS_PER_GROUP
NEG_BIG = -1e30
CHUNK = 64
GATHER_CHUNK = 64
GATHER_UNROLL = 8
VMEM_LIMIT = 56 * 1024 * 1024

BF16 = jnp.bfloat16
F32 = jnp.float32


def _cparams(sem):
    return pltpu.CompilerParams(dimension_semantics=sem, vmem_limit_bytes=VMEM_LIMIT)


def _dot(a, b):
    return jnp.dot(a.astype(BF16), b.astype(BF16), preferred_element_type=F32)


def _dot_nt(a, b):
    return lax.dot_general(a.astype(BF16), b.astype(BF16), (((1,), (1,)), ((), ())),
                           preferred_element_type=F32)


def _dot_tn(a, b):
    return lax.dot_general(a.astype(BF16), b.astype(BF16), (((0,), (0,)), ((), ())),
                           preferred_element_type=F32)


def _split2(x):
    hi = x.astype(BF16)
    lo = (x - hi.astype(F32)).astype(BF16)
    return hi, lo


def _split3(x):
    hi = x.astype(BF16)
    r1 = x - hi.astype(F32)
    mid = r1.astype(BF16)
    lo = (r1 - mid.astype(F32)).astype(BF16)
    return hi, mid, lo


def _head_ones():
    i = lax.broadcasted_iota(jnp.int32, (LANES, LANES), 0) // HEAD_DIM
    j = lax.broadcasted_iota(jnp.int32, (LANES, LANES), 1) // HEAD_DIM
    return (i == j).astype(BF16)


def _head_sum(x, ones_bd):
    hi, lo = _split2(x)
    return (jnp.dot(hi, ones_bd, preferred_element_type=F32)
            + jnp.dot(lo, ones_bd, preferred_element_type=F32))


def _sigmoid(x):
    return 1.0 / (1.0 + jnp.exp(-x))


def _w_in_split_kernel(w_ref, wr_ref, wa_ref, *, rwkv_cols):
    w = w_ref[...]
    nr = wr_ref.shape[1]
    wr_ref[:, 0:rwkv_cols] = w[:, 0:rwkv_cols].astype(BF16)
    wr_ref[:, rwkv_cols:nr] = jnp.zeros((w.shape[0], nr - rwkv_cols), BF16)
    wa_ref[...] = w[:, rwkv_cols:].astype(BF16)


def _w_in_split(w, rwkv_cols, nr, tk):
    d, n = w.shape
    return pl.pallas_call(
        functools.partial(_w_in_split_kernel, rwkv_cols=rwkv_cols),
        grid=(d // tk,),
        in_specs=[pl.BlockSpec((tk, n), lambda i: (i, 0))],
        out_specs=[pl.BlockSpec((tk, nr), lambda i: (i, 0)),
                   pl.BlockSpec((tk, n - rwkv_cols), lambda i: (i, 0))],
        out_shape=[jax.ShapeDtypeStruct((d, nr), BF16),
                   jax.ShapeDtypeStruct((d, n - rwkv_cols), BF16)],
        compiler_params=_cparams(("arbitrary",)),
        name="w_in_split",
    )(w)


def _norm_matmul_kernel(x_ref, g_ref, w_ref, o_ref):
    x = x_ref[...]
    ms = jnp.mean(x * x, axis=-1, keepdims=True)
    h = x * lax.rsqrt(ms + RMS_EPS) * g_ref[...]
    o_ref[...] = jnp.dot(h.astype(BF16), w_ref[...], preferred_element_type=F32)


def _norm_matmul(x, g, w, tm):
    t, d = x.shape
    n = w.shape[1]
    return pl.pallas_call(
        _norm_matmul_kernel,
        grid=(t // tm,),
        in_specs=[pl.BlockSpec((tm, d), lambda i: (i, 0)),
                  pl.BlockSpec((1, d), lambda i: (0, 0)),
                  pl.BlockSpec((d, n), lambda i: (0, 0))],
        out_specs=pl.BlockSpec((tm, n), lambda i: (i, 0)),
        out_shape=jax.ShapeDtypeStruct((t, n), F32),
        compiler_params=_cparams(("arbitrary",)),
        name="norm_in_proj",
    )(x, g, w)


def _rwkv_kernel(zf_ref, zfp_ref, zfn_ref, zb_ref, zbp_ref, zbn_ref,
                 mu_ref, w0_ref, a0_ref, w2_ref, a2_ref, g2_ref, kk_ref, ka_ref, rk_ref,
                 yf_ref, yb_ref, bonus_ref, gate_ref,
                 zs_ref, prep_ref, h_ref, *, nc, d_rwkv):
    c = pl.program_id(0)
    C = CHUNK
    n_hp = d_rwkv // LANES

    @pl.when(c == 0)
    def _():
        h_ref[...] = jnp.zeros_like(h_ref)

    row = lax.broadcasted_iota(jnp.int32, (C, 1), 0)
    mu_p = mu_ref[0:1, :]
    mu_n = mu_ref[1:2, :]
    srcs = ((zf_ref, zfp_ref, zfn_ref, c), (zb_ref, zbp_ref, zbn_ref, nc - 1 - c))
    for d, (z_ref, zp_ref, zn_ref, chunk) in enumerate(srcs):
        z = z_ref[...]
        prev_row = jnp.where(chunk == 0, 0.0, zp_ref[SUBLANES - 1:SUBLANES, :])
        next_row = jnp.where(chunk == nc - 1, 0.0, zn_ref[0:1, :])
        z_prev = jnp.where(row == 0, prev_row, pltpu.roll(z, 1, 0))
        z_next = jnp.where(row == C - 1, next_row, pltpu.roll(z, C - 1, 0))
        zs_ref[d] = z + mu_p * (z_prev - z) + mu_n * (z_next - z)

    ones_bd = _head_ones()
    lane = lax.broadcasted_iota(jnp.int32, (1, LANES), 1)
    m_a = lane < HEAD_DIM
    ti = lax.broadcasted_iota(jnp.int32, (C, C), 0)
    tj = lax.broadcasted_iota(jnp.int32, (C, C), 1)
    bi = lax.broadcasted_iota(jnp.int32, (2 * C, 2 * C), 0)
    bj = lax.broadcasted_iota(jnp.int32, (2 * C, 2 * C), 1)
    same_blk = (bi // C) == (bj // C)
    eye = bi == bj
    cum_mat = ((tj <= ti).astype(BF16), (tj >= ti).astype(BF16))
    strict = (same_blk & ((bj % C) < (bi % C)), same_blk & ((bj % C) > (bi % C)))
    incl = (same_blk & ((bj % C) <= (bi % C)), same_blk & ((bj % C) >= (bi % C)))
    last_row = (C - 1, 0)
    off_wl = 3 * d_rwkv
    off_al = off_wl + LANES
    off_gl = off_al + LANES

    def to_rows(x):
        return jnp.concatenate([x[:, s * LANES:(s + 1) * LANES] for s in range(n_hp)], axis=0)

    def to_lanes(x):
        return jnp.concatenate([x[s * C:(s + 1) * C] for s in range(n_hp)], axis=1)

    def bd(x):
        return jnp.concatenate([jnp.where(m_a, x, 0), jnp.where(m_a, 0, x)], axis=0)

    def unbd(x):
        return x[0:C] + x[C:2 * C]

    w_c = []
    for d in range(2):
        r = zs_ref[d, :, 0:d_rwkv]
        k = zs_ref[d, :, d_rwkv:2 * d_rwkv]
        kkr = k * kk_ref[...]
        sums = [to_rows(kkr * kkr)]
        if d == 0:
            sums.append(to_rows(r * k * rk_ref[...]))
        hs = _head_sum(jnp.concatenate(sums, axis=0), ones_bd)
        kk = kkr * lax.rsqrt(to_lanes(hs[0:n_hp * C]) + 1e-12)
        if d == 0:
            bonus_ref[...] = to_lanes(hs[n_hp * C:2 * n_hp * C]) * zs_ref[0, :, 2 * d_rwkv:3 * d_rwkv]
            gate_ref[...] = _dot(_sigmoid(zs_ref[0, :, off_gl:off_gl + 2 * LANES]), g2_ref[...])
        u = w0_ref[d:d + 1, :] + _dot(jnp.tanh(zs_ref[d, :, off_wl:off_wl + LANES]), w2_ref[d])
        w_log = -(jnp.maximum(-u, 0.0) + jnp.log1p(jnp.exp(-jnp.abs(u)))) - 0.5
        logd = -jnp.exp(w_log)
        a = _sigmoid(a0_ref[d:d + 1, :] + _dot(zs_ref[d, :, off_al:off_al + LANES], a2_ref[d]))
        kd = k * (1.0 + (a - 1.0) * ka_ref[...])
        b = kk * a
        l_hi, l_mid, l_lo = _split3(logd)
        cm = cum_mat[d]
        cum = (jnp.dot(cm, l_hi, preferred_element_type=F32)
               + jnp.dot(cm, l_mid, preferred_element_type=F32)
               + jnp.dot(cm, l_lo, preferred_element_type=F32))
        cum_c = cum[last_row[d]:last_row[d] + 1, :]
        w_inv = jnp.exp(-cum)
        w_c.append(jnp.exp(cum_c))
        w_end = w_c[d] * w_inv
        prep_ref[d, 0] = -kk * jnp.exp(cum - logd)
        prep_ref[d, 1] = r * jnp.exp(cum)
        prep_ref[d, 2] = b * w_inv
        prep_ref[d, 3] = kd * w_inv
        prep_ref[d, 4] = b * w_end
        prep_ref[d, 5] = kd * w_end

    blocks = [(d, hp) for hp in range(n_hp) for d in range(2)]

    def slab(d, i, hp):
        return prep_ref[d, i, :, hp * LANES:(hp + 1) * LANES]

    incl2 = tuple(jnp.concatenate([m, m], axis=1) for m in incl)
    eye_f = eye.astype(F32)
    xs, aks, lows, z0s, vbds = [], [], [], [], []
    for d, hp in blocks:
        at = bd(slab(d, 0, hp).astype(BF16))
        rt = bd(slab(d, 1, hp).astype(BF16))
        bt = slab(d, 2, hp).astype(BF16)
        kt = slab(d, 3, hp).astype(BF16)
        lhs = jnp.concatenate([at, rt], axis=0)
        rhs = jnp.concatenate([bt, bt, kt, kt], axis=0)
        g = lax.dot_general(lhs, rhs, (((1,), (1,)), ((), ())), preferred_element_type=F32)
        xs.append(jnp.where(strict[d], g[0:2 * C, 0:2 * C], 0.0))
        aks.append(jnp.where(strict[d], g[0:2 * C, 2 * C:4 * C], 0.0).astype(BF16))
        lows.append(jnp.where(incl2[d], g[2 * C:4 * C, :], 0.0).astype(BF16))
    for i, (d, hp) in enumerate(blocks):
        vbd = bd(zs_ref[d, :, 2 * d_rwkv + hp * LANES:2 * d_rwkv + (hp + 1) * LANES].astype(BF16))
        vbds.append(vbd)
        akv = jnp.dot(aks[i], vbd, preferred_element_type=F32)
        z0s.append(jnp.concatenate([bd(slab(d, 0, hp).astype(BF16)), akv.astype(BF16)], axis=1))

    n_sq = C.bit_length() - 1
    ts = [eye_f + x for x in xs]
    ps = [x.astype(BF16) for x in xs]
    for j in range(1, n_sq):
        last = j == n_sq - 1
        for i in range(len(blocks)):
            if j == 1:
                ps[i] = jnp.dot(ps[i], ps[i], preferred_element_type=F32).astype(BF16)
            tb = ts[i].astype(BF16)
            if last:
                ts[i] = ts[i] + jnp.dot(ps[i], tb, preferred_element_type=F32)
            else:
                res = jnp.dot(ps[i], jnp.concatenate([ps[i], tb], axis=1), preferred_element_type=F32)
                ts[i] = ts[i] + res[:, LANES:2 * LANES]
                ps[i] = res[:, 0:LANES].astype(BF16)

    wts = []
    for i in range(len(blocks)):
        zz = jnp.dot(ts[i].astype(BF16), z0s[i], preferred_element_type=F32)
        bottom = jnp.concatenate([jnp.zeros((2 * C, LANES), BF16), vbds[i]], axis=1)
        wts.append(jnp.concatenate([zz.astype(BF16), bottom], axis=0))
    outs = []
    for i, (d, hp) in enumerate(blocks):
        bb_t = bd(slab(d, 4, hp)).T.astype(BF16)
        kb_t = bd(slab(d, 5, hp)).T.astype(BF16)
        lhs = jnp.concatenate([lows[i], jnp.concatenate([bb_t, kb_t], axis=1)], axis=0)
        outs.append(jnp.dot(lhs, wts[i], preferred_element_type=F32))
    for i, (d, hp) in enumerate(blocks):
        sl = pl.ds(hp * LANES, LANES)
        o = outs[i]
        r_hat = bd(slab(d, 1, hp)) + o[0:2 * C, 0:LANES]
        p_mat = eye_f * w_c[d][:, hp * LANES:(hp + 1) * LANES] + o[2 * C:4 * C, 0:LANES]
        lhs = jnp.concatenate([r_hat, p_mat], axis=0).astype(BF16)
        res = jnp.dot(lhs, h_ref[d, hp].astype(BF16), preferred_element_type=F32)
        y = unbd(res[0:2 * C] + o[0:2 * C, LANES:2 * LANES])
        h_ref[d, hp] = res[2 * C:4 * C] + o[2 * C:4 * C, LANES:2 * LANES]
        if d == 0:
            yf_ref[:, sl] = y
        else:
            yb_ref[:, sl] = y


def _rwkv_mixer(z, mu, w0, a0, w2p, a2p, g2p, k_k, k_a, r_k, d_rwkv):
    t, nr = z.shape
    C = CHUNK
    nc = t // C
    hb = C // SUBLANES
    n_hp = d_rwkv // LANES
    last_hblk = t // SUBLANES - 1

    def cur(fn):
        return pl.BlockSpec((C, nr), lambda c: (fn(c), 0))

    def prev(fn):
        return pl.BlockSpec((SUBLANES, nr), lambda c: (jnp.maximum(fn(c) * hb - 1, 0), 0))

    def nxt(fn):
        return pl.BlockSpec((SUBLANES, nr), lambda c: (jnp.minimum((fn(c) + 1) * hb, last_hblk), 0))

    fwd = lambda c: c
    bwd = lambda c: nc - 1 - c

    def full(a):
        nd = a.ndim
        return pl.BlockSpec(a.shape, lambda c: (0,) * nd)

    out_spec_f = pl.BlockSpec((C, d_rwkv), lambda c: (c, 0))
    out_spec_b = pl.BlockSpec((C, d_rwkv), lambda c: (nc - 1 - c, 0))
    out_sd = jax.ShapeDtypeStruct((t, d_rwkv), F32)
    params = (mu, w0, a0, w2p, a2p, g2p, k_k, k_a, r_k)
    return pl.pallas_call(
        functools.partial(_rwkv_kernel, nc=nc, d_rwkv=d_rwkv),
        grid=(nc,),
        in_specs=[cur(fwd), prev(fwd), nxt(fwd), cur(bwd), prev(bwd), nxt(bwd)]
                 + [full(p) for p in params],
        out_specs=[out_spec_f, out_spec_b, out_spec_f, out_spec_f],
        out_shape=[out_sd, out_sd, out_sd, out_sd],
        scratch_shapes=[pltpu.VMEM((2, C, nr), F32),
                        pltpu.VMEM((2, 6, C, d_rwkv), F32),
                        pltpu.VMEM((2, n_hp, LANES, LANES), F32)],
        compiler_params=_cparams(("arbitrary",)),
        name="rwkv7_chunk_scan",
    )(z, z, z, z, z, z, *params)


def _rope_table_kernel(pos_ref, inv_ref, c_ref, s1_ref, s2_ref):
    half = ROPE_DIM // 2
    pos = pos_ref[...].astype(F32)
    j = lax.broadcasted_iota(jnp.int32, (1, LANES), 1) % HEAD_DIM
    ang = pos * inv_ref[...]
    cs = jnp.cos(ang)
    sn = jnp.sin(ang)
    c_ref[...] = jnp.where(j < ROPE_DIM, cs, 1.0)
    s1_ref[...] = jnp.where(j < half, -sn, 0.0)
    s2_ref[...] = jnp.where((j >= half) & (j < ROPE_DIM), sn, 0.0)


def _rope_tables(positions, tm):
    t = positions.shape[0]
    half = ROPE_DIM // 2
    inv_freq = jnp.power(ROPE_THETA, -jnp.arange(half, dtype=F32) * 2.0 / ROPE_DIM)
    inv_lane = jnp.tile(inv_freq, LANES // half)[None]
    sd = jax.ShapeDtypeStruct((t, LANES), F32)
    spec = pl.BlockSpec((tm, LANES), lambda i: (i, 0))
    return pl.pallas_call(
        _rope_table_kernel,
        grid=(t // tm,),
        in_specs=[pl.BlockSpec((tm, 1), lambda i: (i, 0)),
                  pl.BlockSpec((1, LANES), lambda i: (0, 0))],
        out_specs=[spec, spec, spec],
        out_shape=[sd, sd, sd],
        compiler_params=_cparams(("arbitrary",)),
        name="rope_tables",
    )(positions, inv_lane)


def _rot(x, c, s1, s2):
    half = ROPE_DIM // 2
    return x * c + pltpu.roll(x, LANES - half, 1) * s1 + pltpu.roll(x, half, 1) * s2


def _norm_matmul_rope_kernel(x_ref, g_ref, w_ref, c_ref, s1_ref, s2_ref, o_ref, *, n_rot):
    x = x_ref[...]
    ms = jnp.mean(x * x, axis=-1, keepdims=True)
    h = x * lax.rsqrt(ms + RMS_EPS) * g_ref[...]
    z = jnp.dot(h.astype(BF16), w_ref[...], preferred_element_type=F32)
    c = c_ref[...]
    s1 = s1_ref[...]
    s2 = s2_ref[...]
    for s in range(z.shape[1] // LANES):
        sl = slice(s * LANES, (s + 1) * LANES)
        zs = z[:, sl]
        if s < n_rot:
            zs = _rot(zs, c, s1, s2)
            if s < n_rot // 2:
                zs = zs * (HEAD_DIM ** -0.5)
        o_ref[:, sl] = zs


def _norm_matmul_rope(x, g, w, tabs, tm):
    t, d = x.shape
    n = w.shape[1]
    tab = pl.BlockSpec((tm, LANES), lambda i: (i, 0))
    return pl.pallas_call(
        functools.partial(_norm_matmul_rope_kernel, n_rot=2 * (n // 3) // LANES),
        grid=(t // tm,),
        in_specs=[pl.BlockSpec((tm, d), lambda i: (i, 0)),
                  pl.BlockSpec((1, d), lambda i: (0, 0)),
                  pl.BlockSpec((d, n), lambda i: (0, 0)), tab, tab, tab],
        out_specs=pl.BlockSpec((tm, n), lambda i: (i, 0)),
        out_shape=jax.ShapeDtypeStruct((t, n), F32),
        compiler_params=_cparams(("arbitrary",)),
        name="norm_in_proj_rope",
    )(x, g, w, *tabs)


def _attn_kernel(q_ref, k_ref, v_ref, o_ref, op_ref, lp_ref, *, seq_t, sb):
    jsb = pl.program_id(1)
    lane = lax.broadcasted_iota(jnp.int32, (1, LANES), 1)
    m_a = lane < HEAD_DIM
    for p, (window, dil) in enumerate(DILATED_PATTERNS):
        radius = window // (2 * dil)
        win = Q_BLOCK + 2 * radius
        seq = seq_t // dil
        nq = sb // (dil * Q_BLOCK)
        qi = lax.broadcasted_iota(jnp.int32, (Q_BLOCK, win), 0)
        kj = lax.broadcasted_iota(jnp.int32, (Q_BLOCK, win), 1)

        def rows(start, size, dil=dil):
            return pl.ds(start, size) if dil == 1 else pl.ds(start, size, stride=dil)

        def body(idx, carry, dil=dil, radius=radius, win=win, seq=seq, nq=nq, qi=qi, kj=kj,
                 rows=rows, p=p):
            r = idx // nq
            jj = idx % nq
            q_l0 = (jsb * nq + jj) * Q_BLOCK
            start_l = jnp.clip(q_l0 - radius, 0, seq - win)
            q_rows = rows(r + dil * Q_BLOCK * jj, Q_BLOCK)
            k_rows = rows(r + dil * start_l, win)
            q = q_ref[q_rows, :].astype(BF16)
            kw = k_ref[k_rows, :].astype(BF16)
            vw = v_ref[k_rows, :].astype(BF16)
            valid = jnp.abs((q_l0 + qi) - (start_l + kj)) <= radius
            one = jnp.ones((), BF16)
            halves = ((jnp.where(m_a, q, 0), jnp.where(m_a, vw, one)),
                      (jnp.where(m_a, 0, q), jnp.where(m_a, one, vw)))
            res, mx = [], []
            for qh, vh in halves:
                sc = lax.dot_general(qh, kw, (((1,), (1,)), ((), ())), preferred_element_type=F32)
                sc = jnp.where(valid, sc, NEG_BIG)
                m = jnp.max(sc, axis=-1, keepdims=True)
                pexp = jnp.exp(sc - m).astype(BF16)
                res.append(jnp.dot(pexp, vh, preferred_element_type=F32))
                mx.append(m)
            num = jnp.where(m_a, res[0], res[1])
            den = pltpu.roll(jnp.where(m_a, res[1], res[0]), HEAD_DIM, 1)
            op_ref[p, q_rows, :] = num / den
            lp_ref[p, q_rows, :] = jnp.where(m_a, mx[0], mx[1]) + jnp.log(den)
            return carry

        lax.fori_loop(0, dil * nq, body, 0, unroll=4)

    l1 = lp_ref[0]
    l2 = lp_ref[1]
    l3 = lp_ref[2]
    m = jnp.maximum(jnp.maximum(l1, l2), l3)
    e1 = jnp.exp(l1 - m)
    e2 = jnp.exp(l2 - m)
    e3 = jnp.exp(l3 - m)
    o_ref[...] = (e1 * op_ref[0] + e2 * op_ref[1] + e3 * op_ref[2]) / (e1 + e2 + e3)


def _dilated_attention(z_attn, d_attn, sb):
    t = z_attn.shape[0]
    n_hp = d_attn // LANES
    n_pat = len(DILATED_PATTERNS)
    return pl.pallas_call(
        functools.partial(_attn_kernel, seq_t=t, sb=sb),
        grid=(n_hp, t // sb),
        in_specs=[pl.BlockSpec((sb, LANES), lambda h, j: (j, h)),
                  pl.BlockSpec((t, LANES), lambda h, j: (0, n_hp + h)),
                  pl.BlockSpec((t, LANES), lambda h, j: (0, 2 * n_hp + h))],
        out_specs=pl.BlockSpec((sb, LANES), lambda h, j: (j, h)),
        out_shape=jax.ShapeDtypeStruct((t, d_attn), F32),
        scratch_shapes=[pltpu.VMEM((n_pat, sb, LANES), F32), pltpu.VMEM((n_pat, sb, LANES), F32)],
        compiler_params=_cparams(("arbitrary", "arbitrary")),
        name="dilated_attention",
    )(z_attn, z_attn, z_attn)


def _out_proj_kernel(x_ref, yf_ref, yb_ref, bonus_ref, gate_ref, lnw_ref, lnb_ref,
                     ya_ref, w_ref, out_ref, yr_ref, *, d_rwkv):
    ones_bd = _head_ones()
    for hp in range(d_rwkv // LANES):
        sl = pl.ds(hp * LANES, LANES)
        y = yf_ref[:, sl] + yb_ref[:, sl]
        mean = _head_sum(y, ones_bd) * (1.0 / HEAD_DIM)
        yc = y - mean
        var = _head_sum(yc * yc, ones_bd) * (1.0 / HEAD_DIM)
        yn = yc * lax.rsqrt(var + GN_EPS) * lnw_ref[:, sl] + lnb_ref[:, sl]
        yr_ref[:, sl] = ((yn + bonus_ref[:, sl]) * gate_ref[:, sl]).astype(BF16)
    acc = jnp.dot(yr_ref[...], w_ref[0:d_rwkv, :], preferred_element_type=F32)
    acc += jnp.dot(ya_ref[...].astype(BF16), w_ref[d_rwkv:, :], preferred_element_type=F32)
    out_ref[...] = x_ref[...] + acc


def _out_proj(x, yf, yb, bonus, gate, ln_w, ln_b, y_attn, w_out, tm):
    t, d = x.shape
    d_rwkv = yf.shape[1]
    d_attn = y_attn.shape[1]
    row = lambda n: pl.BlockSpec((tm, n), lambda i: (i, 0))
    const = lambda a: pl.BlockSpec(a.shape, lambda i: (0, 0))
    return pl.pallas_call(
        functools.partial(_out_proj_kernel, d_rwkv=d_rwkv),
        grid=(t // tm,),
        in_specs=[row(d)] + [row(d_rwkv)] * 4 + [const(ln_w), const(ln_b)]
                 + [row(d_attn), const(w_out)],
        out_specs=row(d),
        out_shape=jax.ShapeDtypeStruct((t, d), F32),
        scratch_shapes=[pltpu.VMEM((tm, d_rwkv), BF16)],
        compiler_params=_cparams(("arbitrary",)),
        name="merge_out_proj",
    )(x, yf, yb, bonus, gate, ln_w, ln_b, y_attn, w_out)


def _router_kernel(x_ref, g_ref, w_ref, b_ref, h_ref, idx_ref, wt_ref):
    x = x_ref[...]
    ms = jnp.mean(x * x, axis=-1, keepdims=True)
    h = x * lax.rsqrt(ms + RMS_EPS) * g_ref[...]
    h_ref[...] = h
    h_hi, h_lo = _split2(h)
    w_hi = w_ref[0]
    w_lo = w_ref[1]
    logits = (jnp.dot(h_hi, w_hi, preferred_element_type=F32)
              + jnp.dot(h_hi, w_lo, preferred_element_type=F32)
              + jnp.dot(h_lo, w_hi, preferred_element_type=F32)) + b_ref[...]
    lane = lax.broadcasted_iota(jnp.int32, logits.shape, 1).astype(F32)
    big = jnp.float32(LANES)
    is_g = lane < N_GROUPS
    gl = jnp.where(is_g, logits, NEG_BIG)
    gmax = jnp.max(gl, axis=-1, keepdims=True)
    gsel = jnp.min(jnp.where(is_g & (gl == gmax), lane, big), axis=-1, keepdims=True)
    g1 = 1.0 / jnp.sum(jnp.where(is_g, jnp.exp(gl - gmax), 0.0), axis=-1, keepdims=True)
    lo = N_GROUPS + gsel * EXPERTS_PER_GROUP
    in_grp = (lane >= lo) & (lane < lo + EXPERTS_PER_GROUP)
    el = jnp.where(in_grp, logits, NEG_BIG)
    v1 = jnp.max(el, axis=-1, keepdims=True)
    i1 = jnp.min(jnp.where(in_grp & (el == v1), lane, big), axis=-1, keepdims=True)
    rest = in_grp & (lane != i1)
    el2 = jnp.where(rest, logits, NEG_BIG)
    v2 = jnp.max(el2, axis=-1, keepdims=True)
    i2 = jnp.min(jnp.where(rest & (el2 == v2), lane, big), axis=-1, keepdims=True)
    e2 = jnp.exp(v2 - v1)
    ww1 = 1.0 / (1.0 + e2)
    ww2 = e2 / (1.0 + e2)
    idx = jnp.where(lane == 0, i1 - N_GROUPS, jnp.where(lane == 1, i2 - N_GROUPS, 0.0))
    idx_ref[...] = idx.astype(jnp.int32)
    wt_ref[...] = jnp.where(lane == 0, g1 * ww1, jnp.where(lane == 1, g1 * ww2, 0.0))


def _router(x, g, w_split, b_pad, tm):
    t, d = x.shape
    return pl.pallas_call(
        _router_kernel,
        grid=(t // tm,),
        in_specs=[pl.BlockSpec((tm, d), lambda i: (i, 0)),
                  pl.BlockSpec((1, d), lambda i: (0, 0)),
                  pl.BlockSpec((2, d, LANES), lambda i: (0, 0, 0)),
                  pl.BlockSpec((1, LANES), lambda i: (0, 0))],
        out_specs=[pl.BlockSpec((tm, d), lambda i: (i, 0)),
                   pl.BlockSpec((tm, LANES), lambda i: (i, 0)),
                   pl.BlockSpec((tm, LANES), lambda i: (i, 0))],
        out_shape=[jax.ShapeDtypeStruct((t, d), F32),
                   jax.ShapeDtypeStruct((t, LANES), jnp.int32),
                   jax.ShapeDtypeStruct((t, LANES), F32)],
        compiler_params=_cparams(("arbitrary",)),
        name="router",
    )(x, g, w_split, b_pad)


def _moe_kernel(te_ref, nt_ref, nv_ref, rows_ref, h_hbm, wg_ref, wu_ref, wd_ref, o_ref,
                xg_ref, xb_ref, acc_ref, sem, *, tm, nf):
    i = pl.program_id(0)
    f = pl.program_id(1)
    nt = nt_ref[0]
    active = i < nt
    slot = i % 2
    n_slab = xg_ref.shape[2] // LANES

    def issue_tile(tile, buf):
        def body(g, carry):
            for u in range(GATHER_UNROLL):
                j = g * GATHER_UNROLL + u
                src = h_hbm.at[pl.ds(rows_ref[tile * tm + j], 1), :]
                pltpu.make_async_copy(src, xg_ref.at[buf, pl.ds(j, 1), :],
                                      sem.at[buf]).start(priority=1)
            return carry
        lax.fori_loop(0, nv_ref[tile] // GATHER_UNROLL, body, 0)

    def wait_tile(tile, buf):
        def body(g, carry):
            pltpu.make_async_copy(h_hbm.at[pl.ds(0, GATHER_CHUNK), :],
                                  xg_ref.at[buf, pl.ds(0, GATHER_CHUNK), :], sem.at[buf]).wait()
            return carry
        lax.fori_loop(0, nv_ref[tile] // GATHER_CHUNK, body, 0)

    @pl.when((i == 0) & (f == 0))
    def _():
        xg_ref[...] = jnp.zeros_like(xg_ref)
        issue_tile(0, 0)

    @pl.when(active & (f == 0))
    def _():
        wait_tile(i, slot)
        xb_ref[...] = xg_ref[slot].astype(BF16)
        issue_tile(i + 1, 1 - slot)

    @pl.when(active)
    def _():
        xb = xb_ref[...]
        gate = jnp.dot(xb, wg_ref[...].astype(BF16), preferred_element_type=F32)
        up = jnp.dot(xb, wu_ref[...].astype(BF16), preferred_element_type=F32)
        hid = (gate * _sigmoid(gate)) * up
        part = jnp.dot(hid.astype(BF16), wd_ref[...].astype(BF16), preferred_element_type=F32)

        @pl.when(f == 0)
        def _():
            acc_ref[...] = part

        @pl.when((f != 0) & (f != nf - 1))
        def _():
            acc_ref[...] += part

        @pl.when(f == nf - 1)
        def _():
            for c in range(n_slab):
                sl = slice(c * LANES, (c + 1) * LANES)
                o_ref[pl.ds(c, tm, stride=n_slab), :] = acc_ref[:, sl] + part[:, sl]

    @pl.when(jnp.logical_not(active) & (f == 0))
    def _():
        o_ref[...] = jnp.zeros_like(o_ref)


def _moe(h, tile_expert, n_tiles, n_valid, rows, w_gate, w_up, w_down, tm, tf):
    t, d = h.shape
    n_e, _, d_e = w_gate.shape
    n_slab = d // LANES
    nt_max = tile_expert.shape[0]
    nf = d_e // tf
    assert nf >= 2 and tm % GATHER_CHUNK == 0 and GATHER_CHUNK % GATHER_UNROLL == 0
    assert rows.shape[0] == (nt_max + 1) * tm and n_valid.shape[0] == nt_max + 1

    def fidx(i, f, nt):
        return jnp.where(i < nt[0], f, nf - 1)

    grid_spec = pltpu.PrefetchScalarGridSpec(
        num_scalar_prefetch=4,
        grid=(nt_max, d_e // tf),
        in_specs=[pl.BlockSpec(memory_space=pl.ANY),
                  pl.BlockSpec((None, d, tf), lambda i, f, te, nt, nv, rw: (te[i], 0, fidx(i, f, nt))),
                  pl.BlockSpec((None, d, tf), lambda i, f, te, nt, nv, rw: (te[i], 0, fidx(i, f, nt))),
                  pl.BlockSpec((None, tf, d), lambda i, f, te, nt, nv, rw: (te[i], fidx(i, f, nt), 0))],
        out_specs=pl.BlockSpec((tm * n_slab, LANES), lambda i, f, te, nt, nv, rw: (i, 0)),
        scratch_shapes=[pltpu.VMEM((2, tm, d), F32), pltpu.VMEM((tm, d), BF16),
                        pltpu.VMEM((tm, d), F32), pltpu.SemaphoreType.DMA((2,))],
    )
    return pl.pallas_call(
        functools.partial(_moe_kernel, tm=tm, nf=nf),
        grid_spec=grid_spec,
        out_shape=jax.ShapeDtypeStruct((nt_max * tm * n_slab, LANES), F32),
        compiler_params=_cparams(("arbitrary", "arbitrary")),
        name="grouped_expert_mlp",
    )(tile_expert, n_tiles, n_valid, rows, h, w_gate, w_up, w_down)


def _final_kernel(s0_ref, s1_ref, x_ref, wt_ref, g_ref, y_hbm, o_ref, yg_ref, sem, *, tm, nt):
    i = pl.program_id(0)
    slot = i % 2
    d = x_ref.shape[1]
    n_slab = d // LANES

    def issue(tile, buf):
        def body(j, carry):
            for k, s_ref in enumerate((s0_ref, s1_ref)):
                src = y_hbm.at[pl.ds(pl.multiple_of(s_ref[tile * tm + j] * n_slab, n_slab), n_slab), :]
                dst = yg_ref.at[buf, k, pl.ds(pl.multiple_of(j * n_slab, n_slab), n_slab), :]
                pltpu.make_async_copy(src, dst, sem.at[buf]).start(priority=k)
            return carry
        lax.fori_loop(0, tm, body, 0, unroll=8)

    @pl.when(i == 0)
    def _():
        issue(0, 0)

    @pl.when(i + 1 < nt)
    def _():
        issue(i + 1, 1 - slot)

    for k in range(2):
        pltpu.make_async_copy(y_hbm.at[pl.ds(0, tm * n_slab), :], yg_ref.at[slot, k],
                              sem.at[slot]).wait()

    wt = wt_ref[...]
    w0 = wt[:, 0:1]
    w1 = wt[:, 1:2]
    ss = jnp.zeros((tm, 1), F32)
    for c in range(n_slab):
        sl = slice(c * LANES, (c + 1) * LANES)
        rows = pl.ds(c, tm, stride=n_slab)
        xs = x_ref[:, sl] + w0 * yg_ref[slot, 0, rows, :] + w1 * yg_ref[slot, 1, rows, :]
        o_ref[:, sl] = xs
        ss = ss + jnp.sum(xs * xs, axis=-1, keepdims=True)
    o_ref[...] = o_ref[...] * lax.rsqrt(ss * (1.0 / d) + RMS_EPS) * g_ref[...]


def _final(slot0, slot1, x, wts, g, y_sorted, tm):
    t, d = x.shape
    grid_spec = pltpu.PrefetchScalarGridSpec(
        num_scalar_prefetch=2,
        grid=(t // tm,),
        in_specs=[pl.BlockSpec((tm, d), lambda i, s0, s1: (i, 0)),
                  pl.BlockSpec((tm, LANES), lambda i, s0, s1: (i, 0)),
                  pl.BlockSpec((1, d), lambda i, s0, s1: (0, 0)),
                  pl.BlockSpec(memory_space=pl.ANY)],
        out_specs=pl.BlockSpec((tm, d), lambda i, s0, s1: (i, 0)),
        scratch_shapes=[pltpu.VMEM((2, 2, tm * (d // LANES), LANES), F32),
                        pltpu.SemaphoreType.DMA((2,))],
    )
    return pl.pallas_call(
        functools.partial(_final_kernel, tm=tm, nt=t // tm),
        grid_spec=grid_spec,
        out_shape=jax.ShapeDtypeStruct((t, d), F32),
        compiler_params=_cparams(("arbitrary",)),
        name="combine_final_norm",
    )(slot0, slot1, x, wts, g, y_sorted)


def _dispatch_plan(experts, tm):
    t = experts.shape[0]
    ef = experts.T.reshape(-1)
    onehot = (ef[:, None] == jnp.arange(N_EXPERTS, dtype=jnp.int32)[None, :]).astype(jnp.int32)
    csum = jnp.cumsum(onehot, axis=0)
    rank = jnp.take_along_axis(csum, ef[:, None], axis=1)[:, 0] - 1
    counts = csum[-1]
    tiles_e = (counts + tm - 1) // tm
    tile_end = jnp.cumsum(tiles_e)
    tile_start = tile_end - tiles_e
    n_tiles = tile_end[-1]
    nt_max = (2 * t) // tm + N_EXPERTS
    slot = tile_start[ef] * tm + rank
    tid = jnp.minimum(jnp.arange(nt_max, dtype=jnp.int32), n_tiles - 1)
    tile_expert = jnp.sum((tile_end[None, :] <= tid[:, None]).astype(jnp.int32), axis=1)
    tok = jnp.tile(jnp.arange(t, dtype=jnp.int32), 2)
    rows = jnp.zeros(((nt_max + 1) * tm,), jnp.int32).at[slot].set(tok)
    tix = jnp.arange(nt_max + 1, dtype=jnp.int32)
    te_all = jnp.concatenate([tile_expert, tile_expert[-1:]])
    left = counts[te_all] - (tix - tile_start[te_all]) * tm
    n_valid = jnp.where(tix < n_tiles, jnp.clip(left, 0, tm), 0)
    n_valid = (n_valid + GATHER_CHUNK - 1) // GATHER_CHUNK * GATHER_CHUNK
    return (slot[:t], slot[t:], rows, tile_expert, n_tiles.reshape(1).astype(jnp.int32),
            n_valid.astype(jnp.int32))


def kernel(x, positions, norm_mix, w_in, mu_shift, w0, w2, a0, a2, g2, k_k, k_a, r_k, ln_x_w, ln_x_b, w_out, norm_ffn, router_group_w, router_group_b, router_expert_w, router_expert_b, w_gate, w_up, w_down, norm_final):
    bsz, seq, d = x.shape
    assert bsz == 1
    depth = w_in.shape[0]
    d_rwkv = k_k.shape[1]
    d_attn = w_out.shape[1] - d_rwkv
    rwkv_cols = mu_shift.shape[2]
    nr = -(-rwkv_cols // (2 * LANES)) * (2 * LANES)
    assert 3 * d_rwkv + 2 * LANES + GATE_LORA == rwkv_cols and nr == 3 * d_rwkv + 4 * LANES
    tm_moe = 512
    tf_moe = 512
    sb_attn = max(dil for _, dil in DILATED_PATTERNS) * Q_BLOCK
    assert seq % sb_attn == 0

    xt = x.reshape(seq, d)
    tabs = _rope_tables(positions.reshape(seq, 1), 512)
    for l in range(depth):
        w_r, w_a = _w_in_split(w_in[l], rwkv_cols, nr, 256)
        mu = jnp.pad(mu_shift[l], ((0, 0), (0, nr - rwkv_cols)))
        zl = jnp.zeros((DECAY_LORA, d_rwkv), F32)
        w2p = jnp.stack([jnp.concatenate([w2[l, 0], zl]), jnp.concatenate([zl, w2[l, 1]])]).astype(BF16)
        a2p = jnp.stack([jnp.concatenate([a2[l, 0], zl]), jnp.concatenate([zl, a2[l, 1]])]).astype(BF16)
        g2p = jnp.pad(g2[l], ((0, 2 * LANES - GATE_LORA), (0, 0))).astype(BF16)
        rk = r_k[l].reshape(1, d_rwkv)

        z_r = _norm_matmul(xt, norm_mix[l][None], w_r, 256)
        z_a = _norm_matmul_rope(xt, norm_mix[l][None], w_a, tabs, 256)
        yf, yb, bonus, gate = _rwkv_mixer(z_r, mu, w0[l], a0[l], w2p, a2p, g2p,
                                          k_k[l][None], k_a[l][None], rk, d_rwkv)
        y_attn = _dilated_attention(z_a, d_attn, sb_attn)
        x2 = _out_proj(xt, yf, yb, bonus, gate, ln_x_w[l][None], ln_x_b[l][None],
                       y_attn, w_out[l].astype(BF16), 256)

        w_rt = jnp.concatenate(
            [router_group_w[l], router_expert_w[l].transpose(1, 0, 2).reshape(d, N_EXPERTS)], axis=1)
        w_rt = jnp.pad(w_rt, ((0, 0), (0, LANES - w_rt.shape[1])))
        rt_hi = w_rt.astype(BF16)
        rt_lo = (w_rt - rt_hi.astype(F32)).astype(BF16)
        b_rt = jnp.concatenate([router_group_b[l], router_expert_b[l].reshape(-1)])
        b_rt = jnp.pad(b_rt, (0, LANES - b_rt.shape[0]))[None]
        h, idx, wts = _router(x2, norm_ffn[l][None], jnp.stack([rt_hi, rt_lo]), b_rt, 256)

        slot0, slot1, rows, tile_expert, n_tiles, n_valid = _dispatch_plan(idx[:, :2], tm_moe)
        y_sorted = _moe(h, tile_expert, n_tiles, n_valid, rows, w_gate[l], w_up[l], w_down[l],
                        tm_moe, tf_moe)
        is_last = l == depth - 1
        assert is_last, "the combine kernel applies the final norm; depth must be 1"
        xt = _final(slot0, slot1, x2, wts, norm_final[None], y_sorted, 256)
    return xt.reshape(bsz, seq, d)
```

```python
import functools

import jax
import jax.numpy as jnp
from jax import lax
from jax.experimental import pallas as pl
from jax.experimental.pallas import tpu as pltpu

HEAD_DIM = 64
LANES = 128
SUBLANES = 8
DECAY_LORA = 64
ICLR_LORA = 64
GATE_LORA = 160
GN_EPS = 64e-5
RMS_EPS = 1e-6
ROPE_THETA = 500000.0
ROPE_DIM = HEAD_DIM // 4
DILATED_PATTERNS = ((128, 1), (512, 4), (2048, 16))
Q_BLOCK = 128
N_GROUPS = 4
EXPERTS_PER_GROUP = 8
N_EXPERTS = N_GROUPS * EXPERTS_PER_GROUP
NEG_BIG = -1e30
CHUNK = 64
GATHER_CHUNK = 64
GATHER_UNROLL = 8
VMEM_LIMIT = 56 * 1024 * 1024

BF16 = jnp.bfloat16
F32 = jnp.float32


def _cparams(sem):
    return pltpu.CompilerParams(dimension_semantics=sem, vmem_limit_bytes=VMEM_LIMIT)


def _dot(a, b):
    return jnp.dot(a.astype(BF16), b.astype(BF16), preferred_element_type=F32)


def _dot_nt(a, b):
    return lax.dot_general(a.astype(BF16), b.astype(BF16), (((1,), (1,)), ((), ())),
                           preferred_element_type=F32)


def _dot_tn(a, b):
    return lax.dot_general(a.astype(BF16), b.astype(BF16), (((0,), (0,)), ((), ())),
                           preferred_element_type=F32)


def _split2(x):
    hi = x.astype(BF16)
    lo = (x - hi.astype(F32)).astype(BF16)
    return hi, lo


def _split3(x):
    hi = x.astype(BF16)
    r1 = x - hi.astype(F32)
    mid = r1.astype(BF16)
    lo = (r1 - mid.astype(F32)).astype(BF16)
    return hi, mid, lo


def _head_ones():
    i = lax.broadcasted_iota(jnp.int32, (LANES, LANES), 0) // HEAD_DIM
    j = lax.broadcasted_iota(jnp.int32, (LANES, LANES), 1) // HEAD_DIM
    return (i == j).astype(BF16)


def _head_sum(x, ones_bd):
    hi, lo = _split2(x)
    return (jnp.dot(hi, ones_bd, preferred_element_type=F32)
            + jnp.dot(lo, ones_bd, preferred_element_type=F32))


def _sigmoid(x):
    return 1.0 / (1.0 + jnp.exp(-x))


def _w_in_split_kernel(w_ref, wr_ref, wa_ref, *, rwkv_cols):
    w = w_ref[...]
    nr = wr_ref.shape[1]
    wr_ref[:, 0:rwkv_cols] = w[:, 0:rwkv_cols].astype(BF16)
    wr_ref[:, rwkv_cols:nr] = jnp.zeros((w.shape[0], nr - rwkv_cols), BF16)
    wa_ref[...] = w[:, rwkv_cols:].astype(BF16)


def _w_in_split(w_all, layer, rwkv_cols, nr, tk):
    _, d, n = w_all.shape
    return pl.pallas_call(
        functools.partial(_w_in_split_kernel, rwkv_cols=rwkv_cols),
        grid=(d // tk,),
        in_specs=[pl.BlockSpec((None, tk, n), lambda i: (layer, i, 0))],
        out_specs=[pl.BlockSpec((tk, nr), lambda i: (i, 0)),
                   pl.BlockSpec((tk, n - rwkv_cols), lambda i: (i, 0))],
        out_shape=[jax.ShapeDtypeStruct((d, nr), BF16),
                   jax.ShapeDtypeStruct((d, n - rwkv_cols), BF16)],
        compiler_params=_cparams(("arbitrary",)),
        name="w_in_split",
    )(w_all)


def _norm_matmul_shift_kernel(x_ref, xp_ref, xn_ref, g_ref, w_ref, mu_ref, o_ref, *, nt):
    i = pl.program_id(0)
    tm = x_ref.shape[0]
    halo = xp_ref.shape[0]
    x = jnp.concatenate([xp_ref[...], x_ref[...], xn_ref[...]], axis=0)
    ms = jnp.mean(x * x, axis=-1, keepdims=True)
    h = x * lax.rsqrt(ms + RMS_EPS) * g_ref[...]
    z = jnp.dot(h.astype(BF16), w_ref[...], preferred_element_type=F32)
    rows = tm + 2 * halo
    zc = z[halo:halo + tm]
    z_prev = pltpu.roll(z, 1, 0)[halo:halo + tm]
    z_next = pltpu.roll(z, rows - 1, 0)[halo:halo + tm]
    row = lax.broadcasted_iota(jnp.int32, (tm, 1), 0)
    z_prev = jnp.where((i == 0) & (row == 0), 0.0, z_prev)
    z_next = jnp.where((i == nt - 1) & (row == tm - 1), 0.0, z_next)
    o_ref[...] = zc + mu_ref[0:1, :] * (z_prev - zc) + mu_ref[1:2, :] * (z_next - zc)


def _norm_matmul_shift(x, g, w, mu, tm):
    t, d = x.shape
    n = w.shape[1]
    nt = t // tm
    hb = tm // SUBLANES
    last = t // SUBLANES - 1
    return pl.pallas_call(
        functools.partial(_norm_matmul_shift_kernel, nt=nt),
        grid=(nt,),
        in_specs=[pl.BlockSpec((tm, d), lambda i: (i, 0)),
                  pl.BlockSpec((SUBLANES, d), lambda i: (jnp.maximum(i * hb - 1, 0), 0)),
                  pl.BlockSpec((SUBLANES, d), lambda i: (jnp.minimum((i + 1) * hb, last), 0)),
                  pl.BlockSpec((1, d), lambda i: (0, 0)),
                  pl.BlockSpec((d, n), lambda i: (0, 0)),
                  pl.BlockSpec((2, n), lambda i: (0, 0))],
        out_specs=pl.BlockSpec((tm, n), lambda i: (i, 0)),
        out_shape=jax.ShapeDtypeStruct((t, n), F32),
        compiler_params=_cparams(("arbitrary",)),
        name="norm_in_proj_shift",
    )(x, x, x, g, w, mu)


def _rwkv_kernel(zf_ref, zb_ref,
                 w0_ref, a0_ref, w2_ref, a2_ref, g2_ref, kk_ref, ka_ref, rk_ref,
                 yf_ref, yb_ref, bonus_ref, gate_ref,
                 prep_ref, h_ref, *, nc, d_rwkv):
    c = pl.program_id(0)
    C = CHUNK
    n_hp = d_rwkv // LANES

    @pl.when(c == 0)
    def _():
        h_ref[...] = jnp.zeros_like(h_ref)

    zs = (zf_ref, zb_ref)
    ones_bd = _head_ones()
    lane = lax.broadcasted_iota(jnp.int32, (1, LANES), 1)
    m_a = lane < HEAD_DIM
    ti = lax.broadcasted_iota(jnp.int32, (C, C), 0)
    tj = lax.broadcasted_iota(jnp.int32, (C, C), 1)
    bi = lax.broadcasted_iota(jnp.int32, (2 * C, 2 * C), 0)
    bj = lax.broadcasted_iota(jnp.int32, (2 * C, 2 * C), 1)
    same_blk = (bi // C) == (bj // C)
    eye = bi == bj
    cum_mat = ((tj <= ti).astype(BF16), (tj >= ti).astype(BF16))
    strict = (same_blk & ((bj % C) < (bi % C)), same_blk & ((bj % C) > (bi % C)))
    incl = (same_blk & ((bj % C) <= (bi % C)), same_blk & ((bj % C) >= (bi % C)))
    last_row = (C - 1, 0)
    off_wl = 3 * d_rwkv
    off_al = off_wl + LANES
    off_gl = off_al + LANES

    def to_rows(x):
        return jnp.concatenate([x[:, s * LANES:(s + 1) * LANES] for s in range(n_hp)], axis=0)

    def to_lanes(x):
        return jnp.concatenate([x[s * C:(s + 1) * C] for s in range(n_hp)], axis=1)

    def bd(x):
        return jnp.concatenate([jnp.where(m_a, x, 0), jnp.where(m_a, 0, x)], axis=0)

    def unbd(x):
        return x[0:C] + x[C:2 * C]

    w_c = []
    for d in range(2):
        r = zs[d][:, 0:d_rwkv]
        k = zs[d][:, d_rwkv:2 * d_rwkv]
        kkr = k * kk_ref[...]
        sums = [to_rows(kkr * kkr)]
        if d == 0:
            sums.append(to_rows(r * k * rk_ref[...]))
        hs = _head_sum(jnp.concatenate(sums, axis=0), ones_bd)
        kk = kkr * lax.rsqrt(to_lanes(hs[0:n_hp * C]) + 1e-12)
        if d == 0:
            bonus_ref[...] = to_lanes(hs[n_hp * C:2 * n_hp * C]) * zs[0][:, 2 * d_rwkv:3 * d_rwkv]
            gate_ref[...] = _dot(_sigmoid(zs[0][:, off_gl:off_gl + 2 * LANES]), g2_ref[...])
        u = w0_ref[d:d + 1, :] + _dot(jnp.tanh(zs[d][:, off_wl:off_wl + LANES]), w2_ref[d])
        w_log = -(jnp.maximum(-u, 0.0) + jnp.log1p(jnp.exp(-jnp.abs(u)))) - 0.5
        logd = -jnp.exp(w_log)
        a = _sigmoid(a0_ref[d:d + 1, :] + _dot(zs[d][:, off_al:off_al + LANES], a2_ref[d]))
        kd = k * (1.0 + (a - 1.0) * ka_ref[...])
        b = kk * a
        l_hi, l_mid, l_lo = _split3(logd)
        cm = cum_mat[d]
        cum = (jnp.dot(cm, l_hi, preferred_element_type=F32)
               + jnp.dot(cm, l_mid, preferred_element_type=F32)
               + jnp.dot(cm, l_lo, preferred_element_type=F32))
        cum_c = cum[last_row[d]:last_row[d] + 1, :]
        w_inv = jnp.exp(-cum)
        w_c.append(jnp.exp(cum_c))
        w_end = w_c[d] * w_inv
        prep_ref[d, 0] = -kk * jnp.exp(cum - logd)
        prep_ref[d, 1] = r * jnp.exp(cum)
        prep_ref[d, 2] = b * w_inv
        prep_ref[d, 3] = kd * w_inv
        prep_ref[d, 4] = b * w_end
        prep_ref[d, 5] = kd * w_end

    blocks = [(d, hp) for hp in range(n_hp) for d in range(2)]

    def slab(d, i, hp):
        return prep_ref[d, i, :, hp * LANES:(hp + 1) * LANES]

    incl2 = tuple(jnp.concatenate([m, m], axis=1) for m in incl)
    eye_f = eye.astype(F32)
    xs, aks, lows, z0s, vbds = [], [], [], [], []
    for d, hp in blocks:
        at = bd(slab(d, 0, hp).astype(BF16))
        rt = bd(slab(d, 1, hp).astype(BF16))
        bt = slab(d, 2, hp).astype(BF16)
        kt = slab(d, 3, hp).astype(BF16)
        lhs = jnp.concatenate([at, rt], axis=0)
        rhs = jnp.concatenate([bt, bt, kt, kt], axis=0)
        g = lax.dot_general(lhs, rhs, (((1,), (1,)), ((), ())), preferred_element_type=F32)
        xs.append(jnp.where(strict[d], g[0:2 * C, 0:2 * C], 0.0))
        aks.append(jnp.where(strict[d], g[0:2 * C, 2 * C:4 * C], 0.0).astype(BF16))
        lows.append(jnp.where(incl2[d], g[2 * C:4 * C, :], 0.0).astype(BF16))
    for i, (d, hp) in enumerate(blocks):
        vbd = bd(zs[d][:, 2 * d_rwkv + hp * LANES:2 * d_rwkv + (hp + 1) * LANES].astype(BF16))
        vbds.append(vbd)
        akv = jnp.dot(aks[i], vbd, preferred_element_type=F32)
        z0s.append(jnp.concatenate([bd(slab(d, 0, hp).astype(BF16)), akv.astype(BF16)], axis=1))

    n_sq = C.bit_length() - 1
    ts = [eye_f + x for x in xs]
    ps = [x.astype(BF16) for x in xs]
    for j in range(1, n_sq):
        last = j == n_sq - 1
        for i in range(len(blocks)):
            if j == 1:
                ps[i] = jnp.dot(ps[i], ps[i], preferred_element_type=F32).astype(BF16)
            tb = ts[i].astype(BF16)
            if last:
                ts[i] = ts[i] + jnp.dot(ps[i], tb, preferred_element_type=F32)
            else:
                res = jnp.dot(ps[i], jnp.concatenate([ps[i], tb], axis=1), preferred_element_type=F32)
                ts[i] = ts[i] + res[:, LANES:2 * LANES]
                ps[i] = res[:, 0:LANES].astype(BF16)

    wts = []
    for i in range(len(blocks)):
        zz = jnp.dot(ts[i].astype(BF16), z0s[i], preferred_element_type=F32)
        bottom = jnp.concatenate([jnp.zeros((2 * C, LANES), BF16), vbds[i]], axis=1)
        wts.append(jnp.concatenate([zz.astype(BF16), bottom], axis=0))
    outs = []
    for i, (d, hp) in enumerate(blocks):
        bb_t = bd(slab(d, 4, hp)).T.astype(BF16)
        kb_t = bd(slab(d, 5, hp)).T.astype(BF16)
        lhs = jnp.concatenate([lows[i], jnp.concatenate([bb_t, kb_t], axis=1)], axis=0)
        outs.append(jnp.dot(lhs, wts[i], preferred_element_type=F32))
    for i, (d, hp) in enumerate(blocks):
        sl = pl.ds(hp * LANES, LANES)
        o = outs[i]
        r_hat = bd(slab(d, 1, hp)) + o[0:2 * C, 0:LANES]
        p_mat = eye_f * w_c[d][:, hp * LANES:(hp + 1) * LANES] + o[2 * C:4 * C, 0:LANES]
        lhs = jnp.concatenate([r_hat, p_mat], axis=0).astype(BF16)
        res = jnp.dot(lhs, h_ref[d, hp].astype(BF16), preferred_element_type=F32)
        y = unbd(res[0:2 * C] + o[0:2 * C, LANES:2 * LANES])
        h_ref[d, hp] = res[2 * C:4 * C] + o[2 * C:4 * C, LANES:2 * LANES]
        if d == 0:
            yf_ref[:, sl] = y
        else:
            yb_ref[:, sl] = y


def _rwkv_mixer(z, w0, a0, w2p, a2p, g2p, k_k, k_a, r_k, d_rwkv):
    t, nr = z.shape
    C = CHUNK
    nc = t // C
    n_hp = d_rwkv // LANES

    def full(a):
        nd = a.ndim
        return pl.BlockSpec(a.shape, lambda c: (0,) * nd)

    out_spec_f = pl.BlockSpec((C, d_rwkv), lambda c: (c, 0))
    out_spec_b = pl.BlockSpec((C, d_rwkv), lambda c: (nc - 1 - c, 0))
    out_sd = jax.ShapeDtypeStruct((t, d_rwkv), F32)
    params = (w0, a0, w2p, a2p, g2p, k_k, k_a, r_k)
    return pl.pallas_call(
        functools.partial(_rwkv_kernel, nc=nc, d_rwkv=d_rwkv),
        grid=(nc,),
        in_specs=[pl.BlockSpec((C, nr), lambda c: (c, 0)),
                  pl.BlockSpec((C, nr), lambda c: (nc - 1 - c, 0))]
                 + [full(p) for p in params],
        out_specs=[out_spec_f, out_spec_b, out_spec_f, out_spec_f],
        out_shape=[out_sd, out_sd, out_sd, out_sd],
        scratch_shapes=[pltpu.VMEM((2, 6, C, d_rwkv), F32),
                        pltpu.VMEM((2, n_hp, LANES, LANES), F32)],
        compiler_params=_cparams(("arbitrary",)),
        name="rwkv7_chunk_scan",
    )(z, z, *params)


def _rope_table_kernel(pos_ref, inv_ref, c_ref, s1_ref, s2_ref):
    half = ROPE_DIM // 2
    pos = pos_ref[...].astype(F32)
    j = lax.broadcasted_iota(jnp.int32, (1, LANES), 1) % HEAD_DIM
    ang = pos * inv_ref[...]
    cs = jnp.cos(ang)
    sn = jnp.sin(ang)
    c_ref[...] = jnp.where(j < ROPE_DIM, cs, 1.0)
    s1_ref[...] = jnp.where(j < half, -sn, 0.0)
    s2_ref[...] = jnp.where((j >= half) & (j < ROPE_DIM), sn, 0.0)


def _rope_tables(positions, tm):
    t = positions.shape[0]
    half = ROPE_DIM // 2
    inv_freq = jnp.power(ROPE_THETA, -jnp.arange(half, dtype=F32) * 2.0 / ROPE_DIM)
    inv_lane = jnp.tile(inv_freq, LANES // half)[None]
    sd = jax.ShapeDtypeStruct((t, LANES), F32)
    spec = pl.BlockSpec((tm, LANES), lambda i: (i, 0))
    return pl.pallas_call(
        _rope_table_kernel,
        grid=(t // tm,),
        in_specs=[pl.BlockSpec((tm, 1), lambda i: (i, 0)),
                  pl.BlockSpec((1, LANES), lambda i: (0, 0))],
        out_specs=[spec, spec, spec],
        out_shape=[sd, sd, sd],
        compiler_params=_cparams(("arbitrary",)),
        name="rope_tables",
    )(positions, inv_lane)


def _rot(x, c, s1, s2):
    half = ROPE_DIM // 2
    return x * c + pltpu.roll(x, LANES - half, 1) * s1 + pltpu.roll(x, half, 1) * s2


def _norm_matmul_rope_kernel(x_ref, g_ref, w_ref, c_ref, s1_ref, s2_ref, o_ref, *, n_rot):
    x = x_ref[...]
    ms = jnp.mean(x * x, axis=-1, keepdims=True)
    h = x * lax.rsqrt(ms + RMS_EPS) * g_ref[...]
    z = jnp.dot(h.astype(BF16), w_ref[...], preferred_element_type=F32)
    c = c_ref[...]
    s1 = s1_ref[...]
    s2 = s2_ref[...]
    for s in range(z.shape[1] // LANES):
        sl = slice(s * LANES, (s + 1) * LANES)
        zs = z[:, sl]
        if s < n_rot:
            zs = _rot(zs, c, s1, s2)
            if s < n_rot // 2:
                zs = zs * (HEAD_DIM ** -0.5)
        o_ref[:, sl] = zs


def _norm_matmul_rope(x, g, w, tabs, tm):
    t, d = x.shape
    n = w.shape[1]
    tab = pl.BlockSpec((tm, LANES), lambda i: (i, 0))
    return pl.pallas_call(
        functools.partial(_norm_matmul_rope_kernel, n_rot=2 * (n // 3) // LANES),
        grid=(t // tm,),
        in_specs=[pl.BlockSpec((tm, d), lambda i: (i, 0)),
                  pl.BlockSpec((1, d), lambda i: (0, 0)),
                  pl.BlockSpec((d, n), lambda i: (0, 0)), tab, tab, tab],
        out_specs=pl.BlockSpec((tm, n), lambda i: (i, 0)),
        out_shape=jax.ShapeDtypeStruct((t, n), F32),
        compiler_params=_cparams(("arbitrary",)),
        name="norm_in_proj_rope",
    )(x, g, w, *tabs)


def _attn_kernel(q_ref, k_ref, v_ref, o_ref, op_ref, lp_ref, *, seq_t, sb):
    jsb = pl.program_id(1)
    lane = lax.broadcasted_iota(jnp.int32, (1, LANES), 1)
    m_a = lane < HEAD_DIM
    for p, (window, dil) in enumerate(DILATED_PATTERNS):
        radius = window // (2 * dil)
        win = Q_BLOCK + 2 * radius
        seq = seq_t // dil
        nq = sb // (dil * Q_BLOCK)
        qi = lax.broadcasted_iota(jnp.int32, (Q_BLOCK, win), 0)
        kj = lax.broadcasted_iota(jnp.int32, (Q_BLOCK, win), 1)

        def rows(start, size, dil=dil):
            return pl.ds(start, size) if dil == 1 else pl.ds(start, size, stride=dil)

        def body(idx, carry, dil=dil, radius=radius, win=win, seq=seq, nq=nq, qi=qi, kj=kj,
                 rows=rows, p=p):
            r = idx // nq
            jj = idx % nq
            q_l0 = (jsb * nq + jj) * Q_BLOCK
            start_l = jnp.clip(q_l0 - radius, 0, seq - win)
            q_rows = rows(r + dil * Q_BLOCK * jj, Q_BLOCK)
            k_rows = rows(r + dil * start_l, win)
            q = q_ref[q_rows, :].astype(BF16)
            kw = k_ref[k_rows, :].astype(BF16)
            vw = v_ref[k_rows, :].astype(BF16)
            valid = jnp.abs((q_l0 + qi) - (start_l + kj)) <= radius
            one = jnp.ones((), BF16)
            halves = ((jnp.where(m_a, q, 0), jnp.where(m_a, vw, one)),
                      (jnp.where(m_a, 0, q), jnp.where(m_a, one, vw)))
            res, mx = [], []
            for qh, vh in halves:
                sc = lax.dot_general(qh, kw, (((1,), (1,)), ((), ())), preferred_element_type=F32)
                sc = jnp.where(valid, sc, NEG_BIG)
                m = jnp.max(sc, axis=-1, keepdims=True)
                pexp = jnp.exp(sc - m).astype(BF16)
                res.append(jnp.dot(pexp, vh, preferred_element_type=F32))
                mx.append(m)
            num = jnp.where(m_a, res[0], res[1])
            den = pltpu.roll(jnp.where(m_a, res[1], res[0]), HEAD_DIM, 1)
            op_ref[p, q_rows, :] = num / den
            lp_ref[p, q_rows, :] = jnp.where(m_a, mx[0], mx[1]) + jnp.log(den)
            return carry

        lax.fori_loop(0, dil * nq, body, 0, unroll=8)

    l1 = lp_ref[0]
    l2 = lp_ref[1]
    l3 = lp_ref[2]
    m = jnp.maximum(jnp.maximum(l1, l2), l3)
    e1 = jnp.exp(l1 - m)
    e2 = jnp.exp(l2 - m)
    e3 = jnp.exp(l3 - m)
    o_ref[...] = (e1 * op_ref[0] + e2 * op_ref[1] + e3 * op_ref[2]) / (e1 + e2 + e3)


def _dilated_attention(z_attn, d_attn, sb):
    t = z_attn.shape[0]
    n_hp = d_attn // LANES
    n_pat = len(DILATED_PATTERNS)
    return pl.pallas_call(
        functools.partial(_attn_kernel, seq_t=t, sb=sb),
        grid=(n_hp, t // sb),
        in_specs=[pl.BlockSpec((sb, LANES), lambda h, j: (j, h)),
                  pl.BlockSpec((t, LANES), lambda h, j: (0, n_hp + h)),
                  pl.BlockSpec((t, LANES), lambda h, j: (0, 2 * n_hp + h))],
        out_specs=pl.BlockSpec((sb, LANES), lambda h, j: (j, h)),
        out_shape=jax.ShapeDtypeStruct((t, d_attn), F32),
        scratch_shapes=[pltpu.VMEM((n_pat, sb, LANES), F32), pltpu.VMEM((n_pat, sb, LANES), F32)],
        compiler_params=_cparams(("arbitrary", "arbitrary")),
        name="dilated_attention",
    )(z_attn, z_attn, z_attn)


def _out_proj_kernel(x_ref, yf_ref, yb_ref, bonus_ref, gate_ref, lnw_ref, lnb_ref,
                     ya_ref, w_ref, out_ref, yr_ref, *, d_rwkv):
    ones_bd = _head_ones()
    for hp in range(d_rwkv // LANES):
        sl = pl.ds(hp * LANES, LANES)
        y = yf_ref[:, sl] + yb_ref[:, sl]
        mean = _head_sum(y, ones_bd) * (1.0 / HEAD_DIM)
        yc = y - mean
        var = _head_sum(yc * yc, ones_bd) * (1.0 / HEAD_DIM)
        yn = yc * lax.rsqrt(var + GN_EPS) * lnw_ref[:, sl] + lnb_ref[:, sl]
        yr_ref[:, sl] = ((yn + bonus_ref[:, sl]) * gate_ref[:, sl]).astype(BF16)
    acc = jnp.dot(yr_ref[...], w_ref[0:d_rwkv, :], preferred_element_type=F32)
    acc += jnp.dot(ya_ref[...].astype(BF16), w_ref[d_rwkv:, :], preferred_element_type=F32)
    out_ref[...] = x_ref[...] + acc


def _out_proj(x, yf, yb, bonus, gate, ln_w, ln_b, y_attn, w_out, tm):
    t, d = x.shape
    d_rwkv = yf.shape[1]
    d_attn = y_attn.shape[1]
    row = lambda n: pl.BlockSpec((tm, n), lambda i: (i, 0))
    const = lambda a: pl.BlockSpec(a.shape, lambda i: (0, 0))
    return pl.pallas_call(
        functools.partial(_out_proj_kernel, d_rwkv=d_rwkv),
        grid=(t // tm,),
        in_specs=[row(d)] + [row(d_rwkv)] * 4 + [const(ln_w), const(ln_b)]
                 + [row(d_attn), const(w_out)],
        out_specs=row(d),
        out_shape=jax.ShapeDtypeStruct((t, d), F32),
        scratch_shapes=[pltpu.VMEM((tm, d_rwkv), BF16)],
        compiler_params=_cparams(("arbitrary",)),
        name="merge_out_proj",
    )(x, yf, yb, bonus, gate, ln_w, ln_b, y_attn, w_out)


def _router_kernel(x_ref, g_ref, w_ref, b_ref, h_ref, idx_ref, wt_ref):
    x = x_ref[...]
    ms = jnp.mean(x * x, axis=-1, keepdims=True)
    h = x * lax.rsqrt(ms + RMS_EPS) * g_ref[...]
    h_ref[...] = h
    h_hi, h_lo = _split2(h)
    w_hi = w_ref[0]
    w_lo = w_ref[1]
    logits = (jnp.dot(h_hi, w_hi, preferred_element_type=F32)
              + jnp.dot(h_hi, w_lo, preferred_element_type=F32)
              + jnp.dot(h_lo, w_hi, preferred_element_type=F32)) + b_ref[...]
    lane = lax.broadcasted_iota(jnp.int32, logits.shape, 1).astype(F32)
    big = jnp.float32(LANES)
    is_g = lane < N_GROUPS
    gl = jnp.where(is_g, logits, NEG_BIG)
    gmax = jnp.max(gl, axis=-1, keepdims=True)
    gsel = jnp.min(jnp.where(is_g & (gl == gmax), lane, big), axis=-1, keepdims=True)
    g1 = 1.0 / jnp.sum(jnp.where(is_g, jnp.exp(gl - gmax), 0.0), axis=-1, keepdims=True)
    lo = N_GROUPS + gsel * EXPERTS_PER_GROUP
    in_grp = (lane >= lo) & (lane < lo + EXPERTS_PER_GROUP)
    el = jnp.where(in_grp, logits, NEG_BIG)
    v1 = jnp.max(el, axis=-1, keepdims=True)
    i1 = jnp.min(jnp.where(in_grp & (el == v1), lane, big), axis=-1, keepdims=True)
    rest = in_grp & (lane != i1)
    el2 = jnp.where(rest, logits, NEG_BIG)
    v2 = jnp.max(el2, axis=-1, keepdims=True)
    i2 = jnp.min(jnp.where(rest & (el2 == v2), lane, big), axis=-1, keepdims=True)
    e2 = jnp.exp(v2 - v1)
    ww1 = 1.0 / (1.0 + e2)
    ww2 = e2 / (1.0 + e2)
    idx = jnp.where(lane == 0, i1 - N_GROUPS, jnp.where(lane == 1, i2 - N_GROUPS, 0.0))
    idx_ref[...] = idx.astype(jnp.int32)
    wt_ref[...] = jnp.where(lane == 0, g1 * ww1, jnp.where(lane == 1, g1 * ww2, 0.0))


def _router(x, g, w_split, b_pad, tm):
    t, d = x.shape
    return pl.pallas_call(
        _router_kernel,
        grid=(t // tm,),
        in_specs=[pl.BlockSpec((tm, d), lambda i: (i, 0)),
                  pl.BlockSpec((1, d), lambda i: (0, 0)),
                  pl.BlockSpec((2, d, LANES), lambda i: (0, 0, 0)),
                  pl.BlockSpec((1, LANES), lambda i: (0, 0))],
        out_specs=[pl.BlockSpec((tm, d), lambda i: (i, 0)),
                   pl.BlockSpec((tm, LANES), lambda i: (i, 0)),
                   pl.BlockSpec((tm, LANES), lambda i: (i, 0))],
        out_shape=[jax.ShapeDtypeStruct((t, d), F32),
                   jax.ShapeDtypeStruct((t, LANES), jnp.int32),
                   jax.ShapeDtypeStruct((t, LANES), F32)],
        compiler_params=_cparams(("arbitrary",)),
        name="router",
    )(x, g, w_split, b_pad)


def _moe_kernel(te_ref, nt_ref, nv_ref, rows_ref, h_hbm, wg_ref, wu_ref, wd_ref, o_ref,
                xg_ref, xb_ref, acc_ref, sem, *, tm, nf):
    i = pl.program_id(0)
    f = pl.program_id(1)
    nt = nt_ref[0]
    active = i < nt
    slot = i % 2
    n_slab = xg_ref.shape[2] // LANES

    def issue_tile(tile, buf):
        def body(g, carry):
            for u in range(GATHER_UNROLL):
                j = g * GATHER_UNROLL + u
                src = h_hbm.at[pl.ds(rows_ref[tile * tm + j], 1), :]
                pltpu.make_async_copy(src, xg_ref.at[buf, pl.ds(j, 1), :],
                                      sem.at[buf]).start(priority=1)
            return carry
        lax.fori_loop(0, nv_ref[tile] // GATHER_UNROLL, body, 0)

    def wait_tile(tile, buf):
        def body(g, carry):
            pltpu.make_async_copy(h_hbm.at[pl.ds(0, GATHER_CHUNK), :],
                                  xg_ref.at[buf, pl.ds(0, GATHER_CHUNK), :], sem.at[buf]).wait()
            return carry
        lax.fori_loop(0, nv_ref[tile] // GATHER_CHUNK, body, 0)

    @pl.when((i == 0) & (f == 0))
    def _():
        xg_ref[...] = jnp.zeros_like(xg_ref)
        issue_tile(0, 0)

    @pl.when(active & (f == 0))
    def _():
        wait_tile(i, slot)
        xb_ref[...] = xg_ref[slot].astype(BF16)
        issue_tile(i + 1, 1 - slot)

    @pl.when(active)
    def _():
        xb = xb_ref[...]
        gate = jnp.dot(xb, wg_ref[...].astype(BF16), preferred_element_type=F32)
        up = jnp.dot(xb, wu_ref[...].astype(BF16), preferred_element_type=F32)
        hid = (gate * _sigmoid(gate)) * up
        part = jnp.dot(hid.astype(BF16), wd_ref[...].astype(BF16), preferred_element_type=F32)

        @pl.when(f == 0)
        def _():
            acc_ref[...] = part

        @pl.when((f != 0) & (f != nf - 1))
        def _():
            acc_ref[...] += part

        @pl.when(f == nf - 1)
        def _():
            for c in range(n_slab):
                sl = slice(c * LANES, (c + 1) * LANES)
                o_ref[pl.ds(c, tm, stride=n_slab), :] = acc_ref[:, sl] + part[:, sl]

    @pl.when(jnp.logical_not(active) & (f == 0))
    def _():
        o_ref[...] = jnp.zeros_like(o_ref)


def _moe(h, tile_expert, n_tiles, n_valid, rows, w_gate, w_up, w_down, tm, tf):
    t, d = h.shape
    n_e, _, d_e = w_gate.shape
    n_slab = d // LANES
    nt_max = tile_expert.shape[0]
    nf = d_e // tf
    assert nf >= 2 and tm % GATHER_CHUNK == 0 and GATHER_CHUNK % GATHER_UNROLL == 0
    assert rows.shape[0] == (nt_max + 1) * tm and n_valid.shape[0] == nt_max + 1

    def fidx(i, f, nt):
        return jnp.where(i < nt[0], f, nf - 1)

    grid_spec = pltpu.PrefetchScalarGridSpec(
        num_scalar_prefetch=4,
        grid=(nt_max, d_e // tf),
        in_specs=[pl.BlockSpec(memory_space=pl.ANY),
                  pl.BlockSpec((None, d, tf), lambda i, f, te, nt, nv, rw: (te[i], 0, fidx(i, f, nt))),
                  pl.BlockSpec((None, d, tf), lambda i, f, te, nt, nv, rw: (te[i], 0, fidx(i, f, nt))),
                  pl.BlockSpec((None, tf, d), lambda i, f, te, nt, nv, rw: (te[i], fidx(i, f, nt), 0))],
        out_specs=pl.BlockSpec((tm * n_slab, LANES), lambda i, f, te, nt, nv, rw: (i, 0)),
        scratch_shapes=[pltpu.VMEM((2, tm, d), F32), pltpu.VMEM((tm, d), BF16),
                        pltpu.VMEM((tm, d), F32), pltpu.SemaphoreType.DMA((2,))],
    )
    return pl.pallas_call(
        functools.partial(_moe_kernel, tm=tm, nf=nf),
        grid_spec=grid_spec,
        out_shape=jax.ShapeDtypeStruct((nt_max * tm * n_slab, LANES), F32),
        compiler_params=_cparams(("arbitrary", "arbitrary")),
        name="grouped_expert_mlp",
    )(tile_expert, n_tiles, n_valid, rows, h, w_gate, w_up, w_down)


def _final_kernel(s0_ref, s1_ref, x_ref, wt_ref, g_ref, y_hbm, o_ref, yg_ref, sem, *, tm, nt):
    i = pl.program_id(0)
    slot = i % 2
    d = x_ref.shape[1]
    n_slab = d // LANES

    def issue(tile, buf):
        def body(j, carry):
            for k, s_ref in enumerate((s0_ref, s1_ref)):
                src = y_hbm.at[pl.ds(pl.multiple_of(s_ref[tile * tm + j] * n_slab, n_slab), n_slab), :]
                dst = yg_ref.at[buf, k, pl.ds(pl.multiple_of(j * n_slab, n_slab), n_slab), :]
                pltpu.make_async_copy(src, dst, sem.at[buf]).start(priority=k)
            return carry
        lax.fori_loop(0, tm, body, 0, unroll=8)

    @pl.when(i == 0)
    def _():
        issue(0, 0)

    @pl.when(i + 1 < nt)
    def _():
        issue(i + 1, 1 - slot)

    for k in range(2):
        pltpu.make_async_copy(y_hbm.at[pl.ds(0, tm * n_slab), :], yg_ref.at[slot, k],
                              sem.at[slot]).wait()

    wt = wt_ref[...]
    w0 = wt[:, 0:1]
    w1 = wt[:, 1:2]
    ss = jnp.zeros((tm, 1), F32)
    for c in range(n_slab):
        sl = slice(c * LANES, (c + 1) * LANES)
        rows = pl.ds(c, tm, stride=n_slab)
        xs = x_ref[:, sl] + w0 * yg_ref[slot, 0, rows, :] + w1 * yg_ref[slot, 1, rows, :]
        o_ref[:, sl] = xs
        ss = ss + jnp.sum(xs * xs, axis=-1, keepdims=True)
    o_ref[...] = o_ref[...] * lax.rsqrt(ss * (1.0 / d) + RMS_EPS) * g_ref[...]


def _final(slot0, slot1, x, wts, g, y_sorted, tm):
    t, d = x.shape
    grid_spec = pltpu.PrefetchScalarGridSpec(
        num_scalar_prefetch=2,
        grid=(t // tm,),
        in_specs=[pl.BlockSpec((tm, d), lambda i, s0, s1: (i, 0)),
                  pl.BlockSpec((tm, LANES), lambda i, s0, s1: (i, 0)),
                  pl.BlockSpec((1, d), lambda i, s0, s1: (0, 0)),
                  pl.BlockSpec(memory_space=pl.ANY)],
        out_specs=pl.BlockSpec((tm, d), lambda i, s0, s1: (i, 0)),
        scratch_shapes=[pltpu.VMEM((2, 2, tm * (d // LANES), LANES), F32),
                        pltpu.SemaphoreType.DMA((2,))],
    )
    return pl.pallas_call(
        functools.partial(_final_kernel, tm=tm, nt=t // tm),
        grid_spec=grid_spec,
        out_shape=jax.ShapeDtypeStruct((t, d), F32),
        compiler_params=_cparams(("arbitrary",)),
        name="combine_final_norm",
    )(slot0, slot1, x, wts, g, y_sorted)


def _dispatch_plan(experts, tm):
    t = experts.shape[0]
    ef = experts.T.reshape(-1)
    onehot = (ef[:, None] == jnp.arange(N_EXPERTS, dtype=jnp.int32)[None, :]).astype(jnp.int32)
    csum = jnp.cumsum(onehot, axis=0)
    rank = jnp.take_along_axis(csum, ef[:, None], axis=1)[:, 0] - 1
    counts = csum[-1]
    tiles_e = (counts + tm - 1) // tm
    tile_end = jnp.cumsum(tiles_e)
    tile_start = tile_end - tiles_e
    n_tiles = tile_end[-1]
    nt_max = (2 * t) // tm + N_EXPERTS
    slot = tile_start[ef] * tm + rank
    tid = jnp.minimum(jnp.arange(nt_max, dtype=jnp.int32), n_tiles - 1)
    tile_expert = jnp.sum((tile_end[None, :] <= tid[:, None]).astype(jnp.int32), axis=1)
    tok = jnp.tile(jnp.arange(t, dtype=jnp.int32), 2)
    rows = jnp.zeros(((nt_max + 1) * tm,), jnp.int32).at[slot].set(tok)
    tix = jnp.arange(nt_max + 1, dtype=jnp.int32)
    te_all = jnp.concatenate([tile_expert, tile_expert[-1:]])
    left = counts[te_all] - (tix - tile_start[te_all]) * tm
    n_valid = jnp.where(tix < n_tiles, jnp.clip(left, 0, tm), 0)
    n_valid = (n_valid + GATHER_CHUNK - 1) // GATHER_CHUNK * GATHER_CHUNK
    return (slot[:t], slot[t:], rows, tile_expert, n_tiles.reshape(1).astype(jnp.int32),
            n_valid.astype(jnp.int32))


def kernel(x, positions, norm_mix, w_in, mu_shift, w0, w2, a0, a2, g2, k_k, k_a, r_k, ln_x_w, ln_x_b, w_out, norm_ffn, router_group_w, router_group_b, router_expert_w, router_expert_b, w_gate, w_up, w_down, norm_final):
    bsz, seq, d = x.shape
    assert bsz == 1
    depth = w_in.shape[0]
    d_rwkv = k_k.shape[1]
    d_attn = w_out.shape[1] - d_rwkv
    rwkv_cols = mu_shift.shape[2]
    nr = -(-rwkv_cols // (2 * LANES)) * (2 * LANES)
    assert 3 * d_rwkv + 2 * LANES + GATE_LORA == rwkv_cols and nr == 3 * d_rwkv + 4 * LANES
    tm_moe = 512
    tf_moe = 512
    sb_attn = max(dil for _, dil in DILATED_PATTERNS) * Q_BLOCK
    assert seq % sb_attn == 0

    xt = x.reshape(seq, d)
    tabs = _rope_tables(positions.reshape(seq, 1), 512)
    for l in range(depth):
        w_r, w_a = _w_in_split(w_in, l, rwkv_cols, nr, 256)
        mu = jnp.pad(mu_shift[l], ((0, 0), (0, nr - rwkv_cols)))
        zl = jnp.zeros((DECAY_LORA, d_rwkv), F32)
        w2p = jnp.stack([jnp.concatenate([w2[l, 0], zl]), jnp.concatenate([zl, w2[l, 1]])]).astype(BF16)
        a2p = jnp.stack([jnp.concatenate([a2[l, 0], zl]), jnp.concatenate([zl, a2[l, 1]])]).astype(BF16)
        g2p = jnp.pad(g2[l], ((0, 2 * LANES - GATE_LORA), (0, 0))).astype(BF16)
        rk = r_k[l].reshape(1, d_rwkv)

        z_r = _norm_matmul_shift(xt, norm_mix[l][None], w_r, mu, 256)
        z_a = _norm_matmul_rope(xt, norm_mix[l][None], w_a, tabs, 256)
        yf, yb, bonus, gate = _rwkv_mixer(z_r, w0[l], a0[l], w2p, a2p, g2p,
                                          k_k[l][None], k_a[l][None], rk, d_rwkv)
        y_attn = _dilated_attention(z_a, d_attn, sb_attn)
        x2 = _out_proj(xt, yf, yb, bonus, gate, ln_x_w[l][None], ln_x_b[l][None],
                       y_attn, w_out[l].astype(BF16), 256)

        w_rt = jnp.concatenate(
            [router_group_w[l], router_expert_w[l].transpose(1, 0, 2).reshape(d, N_EXPERTS)], axis=1)
        w_rt = jnp.pad(w_rt, ((0, 0), (0, LANES - w_rt.shape[1])))
        rt_hi = w_rt.astype(BF16)
        rt_lo = (w_rt - rt_hi.astype(F32)).astype(BF16)
        b_rt = jnp.concatenate([router_group_b[l], router_expert_b[l].reshape(-1)])
        b_rt = jnp.pad(b_rt, (0, LANES - b_rt.shape[0]))[None]
        h, idx, wts = _router(x2, norm_ffn[l][None], jnp.stack([rt_hi, rt_lo]), b_rt, 256)

        slot0, slot1, rows, tile_expert, n_tiles, n_valid = _dispatch_plan(idx[:, :2], tm_moe)
        y_sorted = _moe(h, tile_expert, n_tiles, n_valid, rows, w_gate[l], w_up[l], w_down[l],
                        tm_moe, tf_moe)
        is_last = l == depth - 1
        assert is_last, "the combine kernel applies the final norm; depth must be 1"
        xt = _final(slot0, slot1, x2, wts, norm_final[None], y_sorted, 256)
    return xt.reshape(bsz, seq, d)
```

```python
import functools

import jax
import jax.numpy as jnp
from jax import lax
from jax.experimental import pallas as pl
from jax.experimental.pallas import tpu as pltpu

HEAD_DIM = 64
LANES = 128
SUBLANES = 8
DECAY_LORA = 64
ICLR_LORA = 64
GATE_LORA = 160
GN_EPS = 64e-5
RMS_EPS = 1e-6
ROPE_THETA = 500000.0
ROPE_DIM = HEAD_DIM // 4
DILATED_PATTERNS = ((128, 1), (512, 4), (2048, 16))
Q_BLOCK = 128
N_GROUPS = 4
EXPERTS_PER_GROUP = 8
N_EXPERTS = N_GROUPS * EXPERTS_PER_GROUP
NEG_BIG = -1e30
CHUNK = 64
GATHER_CHUNK = 64
GATHER_UNROLL = 8
VMEM_LIMIT = 56 * 1024 * 1024

BF16 = jnp.bfloat16
F32 = jnp.float32


def _cparams(sem):
    return pltpu.CompilerParams(dimension_semantics=sem, vmem_limit_bytes=VMEM_LIMIT)


def _dot(a, b):
    return jnp.dot(a.astype(BF16), b.astype(BF16), preferred_element_type=F32)


def _dot_nt(a, b):
    return lax.dot_general(a.astype(BF16), b.astype(BF16), (((1,), (1,)), ((), ())),
                           preferred_element_type=F32)


def _dot_tn(a, b):
    return lax.dot_general(a.astype(BF16), b.astype(BF16), (((0,), (0,)), ((), ())),
                           preferred_element_type=F32)


def _split2(x):
    hi = x.astype(BF16)
    lo = (x - hi.astype(F32)).astype(BF16)
    return hi, lo


def _split3(x):
    hi = x.astype(BF16)
    r1 = x - hi.astype(F32)
    mid = r1.astype(BF16)
    lo = (r1 - mid.astype(F32)).astype(BF16)
    return hi, mid, lo


def _head_ones():
    i = lax.broadcasted_iota(jnp.int32, (LANES, LANES), 0) // HEAD_DIM
    j = lax.broadcasted_iota(jnp.int32, (LANES, LANES), 1) // HEAD_DIM
    return (i == j).astype(BF16)


def _head_sum(x, ones_bd):
    hi, lo = _split2(x)
    return (jnp.dot(hi, ones_bd, preferred_element_type=F32)
            + jnp.dot(lo, ones_bd, preferred_element_type=F32))


def _sigmoid(x):
    return 1.0 / (1.0 + jnp.exp(-x))


def _w_in_split_kernel(w_ref, wr_ref, wa_ref, *, rwkv_cols):
    w = w_ref[...]
    nr = wr_ref.shape[1]
    wr_ref[:, 0:rwkv_cols] = w[:, 0:rwkv_cols].astype(BF16)
    wr_ref[:, rwkv_cols:nr] = jnp.zeros((w.shape[0], nr - rwkv_cols), BF16)
    wa_ref[...] = w[:, rwkv_cols:].astype(BF16)


def _w_in_split(w_all, layer, rwkv_cols, nr, tk):
    _, d, n = w_all.shape
    return pl.pallas_call(
        functools.partial(_w_in_split_kernel, rwkv_cols=rwkv_cols),
        grid=(d // tk,),
        in_specs=[pl.BlockSpec((None, tk, n), lambda i: (layer, i, 0))],
        out_specs=[pl.BlockSpec((tk, nr), lambda i: (i, 0)),
                   pl.BlockSpec((tk, n - rwkv_cols), lambda i: (i, 0))],
        out_shape=[jax.ShapeDtypeStruct((d, nr), BF16),
                   jax.ShapeDtypeStruct((d, n - rwkv_cols), BF16)],
        compiler_params=_cparams(("arbitrary",)),
        name="w_in_split",
    )(w_all)


def _norm_matmul_shift_kernel(x_ref, xp_ref, xn_ref, g_ref, w_ref, mu_ref, o_ref, *, nt):
    i = pl.program_id(0)
    tm = x_ref.shape[0]
    halo = xp_ref.shape[0]
    x = jnp.concatenate([xp_ref[...], x_ref[...], xn_ref[...]], axis=0)
    ms = jnp.mean(x * x, axis=-1, keepdims=True)
    h = x * lax.rsqrt(ms + RMS_EPS) * g_ref[...]
    z = jnp.dot(h.astype(BF16), w_ref[...], preferred_element_type=F32)
    rows = tm + 2 * halo
    zc = z[halo:halo + tm]
    z_prev = pltpu.roll(z, 1, 0)[halo:halo + tm]
    z_next = pltpu.roll(z, rows - 1, 0)[halo:halo + tm]
    row = lax.broadcasted_iota(jnp.int32, (tm, 1), 0)
    z_prev = jnp.where((i == 0) & (row == 0), 0.0, z_prev)
    z_next = jnp.where((i == nt - 1) & (row == tm - 1), 0.0, z_next)
    o_ref[...] = zc + mu_ref[0:1, :] * (z_prev - zc) + mu_ref[1:2, :] * (z_next - zc)


def _norm_matmul_shift(x, g, w, mu, tm):
    t, d = x.shape
    n = w.shape[1]
    nt = t // tm
    hb = tm // SUBLANES
    last = t // SUBLANES - 1
    return pl.pallas_call(
        functools.partial(_norm_matmul_shift_kernel, nt=nt),
        grid=(nt,),
        in_specs=[pl.BlockSpec((tm, d), lambda i: (i, 0)),
                  pl.BlockSpec((SUBLANES, d), lambda i: (jnp.maximum(i * hb - 1, 0), 0)),
                  pl.BlockSpec((SUBLANES, d), lambda i: (jnp.minimum((i + 1) * hb, last), 0)),
                  pl.BlockSpec((1, d), lambda i: (0, 0)),
                  pl.BlockSpec((d, n), lambda i: (0, 0)),
                  pl.BlockSpec((2, n), lambda i: (0, 0))],
        out_specs=pl.BlockSpec((tm, n), lambda i: (i, 0)),
        out_shape=jax.ShapeDtypeStruct((t, n), F32),
        compiler_params=_cparams(("arbitrary",)),
        name="norm_in_proj_shift",
    )(x, x, x, g, w, mu)


def _rwkv_kernel(zf_ref, zb_ref,
                 w0_ref, a0_ref, w2_ref, a2_ref, g2_ref, kk_ref, ka_ref, rk_ref,
                 yf_ref, yb_ref, bonus_ref, gate_ref,
                 prep_ref, h_ref, *, nc, d_rwkv):
    c = pl.program_id(0)
    C = CHUNK
    n_hp = d_rwkv // LANES

    @pl.when(c == 0)
    def _():
        h_ref[...] = jnp.zeros_like(h_ref)

    zs = (zf_ref, zb_ref)
    ones_bd = _head_ones()
    lane = lax.broadcasted_iota(jnp.int32, (1, LANES), 1)
    m_a = lane < HEAD_DIM
    ti = lax.broadcasted_iota(jnp.int32, (C, C), 0)
    tj = lax.broadcasted_iota(jnp.int32, (C, C), 1)
    bi = lax.broadcasted_iota(jnp.int32, (2 * C, 2 * C), 0)
    bj = lax.broadcasted_iota(jnp.int32, (2 * C, 2 * C), 1)
    same_blk = (bi // C) == (bj // C)
    eye = bi == bj
    cum_mat = ((tj <= ti).astype(BF16), (tj >= ti).astype(BF16))
    strict = (same_blk & ((bj % C) < (bi % C)), same_blk & ((bj % C) > (bi % C)))
    incl = (same_blk & ((bj % C) <= (bi % C)), same_blk & ((bj % C) >= (bi % C)))
    last_row = (C - 1, 0)
    off_wl = 3 * d_rwkv
    off_al = off_wl + LANES
    off_gl = off_al + LANES

    def to_rows(x):
        return jnp.concatenate([x[:, s * LANES:(s + 1) * LANES] for s in range(n_hp)], axis=0)

    def to_lanes(x):
        return jnp.concatenate([x[s * C:(s + 1) * C] for s in range(n_hp)], axis=1)

    def bd(x):
        return jnp.concatenate([jnp.where(m_a, x, 0), jnp.where(m_a, 0, x)], axis=0)

    def unbd(x):
        return x[0:C] + x[C:2 * C]

    w_c = []
    for d in range(2):
        r = zs[d][:, 0:d_rwkv]
        k = zs[d][:, d_rwkv:2 * d_rwkv]
        kkr = k * kk_ref[...]
        sums = [to_rows(kkr * kkr)]
        if d == 0:
            sums.append(to_rows(r * k * rk_ref[...]))
        hs = _head_sum(jnp.concatenate(sums, axis=0), ones_bd)
        kk = kkr * lax.rsqrt(to_lanes(hs[0:n_hp * C]) + 1e-12)
        if d == 0:
            bonus_ref[...] = to_lanes(hs[n_hp * C:2 * n_hp * C]) * zs[0][:, 2 * d_rwkv:3 * d_rwkv]
            gate_ref[...] = _dot(_sigmoid(zs[0][:, off_gl:off_gl + 2 * LANES]), g2_ref[...])
        u = w0_ref[d:d + 1, :] + _dot(jnp.tanh(zs[d][:, off_wl:off_wl + LANES]), w2_ref[d])
        w_log = -(jnp.maximum(-u, 0.0) + jnp.log1p(jnp.exp(-jnp.abs(u)))) - 0.5
        logd = -jnp.exp(w_log)
        a = _sigmoid(a0_ref[d:d + 1, :] + _dot(zs[d][:, off_al:off_al + LANES], a2_ref[d]))
        kd = k * (1.0 + (a - 1.0) * ka_ref[...])
        b = kk * a
        l_hi, l_mid, l_lo = _split3(logd)
        cm = cum_mat[d]
        cum = (jnp.dot(cm, l_hi, preferred_element_type=F32)
               + jnp.dot(cm, l_mid, preferred_element_type=F32)
               + jnp.dot(cm, l_lo, preferred_element_type=F32))
        cum_c = cum[last_row[d]:last_row[d] + 1, :]
        w_inv = jnp.exp(-cum)
        w_c.append(jnp.exp(cum_c))
        w_end = w_c[d] * w_inv
        prep_ref[d, 0] = -kk * jnp.exp(cum - logd)
        prep_ref[d, 1] = r * jnp.exp(cum)
        prep_ref[d, 2] = b * w_inv
        prep_ref[d, 3] = kd * w_inv
        prep_ref[d, 4] = b * w_end
        prep_ref[d, 5] = kd * w_end

    blocks = [(d, hp) for hp in range(n_hp) for d in range(2)]

    def slab(d, i, hp):
        return prep_ref[d, i, :, hp * LANES:(hp + 1) * LANES]

    incl2 = tuple(jnp.concatenate([m, m], axis=1) for m in incl)
    eye_f = eye.astype(F32)
    xs, aks, lows, z0s, vbds = [], [], [], [], []
    for d, hp in blocks:
        at = bd(slab(d, 0, hp).astype(BF16))
        rt = bd(slab(d, 1, hp).astype(BF16))
        bt = slab(d, 2, hp).astype(BF16)
        kt = slab(d, 3, hp).astype(BF16)
        lhs = jnp.concatenate([at, rt], axis=0)
        rhs = jnp.concatenate([bt, bt, kt, kt], axis=0)
        g = lax.dot_general(lhs, rhs, (((1,), (1,)), ((), ())), preferred_element_type=F32)
        xs.append(jnp.where(strict[d], g[0:2 * C, 0:2 * C], 0.0))
        aks.append(jnp.where(strict[d], g[0:2 * C, 2 * C:4 * C], 0.0).astype(BF16))
        lows.append(jnp.where(incl2[d], g[2 * C:4 * C, :], 0.0).astype(BF16))
    for i, (d, hp) in enumerate(blocks):
        vbd = bd(zs[d][:, 2 * d_rwkv + hp * LANES:2 * d_rwkv + (hp + 1) * LANES].astype(BF16))
        vbds.append(vbd)
        akv = jnp.dot(aks[i], vbd, preferred_element_type=F32)
        z0s.append(jnp.concatenate([bd(slab(d, 0, hp).astype(BF16)), akv.astype(BF16)], axis=1))

    n_sq = C.bit_length() - 1
    ts = [eye_f + x for x in xs]
    ps = [x.astype(BF16) for x in xs]
    for j in range(1, n_sq):
        last = j == n_sq - 1
        for i in range(len(blocks)):
            if j == 1:
                ps[i] = jnp.dot(ps[i], ps[i], preferred_element_type=F32).astype(BF16)
            tb = ts[i].astype(BF16)
            if last:
                ts[i] = ts[i] + jnp.dot(ps[i], tb, preferred_element_type=F32)
            else:
                res = jnp.dot(ps[i], jnp.concatenate([ps[i], tb], axis=1), preferred_element_type=F32)
                ts[i] = ts[i] + res[:, LANES:2 * LANES]
                ps[i] = res[:, 0:LANES].astype(BF16)

    wts = []
    for i in range(len(blocks)):
        zz = jnp.dot(ts[i].astype(BF16), z0s[i], preferred_element_type=F32)
        bottom = jnp.concatenate([jnp.zeros((2 * C, LANES), BF16), vbds[i]], axis=1)
        wts.append(jnp.concatenate([zz.astype(BF16), bottom], axis=0))
    outs = []
    for i, (d, hp) in enumerate(blocks):
        bb_t = bd(slab(d, 4, hp)).T.astype(BF16)
        kb_t = bd(slab(d, 5, hp)).T.astype(BF16)
        lhs = jnp.concatenate([lows[i], jnp.concatenate([bb_t, kb_t], axis=1)], axis=0)
        outs.append(jnp.dot(lhs, wts[i], preferred_element_type=F32))
    for i, (d, hp) in enumerate(blocks):
        sl = pl.ds(hp * LANES, LANES)
        o = outs[i]
        r_hat = bd(slab(d, 1, hp)) + o[0:2 * C, 0:LANES]
        p_mat = eye_f * w_c[d][:, hp * LANES:(hp + 1) * LANES] + o[2 * C:4 * C, 0:LANES]
        lhs = jnp.concatenate([r_hat, p_mat], axis=0).astype(BF16)
        res = jnp.dot(lhs, h_ref[d, hp].astype(BF16), preferred_element_type=F32)
        y = unbd(res[0:2 * C] + o[0:2 * C, LANES:2 * LANES])
        h_ref[d, hp] = res[2 * C:4 * C] + o[2 * C:4 * C, LANES:2 * LANES]
        if d == 0:
            yf_ref[:, sl] = y
        else:
            yb_ref[:, sl] = y


def _rwkv_mixer(z, w0, a0, w2p, a2p, g2p, k_k, k_a, r_k, d_rwkv):
    t, nr = z.shape
    C = CHUNK
    nc = t // C
    n_hp = d_rwkv // LANES

    def full(a):
        nd = a.ndim
        return pl.BlockSpec(a.shape, lambda c: (0,) * nd)

    out_spec_f = pl.BlockSpec((C, d_rwkv), lambda c: (c, 0))
    out_spec_b = pl.BlockSpec((C, d_rwkv), lambda c: (nc - 1 - c, 0))
    out_sd = jax.ShapeDtypeStruct((t, d_rwkv), F32)
    params = (w0, a0, w2p, a2p, g2p, k_k, k_a, r_k)
    return pl.pallas_call(
        functools.partial(_rwkv_kernel, nc=nc, d_rwkv=d_rwkv),
        grid=(nc,),
        in_specs=[pl.BlockSpec((C, nr), lambda c: (c, 0)),
                  pl.BlockSpec((C, nr), lambda c: (nc - 1 - c, 0))]
                 + [full(p) for p in params],
        out_specs=[out_spec_f, out_spec_b, out_spec_f, out_spec_f],
        out_shape=[out_sd, out_sd, out_sd, out_sd],
        scratch_shapes=[pltpu.VMEM((2, 6, C, d_rwkv), F32),
                        pltpu.VMEM((2, n_hp, LANES, LANES), F32)],
        compiler_params=_cparams(("arbitrary",)),
        name="rwkv7_chunk_scan",
    )(z, z, *params)


def _rope_table_kernel(pos_ref, inv_ref, c_ref, s1_ref, s2_ref):
    half = ROPE_DIM // 2
    pos = pos_ref[...].astype(F32)
    j = lax.broadcasted_iota(jnp.int32, (1, LANES), 1) % HEAD_DIM
    ang = pos * inv_ref[...]
    cs = jnp.cos(ang)
    sn = jnp.sin(ang)
    c_ref[...] = jnp.where(j < ROPE_DIM, cs, 1.0)
    s1_ref[...] = jnp.where(j < half, -sn, 0.0)
    s2_ref[...] = jnp.where((j >= half) & (j < ROPE_DIM), sn, 0.0)


def _rope_tables(positions, tm):
    t = positions.shape[0]
    half = ROPE_DIM // 2
    inv_freq = jnp.power(ROPE_THETA, -jnp.arange(half, dtype=F32) * 2.0 / ROPE_DIM)
    inv_lane = jnp.tile(inv_freq, LANES // half)[None]
    sd = jax.ShapeDtypeStruct((t, LANES), F32)
    spec = pl.BlockSpec((tm, LANES), lambda i: (i, 0))
    return pl.pallas_call(
        _rope_table_kernel,
        grid=(t // tm,),
        in_specs=[pl.BlockSpec((tm, 1), lambda i: (i, 0)),
                  pl.BlockSpec((1, LANES), lambda i: (0, 0))],
        out_specs=[spec, spec, spec],
        out_shape=[sd, sd, sd],
        compiler_params=_cparams(("arbitrary",)),
        name="rope_tables",
    )(positions, inv_lane)


def _rot(x, c, s1, s2):
    half = ROPE_DIM // 2
    return x * c + pltpu.roll(x, LANES - half, 1) * s1 + pltpu.roll(x, half, 1) * s2


def _norm_matmul_rope_kernel(x_ref, g_ref, w_ref, c_ref, s1_ref, s2_ref, o_ref, *, n_rot):
    x = x_ref[...]
    ms = jnp.mean(x * x, axis=-1, keepdims=True)
    h = x * lax.rsqrt(ms + RMS_EPS) * g_ref[...]
    z = jnp.dot(h.astype(BF16), w_ref[...], preferred_element_type=F32)
    c = c_ref[...]
    s1 = s1_ref[...]
    s2 = s2_ref[...]
    for s in range(z.shape[1] // LANES):
        sl = slice(s * LANES, (s + 1) * LANES)
        zs = z[:, sl]
        if s < n_rot:
            zs = _rot(zs, c, s1, s2)
            if s < n_rot // 2:
                zs = zs * (HEAD_DIM ** -0.5)
        o_ref[:, sl] = zs


def _norm_matmul_rope(x, g, w, tabs, tm):
    t, d = x.shape
    n = w.shape[1]
    tab = pl.BlockSpec((tm, LANES), lambda i: (i, 0))
    return pl.pallas_call(
        functools.partial(_norm_matmul_rope_kernel, n_rot=2 * (n // 3) // LANES),
        grid=(t // tm,),
        in_specs=[pl.BlockSpec((tm, d), lambda i: (i, 0)),
                  pl.BlockSpec((1, d), lambda i: (0, 0)),
                  pl.BlockSpec((d, n), lambda i: (0, 0)), tab, tab, tab],
        out_specs=pl.BlockSpec((tm, n), lambda i: (i, 0)),
        out_shape=jax.ShapeDtypeStruct((t, n), F32),
        compiler_params=_cparams(("arbitrary",)),
        name="norm_in_proj_rope",
    )(x, g, w, *tabs)


def _attn_kernel(q_ref, k_ref, v_ref, o_ref, op_ref, lp_ref, *, seq_t, sb):
    jsb = pl.program_id(1)
    lane = lax.broadcasted_iota(jnp.int32, (1, LANES), 1)
    m_a = lane < HEAD_DIM
    for p, (window, dil) in enumerate(DILATED_PATTERNS):
        radius = window // (2 * dil)
        win = Q_BLOCK + 2 * radius
        seq = seq_t // dil
        nq = sb // (dil * Q_BLOCK)
        qi = lax.broadcasted_iota(jnp.int32, (Q_BLOCK, win), 0)
        kj = lax.broadcasted_iota(jnp.int32, (Q_BLOCK, win), 1)

        def rows(start, size, dil=dil):
            return pl.ds(start, size) if dil == 1 else pl.ds(start, size, stride=dil)

        def body(idx, carry, dil=dil, radius=radius, win=win, seq=seq, nq=nq, qi=qi, kj=kj,
                 rows=rows, p=p):
            r = idx // nq
            jj = idx % nq
            q_l0 = (jsb * nq + jj) * Q_BLOCK
            start_l = jnp.clip(q_l0 - radius, 0, seq - win)
            q_rows = rows(r + dil * Q_BLOCK * jj, Q_BLOCK)
            k_rows = rows(r + dil * start_l, win)
            q = q_ref[q_rows, :].astype(BF16)
            kw = k_ref[k_rows, :].astype(BF16)
            vw = v_ref[k_rows, :].astype(BF16)
            valid = jnp.abs((q_l0 + qi) - (start_l + kj)) <= radius
            one = jnp.ones((), BF16)
            halves = ((jnp.where(m_a, q, 0), jnp.where(m_a, vw, one)),
                      (jnp.where(m_a, 0, q), jnp.where(m_a, one, vw)))
            res, mx = [], []
            for qh, vh in halves:
                sc = lax.dot_general(qh, kw, (((1,), (1,)), ((), ())), preferred_element_type=F32)
                sc = jnp.where(valid, sc, NEG_BIG)
                m = jnp.max(sc, axis=-1, keepdims=True)
                pexp = jnp.exp(sc - m).astype(BF16)
                res.append(jnp.dot(pexp, vh, preferred_element_type=F32))
                mx.append(m)
            num = jnp.where(m_a, res[0], res[1])
            den = pltpu.roll(jnp.where(m_a, res[1], res[0]), HEAD_DIM, 1)
            op_ref[p, q_rows, :] = num / den
            lp_ref[p, q_rows, :] = jnp.where(m_a, mx[0], mx[1]) + jnp.log(den)
            return carry

        lax.fori_loop(0, dil * nq, body, 0, unroll=8)

    l1 = lp_ref[0]
    l2 = lp_ref[1]
    l3 = lp_ref[2]
    m = jnp.maximum(jnp.maximum(l1, l2), l3)
    e1 = jnp.exp(l1 - m)
    e2 = jnp.exp(l2 - m)
    e3 = jnp.exp(l3 - m)
    o_ref[...] = (e1 * op_ref[0] + e2 * op_ref[1] + e3 * op_ref[2]) / (e1 + e2 + e3)


def _dilated_attention(z_attn, d_attn, sb):
    t = z_attn.shape[0]
    n_hp = d_attn // LANES
    n_pat = len(DILATED_PATTERNS)
    return pl.pallas_call(
        functools.partial(_attn_kernel, seq_t=t, sb=sb),
        grid=(n_hp, t // sb),
        in_specs=[pl.BlockSpec((sb, LANES), lambda h, j: (j, h)),
                  pl.BlockSpec((t, LANES), lambda h, j: (0, n_hp + h)),
                  pl.BlockSpec((t, LANES), lambda h, j: (0, 2 * n_hp + h))],
        out_specs=pl.BlockSpec((sb, LANES), lambda h, j: (j, h)),
        out_shape=jax.ShapeDtypeStruct((t, d_attn), F32),
        scratch_shapes=[pltpu.VMEM((n_pat, sb, LANES), F32), pltpu.VMEM((n_pat, sb, LANES), F32)],
        compiler_params=_cparams(("arbitrary", "arbitrary")),
        name="dilated_attention",
    )(z_attn, z_attn, z_attn)


def _out_proj_kernel(x_ref, yf_ref, yb_ref, bonus_ref, gate_ref, lnw_ref, lnb_ref,
                     ya_ref, w_ref, out_ref, yr_ref, *, d_rwkv):
    ones_bd = _head_ones()
    for hp in range(d_rwkv // LANES):
        sl = pl.ds(hp * LANES, LANES)
        y = yf_ref[:, sl] + yb_ref[:, sl]
        mean = _head_sum(y, ones_bd) * (1.0 / HEAD_DIM)
        yc = y - mean
        var = _head_sum(yc * yc, ones_bd) * (1.0 / HEAD_DIM)
        yn = yc * lax.rsqrt(var + GN_EPS) * lnw_ref[:, sl] + lnb_ref[:, sl]
        yr_ref[:, sl] = ((yn + bonus_ref[:, sl]) * gate_ref[:, sl]).astype(BF16)
    acc = jnp.dot(yr_ref[...], w_ref[0:d_rwkv, :], preferred_element_type=F32)
    acc += jnp.dot(ya_ref[...].astype(BF16), w_ref[d_rwkv:, :], preferred_element_type=F32)
    out_ref[...] = x_ref[...] + acc


def _out_proj(x, yf, yb, bonus, gate, ln_w, ln_b, y_attn, w_out, tm):
    t, d = x.shape
    d_rwkv = yf.shape[1]
    d_attn = y_attn.shape[1]
    row = lambda n: pl.BlockSpec((tm, n), lambda i: (i, 0))
    const = lambda a: pl.BlockSpec(a.shape, lambda i: (0, 0))
    return pl.pallas_call(
        functools.partial(_out_proj_kernel, d_rwkv=d_rwkv),
        grid=(t // tm,),
        in_specs=[row(d)] + [row(d_rwkv)] * 4 + [const(ln_w), const(ln_b)]
                 + [row(d_attn), const(w_out)],
        out_specs=row(d),
        out_shape=jax.ShapeDtypeStruct((t, d), F32),
        scratch_shapes=[pltpu.VMEM((tm, d_rwkv), BF16)],
        compiler_params=_cparams(("arbitrary",)),
        name="merge_out_proj",
    )(x, yf, yb, bonus, gate, ln_w, ln_b, y_attn, w_out)


def _router_kernel(x_ref, g_ref, w_ref, b_ref, h_ref, idx_ref, wt_ref):
    x = x_ref[...]
    ms = jnp.mean(x * x, axis=-1, keepdims=True)
    h = x * lax.rsqrt(ms + RMS_EPS) * g_ref[...]
    h_ref[...] = h
    h_hi, h_lo = _split2(h)
    w_hi = w_ref[0]
    w_lo = w_ref[1]
    logits = (jnp.dot(h_hi, w_hi, preferred_element_type=F32)
              + jnp.dot(h_hi, w_lo, preferred_element_type=F32)
              + jnp.dot(h_lo, w_hi, preferred_element_type=F32)) + b_ref[...]
    lane = lax.broadcasted_iota(jnp.int32, logits.shape, 1).astype(F32)
    big = jnp.float32(LANES)
    is_g = lane < N_GROUPS
    gl = jnp.where(is_g, logits, NEG_BIG)
    gmax = jnp.max(gl, axis=-1, keepdims=True)
    gsel = jnp.min(jnp.where(is_g & (gl == gmax), lane, big), axis=-1, keepdims=True)
    g1 = 1.0 / jnp.sum(jnp.where(is_g, jnp.exp(gl - gmax), 0.0), axis=-1, keepdims=True)
    lo = N_GROUPS + gsel * EXPERTS_PER_GROUP
    in_grp = (lane >= lo) & (lane < lo + EXPERTS_PER_GROUP)
    el = jnp.where(in_grp, logits, NEG_BIG)
    v1 = jnp.max(el, axis=-1, keepdims=True)
    i1 = jnp.min(jnp.where(in_grp & (el == v1), lane, big), axis=-1, keepdims=True)
    rest = in_grp & (lane != i1)
    el2 = jnp.where(rest, logits, NEG_BIG)
    v2 = jnp.max(el2, axis=-1, keepdims=True)
    i2 = jnp.min(jnp.where(rest & (el2 == v2), lane, big), axis=-1, keepdims=True)
    e2 = jnp.exp(v2 - v1)
    ww1 = 1.0 / (1.0 + e2)
    ww2 = e2 / (1.0 + e2)
    idx = jnp.where(lane == 0, i1 - N_GROUPS, jnp.where(lane == 1, i2 - N_GROUPS, 0.0))
    idx_ref[...] = idx.astype(jnp.int32)
    wt_ref[...] = jnp.where(lane == 0, g1 * ww1, jnp.where(lane == 1, g1 * ww2, 0.0))


def _router(x, g, w_split, b_pad, tm):
    t, d = x.shape
    return pl.pallas_call(
        _router_kernel,
        grid=(t // tm,),
        in_specs=[pl.BlockSpec((tm, d), lambda i: (i, 0)),
                  pl.BlockSpec((1, d), lambda i: (0, 0)),
                  pl.BlockSpec((2, d, LANES), lambda i: (0, 0, 0)),
                  pl.BlockSpec((1, LANES), lambda i: (0, 0))],
        out_specs=[pl.BlockSpec((tm, d), lambda i: (i, 0)),
                   pl.BlockSpec((tm, LANES), lambda i: (i, 0)),
                   pl.BlockSpec((tm, LANES), lambda i: (i, 0))],
        out_shape=[jax.ShapeDtypeStruct((t, d), F32),
                   jax.ShapeDtypeStruct((t, LANES), jnp.int32),
                   jax.ShapeDtypeStruct((t, LANES), F32)],
        compiler_params=_cparams(("arbitrary",)),
        name="router",
    )(x, g, w_split, b_pad)


def _moe_kernel(te_ref, nt_ref, nv_ref, rows_ref, h_hbm, wg_ref, wu_ref, wd_ref, o_ref,
                xg_ref, xb_ref, acc_ref, sem, *, tm, nf):
    i = pl.program_id(0)
    f = pl.program_id(1)
    nt = nt_ref[0]
    active = i < nt
    slot = i % 2
    n_slab = xg_ref.shape[2] // LANES

    def issue_tile(tile, buf):
        def body(g, carry):
            for u in range(GATHER_UNROLL):
                j = g * GATHER_UNROLL + u
                src = h_hbm.at[pl.ds(rows_ref[tile * tm + j], 1), :]
                pltpu.make_async_copy(src, xg_ref.at[buf, pl.ds(j, 1), :],
                                      sem.at[buf]).start(priority=1)
            return carry
        lax.fori_loop(0, nv_ref[tile] // GATHER_UNROLL, body, 0)

    def wait_tile(tile, buf):
        def body(g, carry):
            pltpu.make_async_copy(h_hbm.at[pl.ds(0, GATHER_CHUNK), :],
                                  xg_ref.at[buf, pl.ds(0, GATHER_CHUNK), :], sem.at[buf]).wait()
            return carry
        lax.fori_loop(0, nv_ref[tile] // GATHER_CHUNK, body, 0)

    @pl.when((i == 0) & (f == 0))
    def _():
        xg_ref[...] = jnp.zeros_like(xg_ref)
        issue_tile(0, 0)

    @pl.when(active & (f == 0))
    def _():
        wait_tile(i, slot)
        xb_ref[...] = xg_ref[slot].astype(BF16)
        issue_tile(i + 1, 1 - slot)

    @pl.when(active)
    def _():
        xb = xb_ref[...]
        gate = jnp.dot(xb, wg_ref[...].astype(BF16), preferred_element_type=F32)
        up = jnp.dot(xb, wu_ref[...].astype(BF16), preferred_element_type=F32)
        hid = (gate * _sigmoid(gate)) * up
        part = jnp.dot(hid.astype(BF16), wd_ref[...].astype(BF16), preferred_element_type=F32)

        @pl.when(f == 0)
        def _():
            acc_ref[...] = part

        @pl.when((f != 0) & (f != nf - 1))
        def _():
            acc_ref[...] += part

        @pl.when(f == nf - 1)
        def _():
            for c in range(n_slab):
                sl = slice(c * LANES, (c + 1) * LANES)
                o_ref[pl.ds(c, tm, stride=n_slab), :] = acc_ref[:, sl] + part[:, sl]

    @pl.when(jnp.logical_not(active) & (f == 0))
    def _():
        o_ref[...] = jnp.zeros_like(o_ref)


def _moe(h, tile_expert, n_tiles, n_valid, rows, w_gate, w_up, w_down, tm, tf):
    t, d = h.shape
    n_e, _, d_e = w_gate.shape
    n_slab = d // LANES
    nt_max = tile_expert.shape[0]
    nf = d_e // tf
    assert nf >= 2 and tm % GATHER_CHUNK == 0 and GATHER_CHUNK % GATHER_UNROLL == 0
    assert rows.shape[0] == (nt_max + 1) * tm and n_valid.shape[0] == nt_max + 1

    def fidx(i, f, nt):
        return jnp.where(i < nt[0], f, nf - 1)

    grid_spec = pltpu.PrefetchScalarGridSpec(
        num_scalar_prefetch=4,
        grid=(nt_max, d_e // tf),
        in_specs=[pl.BlockSpec(memory_space=pl.ANY),
                  pl.BlockSpec((None, d, tf), lambda i, f, te, nt, nv, rw: (te[i], 0, fidx(i, f, nt))),
                  pl.BlockSpec((None, d, tf), lambda i, f, te, nt, nv, rw: (te[i], 0, fidx(i, f, nt))),
                  pl.BlockSpec((None, tf, d), lambda i, f, te, nt, nv, rw: (te[i], fidx(i, f, nt), 0))],
        out_specs=pl.BlockSpec((tm * n_slab, LANES), lambda i, f, te, nt, nv, rw: (i, 0)),
        scratch_shapes=[pltpu.VMEM((2, tm, d), F32), pltpu.VMEM((tm, d), BF16),
                        pltpu.VMEM((tm, d), F32), pltpu.SemaphoreType.DMA((2,))],
    )
    return pl.pallas_call(
        functools.partial(_moe_kernel, tm=tm, nf=nf),
        grid_spec=grid_spec,
        out_shape=jax.ShapeDtypeStruct((nt_max * tm * n_slab, LANES), F32),
        compiler_params=_cparams(("arbitrary", "arbitrary")),
        name="grouped_expert_mlp",
    )(tile_expert, n_tiles, n_valid, rows, h, w_gate, w_up, w_down)


def _final_kernel(s0_ref, s1_ref, x_ref, wt_ref, g_ref, y_hbm, o_ref, yg_ref, sem, *, tm, nt):
    i = pl.program_id(0)
    slot = i % 2
    d = x_ref.shape[1]
    n_slab = d // LANES

    def issue(tile, buf):
        def body(j, carry):
            for k, s_ref in enumerate((s0_ref, s1_ref)):
                src = y_hbm.at[pl.ds(pl.multiple_of(s_ref[tile * tm + j] * n_slab, n_slab), n_slab), :]
                dst = yg_ref.at[buf, k, pl.ds(pl.multiple_of(j * n_slab, n_slab), n_slab), :]
                pltpu.make_async_copy(src, dst, sem.at[buf]).start(priority=k)
            return carry
        lax.fori_loop(0, tm, body, 0, unroll=8)

    @pl.when(i == 0)
    def _():
        issue(0, 0)

    @pl.when(i + 1 < nt)
    def _():
        issue(i + 1, 1 - slot)

    for k in range(2):
        pltpu.make_async_copy(y_hbm.at[pl.ds(0, tm * n_slab), :], yg_ref.at[slot, k],
                              sem.at[slot]).wait()

    wt = wt_ref[...]
    w0 = wt[:, 0:1]
    w1 = wt[:, 1:2]
    ss = jnp.zeros((tm, 1), F32)
    for c in range(n_slab):
        sl = slice(c * LANES, (c + 1) * LANES)
        rows = pl.ds(c, tm, stride=n_slab)
        xs = x_ref[:, sl] + w0 * yg_ref[slot, 0, rows, :] + w1 * yg_ref[slot, 1, rows, :]
        o_ref[:, sl] = xs
        ss = ss + jnp.sum(xs * xs, axis=-1, keepdims=True)
    o_ref[...] = o_ref[...] * lax.rsqrt(ss * (1.0 / d) + RMS_EPS) * g_ref[...]


def _final(slot0, slot1, x, wts, g, y_sorted, tm):
    t, d = x.shape
    grid_spec = pltpu.PrefetchScalarGridSpec(
        num_scalar_prefetch=2,
        grid=(t // tm,),
        in_specs=[pl.BlockSpec((tm, d), lambda i, s0, s1: (i, 0)),
                  pl.BlockSpec((tm, LANES), lambda i, s0, s1: (i, 0)),
                  pl.BlockSpec((1, d), lambda i, s0, s1: (0, 0)),
                  pl.BlockSpec(memory_space=pl.ANY)],
        out_specs=pl.BlockSpec((tm, d), lambda i, s0, s1: (i, 0)),
        scratch_shapes=[pltpu.VMEM((2, 2, tm * (d // LANES), LANES), F32),
                        pltpu.SemaphoreType.DMA((2,))],
    )
    return pl.pallas_call(
        functools.partial(_final_kernel, tm=tm, nt=t // tm),
        grid_spec=grid_spec,
        out_shape=jax.ShapeDtypeStruct((t, d), F32),
        compiler_params=_cparams(("arbitrary",)),
        name="combine_final_norm",
    )(slot0, slot1, x, wts, g, y_sorted)


def _dispatch_plan(experts, tm):
    t = experts.shape[0]
    ef = experts.T.reshape(-1)
    onehot = (ef[:, None] == jnp.arange(N_EXPERTS, dtype=jnp.int32)[None, :]).astype(jnp.int32)
    csum = jnp.cumsum(onehot, axis=0)
    rank = jnp.take_along_axis(csum, ef[:, None], axis=1)[:, 0] - 1
    counts = csum[-1]
    tiles_e = (counts + tm - 1) // tm
    tile_end = jnp.cumsum(tiles_e)
    tile_start = tile_end - tiles_e
    n_tiles = tile_end[-1]
    nt_max = (2 * t) // tm + N_EXPERTS
    slot = tile_start[ef] * tm + rank
    tid = jnp.minimum(jnp.arange(nt_max, dtype=jnp.int32), n_tiles - 1)
    tile_expert = jnp.sum((tile_end[None, :] <= tid[:, None]).astype(jnp.int32), axis=1)
    tok = jnp.tile(jnp.arange(t, dtype=jnp.int32), 2)
    rows = jnp.zeros(((nt_max + 1) * tm,), jnp.int32).at[slot].set(tok)
    tix = jnp.arange(nt_max + 1, dtype=jnp.int32)
    te_all = jnp.concatenate([tile_expert, tile_expert[-1:]])
    left = counts[te_all] - (tix - tile_start[te_all]) * tm
    n_valid = jnp.where(tix < n_tiles, jnp.clip(left, 0, tm), 0)
    n_valid = (n_valid + GATHER_CHUNK - 1) // GATHER_CHUNK * GATHER_CHUNK
    return (slot[:t], slot[t:], rows, tile_expert, n_tiles.reshape(1).astype(jnp.int32),
            n_valid.astype(jnp.int32))


def kernel(x, positions, norm_mix, w_in, mu_shift, w0, w2, a0, a2, g2, k_k, k_a, r_k, ln_x_w, ln_x_b, w_out, norm_ffn, router_group_w, router_group_b, router_expert_w, router_expert_b, w_gate, w_up, w_down, norm_final):
    bsz, seq, d = x.shape
    assert bsz == 1
    depth = w_in.shape[0]
    d_rwkv = k_k.shape[1]
    d_attn = w_out.shape[1] - d_rwkv
    rwkv_cols = mu_shift.shape[2]
    nr = -(-rwkv_cols // (2 * LANES)) * (2 * LANES)
    assert 3 * d_rwkv + 2 * LANES + GATE_LORA == rwkv_cols and nr == 3 * d_rwkv + 4 * LANES
    tm_moe = 576
    tf_moe = 512
    sb_attn = max(dil for _, dil in DILATED_PATTERNS) * Q_BLOCK
    assert seq % sb_attn == 0

    xt = x.reshape(seq, d)
    tabs = _rope_tables(positions.reshape(seq, 1), 512)
    for l in range(depth):
        w_r, w_a = _w_in_split(w_in, l, rwkv_cols, nr, 256)
        mu = jnp.pad(mu_shift[l], ((0, 0), (0, nr - rwkv_cols)))
        zl = jnp.zeros((DECAY_LORA, d_rwkv), F32)
        w2p = jnp.stack([jnp.concatenate([w2[l, 0], zl]), jnp.concatenate([zl, w2[l, 1]])]).astype(BF16)
        a2p = jnp.stack([jnp.concatenate([a2[l, 0], zl]), jnp.concatenate([zl, a2[l, 1]])]).astype(BF16)
        g2p = jnp.pad(g2[l], ((0, 2 * LANES - GATE_LORA), (0, 0))).astype(BF16)
        rk = r_k[l].reshape(1, d_rwkv)

        z_r = _norm_matmul_shift(xt, norm_mix[l][None], w_r, mu, 256)
        z_a = _norm_matmul_rope(xt, norm_mix[l][None], w_a, tabs, 256)
        yf, yb, bonus, gate = _rwkv_mixer(z_r, w0[l], a0[l], w2p, a2p, g2p,
                                          k_k[l][None], k_a[l][None], rk, d_rwkv)
        y_attn = _dilated_attention(z_a, d_attn, sb_attn)
        x2 = _out_proj(xt, yf, yb, bonus, gate, ln_x_w[l][None], ln_x_b[l][None],
                       y_attn, w_out[l].astype(BF16), 256)

        w_rt = jnp.concatenate(
            [router_group_w[l], router_expert_w[l].transpose(1, 0, 2).reshape(d, N_EXPERTS)], axis=1)
        w_rt = jnp.pad(w_rt, ((0, 0), (0, LANES - w_rt.shape[1])))
        rt_hi = w_rt.astype(BF16)
        rt_lo = (w_rt - rt_hi.astype(F32)).astype(BF16)
        b_rt = jnp.concatenate([router_group_b[l], router_expert_b[l].reshape(-1)])
        b_rt = jnp.pad(b_rt, (0, LANES - b_rt.shape[0]))[None]
        h, idx, wts = _router(x2, norm_ffn[l][None], jnp.stack([rt_hi, rt_lo]), b_rt, 256)

        slot0, slot1, rows, tile_expert, n_tiles, n_valid = _dispatch_plan(idx[:, :2], tm_moe)
        y_sorted = _moe(h, tile_expert, n_tiles, n_valid, rows, w_gate[l], w_up[l], w_down[l],
                        tm_moe, tf_moe)
        is_last = l == depth - 1
        assert is_last, "the combine kernel applies the final norm; depth must be 1"
        xt = _final(slot0, slot1, x2, wts, norm_final[None], y_sorted, 256)
    return xt.reshape(bsz, seq, d)
```

```python
import functools

import jax
import jax.numpy as jnp
from jax import lax
from jax.experimental import pallas as pl
from jax.experimental.pallas import tpu as pltpu

HEAD_DIM = 64
LANES = 128
SUBLANES = 8
DECAY_LORA = 64
ICLR_LORA = 64
GATE_LORA = 160
GN_EPS = 64e-5
RMS_EPS = 1e-6
ROPE_THETA = 500000.0
ROPE_DIM = HEAD_DIM // 4
DILATED_PATTERNS = ((128, 1), (512, 4), (2048, 16))
Q_BLOCK = 128
N_GROUPS = 4
EXPERTS_PER_GROUP = 8
N_EXPERTS = N_GROUPS * EXPERTS_PER_GROUP
NEG_BIG = -1e30
CHUNK = 64
GATHER_CHUNK = 64
GATHER_UNROLL = 8
VMEM_LIMIT = 56 * 1024 * 1024

BF16 = jnp.bfloat16
F32 = jnp.float32


def _cparams(sem):
    return pltpu.CompilerParams(dimension_semantics=sem, vmem_limit_bytes=VMEM_LIMIT)


def _dot(a, b):
    return jnp.dot(a.astype(BF16), b.astype(BF16), preferred_element_type=F32)


def _dot_nt(a, b):
    return lax.dot_general(a.astype(BF16), b.astype(BF16), (((1,), (1,)), ((), ())),
                           preferred_element_type=F32)


def _dot_tn(a, b):
    return lax.dot_general(a.astype(BF16), b.astype(BF16), (((0,), (0,)), ((), ())),
                           preferred_element_type=F32)


def _split2(x):
    hi = x.astype(BF16)
    lo = (x - hi.astype(F32)).astype(BF16)
    return hi, lo


def _split3(x):
    hi = x.astype(BF16)
    r1 = x - hi.astype(F32)
    mid = r1.astype(BF16)
    lo = (r1 - mid.astype(F32)).astype(BF16)
    return hi, mid, lo


def _head_ones():
    i = lax.broadcasted_iota(jnp.int32, (LANES, LANES), 0) // HEAD_DIM
    j = lax.broadcasted_iota(jnp.int32, (LANES, LANES), 1) // HEAD_DIM
    return (i == j).astype(BF16)


def _head_sum(x, ones_bd):
    hi, lo = _split2(x)
    return (jnp.dot(hi, ones_bd, preferred_element_type=F32)
            + jnp.dot(lo, ones_bd, preferred_element_type=F32))


def _sigmoid(x):
    return 1.0 / (1.0 + jnp.exp(-x))


def _w_in_split_kernel(w_ref, wr_ref, wa_ref, *, rwkv_cols):
    w = w_ref[...]
    nr = wr_ref.shape[1]
    wr_ref[:, 0:rwkv_cols] = w[:, 0:rwkv_cols].astype(BF16)
    wr_ref[:, rwkv_cols:nr] = jnp.zeros((w.shape[0], nr - rwkv_cols), BF16)
    wa_ref[...] = w[:, rwkv_cols:].astype(BF16)


def _w_in_split(w_all, layer, rwkv_cols, nr, tk):
    _, d, n = w_all.shape
    return pl.pallas_call(
        functools.partial(_w_in_split_kernel, rwkv_cols=rwkv_cols),
        grid=(d // tk,),
        in_specs=[pl.BlockSpec((None, tk, n), lambda i: (layer, i, 0))],
        out_specs=[pl.BlockSpec((tk, nr), lambda i: (i, 0)),
                   pl.BlockSpec((tk, n - rwkv_cols), lambda i: (i, 0))],
        out_shape=[jax.ShapeDtypeStruct((d, nr), BF16),
                   jax.ShapeDtypeStruct((d, n - rwkv_cols), BF16)],
        compiler_params=_cparams(("arbitrary",)),
        name="w_in_split",
    )(w_all)


def _norm_matmul_shift_kernel(x_ref, xp_ref, xn_ref, g_ref, w_ref, mu_ref, o_ref, *, nt):
    i = pl.program_id(0)
    tm = x_ref.shape[0]
    halo = xp_ref.shape[0]
    x = jnp.concatenate([xp_ref[...], x_ref[...], xn_ref[...]], axis=0)
    ms = jnp.mean(x * x, axis=-1, keepdims=True)
    h = x * lax.rsqrt(ms + RMS_EPS) * g_ref[...]
    z = lax.dot_general(h.astype(BF16), w_ref[...], (((1,), (1,)), ((), ())),
                        preferred_element_type=F32)
    rows = tm + 2 * halo
    zc = z[halo:halo + tm]
    z_prev = pltpu.roll(z, 1, 0)[halo:halo + tm]
    z_next = pltpu.roll(z, rows - 1, 0)[halo:halo + tm]
    row = lax.broadcasted_iota(jnp.int32, (tm, 1), 0)
    z_prev = jnp.where((i == 0) & (row == 0), 0.0, z_prev)
    z_next = jnp.where((i == nt - 1) & (row == tm - 1), 0.0, z_next)
    o_ref[...] = zc + mu_ref[0:1, :] * (z_prev - zc) + mu_ref[1:2, :] * (z_next - zc)


def _norm_matmul_shift(x, g, w, mu, tm):
    t, d = x.shape
    n = w.shape[0]
    nt = t // tm
    hb = tm // SUBLANES
    last = t // SUBLANES - 1
    return pl.pallas_call(
        functools.partial(_norm_matmul_shift_kernel, nt=nt),
        grid=(nt,),
        in_specs=[pl.BlockSpec((tm, d), lambda i: (i, 0)),
                  pl.BlockSpec((SUBLANES, d), lambda i: (jnp.maximum(i * hb - 1, 0), 0)),
                  pl.BlockSpec((SUBLANES, d), lambda i: (jnp.minimum((i + 1) * hb, last), 0)),
                  pl.BlockSpec((1, d), lambda i: (0, 0)),
                  pl.BlockSpec((n, d), lambda i: (0, 0)),
                  pl.BlockSpec((2, n), lambda i: (0, 0))],
        out_specs=pl.BlockSpec((tm, n), lambda i: (i, 0)),
        out_shape=jax.ShapeDtypeStruct((t, n), F32),
        compiler_params=_cparams(("arbitrary",)),
        name="norm_in_proj_shift",
    )(x, x, x, g, w, mu)


def _rwkv_kernel(zf_ref, zb_ref,
                 w0_ref, a0_ref, w2_ref, a2_ref, g2_ref, kk_ref, ka_ref, rk_ref,
                 yf_ref, yb_ref, bonus_ref, gate_ref,
                 prep_ref, h_ref, *, nc, d_rwkv):
    c = pl.program_id(0)
    C = CHUNK
    n_hp = d_rwkv // LANES

    @pl.when(c == 0)
    def _():
        h_ref[...] = jnp.zeros_like(h_ref)

    zs = (zf_ref, zb_ref)
    ones_bd = _head_ones()
    lane = lax.broadcasted_iota(jnp.int32, (1, LANES), 1)
    m_a = lane < HEAD_DIM
    ti = lax.broadcasted_iota(jnp.int32, (C, C), 0)
    tj = lax.broadcasted_iota(jnp.int32, (C, C), 1)
    bi = lax.broadcasted_iota(jnp.int32, (2 * C, 2 * C), 0)
    bj = lax.broadcasted_iota(jnp.int32, (2 * C, 2 * C), 1)
    same_blk = (bi // C) == (bj // C)
    eye = bi == bj
    cum_mat = ((tj <= ti).astype(BF16), (tj >= ti).astype(BF16))
    strict = (same_blk & ((bj % C) < (bi % C)), same_blk & ((bj % C) > (bi % C)))
    incl = (same_blk & ((bj % C) <= (bi % C)), same_blk & ((bj % C) >= (bi % C)))
    last_row = (C - 1, 0)
    off_wl = 3 * d_rwkv
    off_al = off_wl + LANES
    off_gl = off_al + LANES

    def to_rows(x):
        return jnp.concatenate([x[:, s * LANES:(s + 1) * LANES] for s in range(n_hp)], axis=0)

    def to_lanes(x):
        return jnp.concatenate([x[s * C:(s + 1) * C] for s in range(n_hp)], axis=1)

    def bd(x):
        return jnp.concatenate([jnp.where(m_a, x, 0), jnp.where(m_a, 0, x)], axis=0)

    def unbd(x):
        return x[0:C] + x[C:2 * C]

    w_c = []
    for d in range(2):
        r = zs[d][:, 0:d_rwkv]
        k = zs[d][:, d_rwkv:2 * d_rwkv]
        kkr = k * kk_ref[...]
        sums = [to_rows(kkr * kkr)]
        if d == 0:
            sums.append(to_rows(r * k * rk_ref[...]))
        hs = _head_sum(jnp.concatenate(sums, axis=0), ones_bd)
        kk = kkr * lax.rsqrt(to_lanes(hs[0:n_hp * C]) + 1e-12)
        if d == 0:
            bonus_ref[...] = to_lanes(hs[n_hp * C:2 * n_hp * C]) * zs[0][:, 2 * d_rwkv:3 * d_rwkv]
            gate_ref[...] = _dot(_sigmoid(zs[0][:, off_gl:off_gl + 2 * LANES]), g2_ref[...])
        u = w0_ref[d:d + 1, :] + _dot(jnp.tanh(zs[d][:, off_wl:off_wl + LANES]), w2_ref[d])
        w_log = -(jnp.maximum(-u, 0.0) + jnp.log1p(jnp.exp(-jnp.abs(u)))) - 0.5
        logd = -jnp.exp(w_log)
        a = _sigmoid(a0_ref[d:d + 1, :] + _dot(zs[d][:, off_al:off_al + LANES], a2_ref[d]))
        kd = k * (1.0 + (a - 1.0) * ka_ref[...])
        b = kk * a
        l_hi, l_mid, l_lo = _split3(logd)
        cm = cum_mat[d]
        cum = (jnp.dot(cm, l_hi, preferred_element_type=F32)
               + jnp.dot(cm, l_mid, preferred_element_type=F32)
               + jnp.dot(cm, l_lo, preferred_element_type=F32))
        cum_c = cum[last_row[d]:last_row[d] + 1, :]
        w_inv = jnp.exp(-cum)
        w_c.append(jnp.exp(cum_c))
        w_end = w_c[d] * w_inv
        prep_ref[d, 0] = -kk * jnp.exp(cum - logd)
        prep_ref[d, 1] = r * jnp.exp(cum)
        prep_ref[d, 2] = b * w_inv
        prep_ref[d, 3] = kd * w_inv
        prep_ref[d, 4] = b * w_end
        prep_ref[d, 5] = kd * w_end

    blocks = [(d, hp) for hp in range(n_hp) for d in range(2)]

    def slab(d, i, hp):
        return prep_ref[d, i, :, hp * LANES:(hp + 1) * LANES]

    incl2 = tuple(jnp.concatenate([m, m], axis=1) for m in incl)
    eye_f = eye.astype(F32)
    xs, aks, lows, z0s, vbds = [], [], [], [], []
    for d, hp in blocks:
        at = bd(slab(d, 0, hp).astype(BF16))
        rt = bd(slab(d, 1, hp).astype(BF16))
        bt = slab(d, 2, hp).astype(BF16)
        kt = slab(d, 3, hp).astype(BF16)
        lhs = jnp.concatenate([at, rt], axis=0)
        rhs = jnp.concatenate([bt, bt, kt, kt], axis=0)
        g = lax.dot_general(lhs, rhs, (((1,), (1,)), ((), ())), preferred_element_type=F32)
        xs.append(jnp.where(strict[d], g[0:2 * C, 0:2 * C], 0.0))
        aks.append(jnp.where(strict[d], g[0:2 * C, 2 * C:4 * C], 0.0).astype(BF16))
        lows.append(jnp.where(incl2[d], g[2 * C:4 * C, :], 0.0).astype(BF16))
    for i, (d, hp) in enumerate(blocks):
        vbd = bd(zs[d][:, 2 * d_rwkv + hp * LANES:2 * d_rwkv + (hp + 1) * LANES].astype(BF16))
        vbds.append(vbd)
        akv = jnp.dot(aks[i], vbd, preferred_element_type=F32)
        z0s.append(jnp.concatenate([bd(slab(d, 0, hp).astype(BF16)), akv.astype(BF16)], axis=1))

    n_sq = C.bit_length() - 1
    ts = [eye_f + x for x in xs]
    ps = [x.astype(BF16) for x in xs]
    for j in range(1, n_sq):
        last = j == n_sq - 1
        for i in range(len(blocks)):
            if j == 1:
                ps[i] = jnp.dot(ps[i], ps[i], preferred_element_type=F32).astype(BF16)
            tb = ts[i].astype(BF16)
            if last:
                ts[i] = ts[i] + jnp.dot(ps[i], tb, preferred_element_type=F32)
            else:
                res = jnp.dot(ps[i], jnp.concatenate([ps[i], tb], axis=1), preferred_element_type=F32)
                ts[i] = ts[i] + res[:, LANES:2 * LANES]
                ps[i] = res[:, 0:LANES].astype(BF16)

    wts = []
    for i in range(len(blocks)):
        zz = jnp.dot(ts[i].astype(BF16), z0s[i], preferred_element_type=F32)
        bottom = jnp.concatenate([jnp.zeros((2 * C, LANES), BF16), vbds[i]], axis=1)
        wts.append(jnp.concatenate([zz.astype(BF16), bottom], axis=0))
    outs = []
    for i, (d, hp) in enumerate(blocks):
        bb_t = bd(slab(d, 4, hp)).T.astype(BF16)
        kb_t = bd(slab(d, 5, hp)).T.astype(BF16)
        lhs = jnp.concatenate([lows[i], jnp.concatenate([bb_t, kb_t], axis=1)], axis=0)
        outs.append(jnp.dot(lhs, wts[i], preferred_element_type=F32))
    for i, (d, hp) in enumerate(blocks):
        sl = pl.ds(hp * LANES, LANES)
        o = outs[i]
        r_hat = bd(slab(d, 1, hp)) + o[0:2 * C, 0:LANES]
        p_mat = eye_f * w_c[d][:, hp * LANES:(hp + 1) * LANES] + o[2 * C:4 * C, 0:LANES]
        lhs = jnp.concatenate([r_hat, p_mat], axis=0).astype(BF16)
        res = jnp.dot(lhs, h_ref[d, hp].astype(BF16), preferred_element_type=F32)
        y = unbd(res[0:2 * C] + o[0:2 * C, LANES:2 * LANES])
        h_ref[d, hp] = res[2 * C:4 * C] + o[2 * C:4 * C, LANES:2 * LANES]
        if d == 0:
            yf_ref[:, sl] = y
        else:
            yb_ref[:, sl] = y


def _rwkv_mixer(z, w0, a0, w2p, a2p, g2p, k_k, k_a, r_k, d_rwkv):
    t, nr = z.shape
    C = CHUNK
    nc = t // C
    n_hp = d_rwkv // LANES

    def full(a):
        nd = a.ndim
        return pl.BlockSpec(a.shape, lambda c: (0,) * nd)

    out_spec_f = pl.BlockSpec((C, d_rwkv), lambda c: (c, 0))
    out_spec_b = pl.BlockSpec((C, d_rwkv), lambda c: (nc - 1 - c, 0))
    out_sd = jax.ShapeDtypeStruct((t, d_rwkv), F32)
    params = (w0, a0, w2p, a2p, g2p, k_k, k_a, r_k)
    return pl.pallas_call(
        functools.partial(_rwkv_kernel, nc=nc, d_rwkv=d_rwkv),
        grid=(nc,),
        in_specs=[pl.BlockSpec((C, nr), lambda c: (c, 0)),
                  pl.BlockSpec((C, nr), lambda c: (nc - 1 - c, 0))]
                 + [full(p) for p in params],
        out_specs=[out_spec_f, out_spec_b, out_spec_f, out_spec_f],
        out_shape=[out_sd, out_sd, out_sd, out_sd],
        scratch_shapes=[pltpu.VMEM((2, 6, C, d_rwkv), F32),
                        pltpu.VMEM((2, n_hp, LANES, LANES), F32)],
        compiler_params=_cparams(("arbitrary",)),
        name="rwkv7_chunk_scan",
    )(z, z, *params)


def _rope_table_kernel(pos_ref, inv_ref, c_ref, s1_ref, s2_ref):
    half = ROPE_DIM // 2
    pos = pos_ref[...].astype(F32)
    j = lax.broadcasted_iota(jnp.int32, (1, LANES), 1) % HEAD_DIM
    ang = pos * inv_ref[...]
    cs = jnp.cos(ang)
    sn = jnp.sin(ang)
    c_ref[...] = jnp.where(j < ROPE_DIM, cs, 1.0)
    s1_ref[...] = jnp.where(j < half, -sn, 0.0)
    s2_ref[...] = jnp.where((j >= half) & (j < ROPE_DIM), sn, 0.0)


def _rope_tables(positions, tm):
    t = positions.shape[0]
    half = ROPE_DIM // 2
    inv_freq = jnp.power(ROPE_THETA, -jnp.arange(half, dtype=F32) * 2.0 / ROPE_DIM)
    inv_lane = jnp.tile(inv_freq, LANES // half)[None]
    sd = jax.ShapeDtypeStruct((t, LANES), F32)
    spec = pl.BlockSpec((tm, LANES), lambda i: (i, 0))
    return pl.pallas_call(
        _rope_table_kernel,
        grid=(t // tm,),
        in_specs=[pl.BlockSpec((tm, 1), lambda i: (i, 0)),
                  pl.BlockSpec((1, LANES), lambda i: (0, 0))],
        out_specs=[spec, spec, spec],
        out_shape=[sd, sd, sd],
        compiler_params=_cparams(("arbitrary",)),
        name="rope_tables",
    )(positions, inv_lane)


def _rot(x, c, s1, s2):
    half = ROPE_DIM // 2
    return x * c + pltpu.roll(x, LANES - half, 1) * s1 + pltpu.roll(x, half, 1) * s2


def _norm_matmul_rope_kernel(x_ref, g_ref, w_ref, c_ref, s1_ref, s2_ref, o_ref, *, n_rot):
    x = x_ref[...]
    ms = jnp.mean(x * x, axis=-1, keepdims=True)
    h = x * lax.rsqrt(ms + RMS_EPS) * g_ref[...]
    z = lax.dot_general(h.astype(BF16), w_ref[...], (((1,), (1,)), ((), ())),
                        preferred_element_type=F32)
    c = c_ref[...]
    s1 = s1_ref[...]
    s2 = s2_ref[...]
    for s in range(z.shape[1] // LANES):
        sl = slice(s * LANES, (s + 1) * LANES)
        zs = z[:, sl]
        if s < n_rot:
            zs = _rot(zs, c, s1, s2)
            if s < n_rot // 2:
                zs = zs * (HEAD_DIM ** -0.5)
        o_ref[:, sl] = zs


def _norm_matmul_rope(x, g, w, tabs, tm):
    t, d = x.shape
    n = w.shape[0]
    tab = pl.BlockSpec((tm, LANES), lambda i: (i, 0))
    return pl.pallas_call(
        functools.partial(_norm_matmul_rope_kernel, n_rot=2 * (n // 3) // LANES),
        grid=(t // tm,),
        in_specs=[pl.BlockSpec((tm, d), lambda i: (i, 0)),
                  pl.BlockSpec((1, d), lambda i: (0, 0)),
                  pl.BlockSpec((n, d), lambda i: (0, 0)), tab, tab, tab],
        out_specs=pl.BlockSpec((tm, n), lambda i: (i, 0)),
        out_shape=jax.ShapeDtypeStruct((t, n), F32),
        compiler_params=_cparams(("arbitrary",)),
        name="norm_in_proj_rope",
    )(x, g, w, *tabs)


def _attn_kernel(q_ref, k_ref, v_ref, o_ref, op_ref, lp_ref, *, seq_t, sb):
    jsb = pl.program_id(1)
    lane = lax.broadcasted_iota(jnp.int32, (1, LANES), 1)
    m_a = lane < HEAD_DIM
    for p, (window, dil) in enumerate(DILATED_PATTERNS):
        radius = window // (2 * dil)
        win = Q_BLOCK + 2 * radius
        seq = seq_t // dil
        nq = sb // (dil * Q_BLOCK)
        qi = lax.broadcasted_iota(jnp.int32, (Q_BLOCK, win), 0)
        kj = lax.broadcasted_iota(jnp.int32, (Q_BLOCK, win), 1)

        def rows(start, size, dil=dil):
            return pl.ds(start, size) if dil == 1 else pl.ds(start, size, stride=dil)

        def body(idx, carry, dil=dil, radius=radius, win=win, seq=seq, nq=nq, qi=qi, kj=kj,
                 rows=rows, p=p):
            r = idx // nq
            jj = idx % nq
            q_l0 = (jsb * nq + jj) * Q_BLOCK
            start_l = jnp.clip(q_l0 - radius, 0, seq - win)
            q_rows = rows(r + dil * Q_BLOCK * jj, Q_BLOCK)
            k_rows = rows(r + dil * start_l, win)
            q = q_ref[q_rows, :].astype(BF16)
            kw = k_ref[k_rows, :].astype(BF16)
            vw = v_ref[k_rows, :].astype(BF16)
            valid = jnp.abs((q_l0 + qi) - (start_l + kj)) <= radius
            one = jnp.ones((), BF16)
            halves = ((jnp.where(m_a, q, 0), jnp.where(m_a, vw, one)),
                      (jnp.where(m_a, 0, q), jnp.where(m_a, one, vw)))
            res, mx = [], []
            for qh, vh in halves:
                sc = lax.dot_general(qh, kw, (((1,), (1,)), ((), ())), preferred_element_type=F32)
                sc = jnp.where(valid, sc, NEG_BIG)
                m = jnp.max(sc, axis=-1, keepdims=True)
                pexp = jnp.exp(sc - m).astype(BF16)
                res.append(jnp.dot(pexp, vh, preferred_element_type=F32))
                mx.append(m)
            num = jnp.where(m_a, res[0], res[1])
            den = pltpu.roll(jnp.where(m_a, res[1], res[0]), HEAD_DIM, 1)
            op_ref[p, q_rows, :] = num / den
            lp_ref[p, q_rows, :] = jnp.where(m_a, mx[0], mx[1]) + jnp.log(den)
            return carry

        lax.fori_loop(0, dil * nq, body, 0, unroll=8)

    l1 = lp_ref[0]
    l2 = lp_ref[1]
    l3 = lp_ref[2]
    m = jnp.maximum(jnp.maximum(l1, l2), l3)
    e1 = jnp.exp(l1 - m)
    e2 = jnp.exp(l2 - m)
    e3 = jnp.exp(l3 - m)
    o_ref[...] = (e1 * op_ref[0] + e2 * op_ref[1] + e3 * op_ref[2]) / (e1 + e2 + e3)


def _dilated_attention(z_attn, d_attn, sb):
    t = z_attn.shape[0]
    n_hp = d_attn // LANES
    n_pat = len(DILATED_PATTERNS)
    return pl.pallas_call(
        functools.partial(_attn_kernel, seq_t=t, sb=sb),
        grid=(n_hp, t // sb),
        in_specs=[pl.BlockSpec((sb, LANES), lambda h, j: (j, h)),
                  pl.BlockSpec((t, LANES), lambda h, j: (0, n_hp + h)),
                  pl.BlockSpec((t, LANES), lambda h, j: (0, 2 * n_hp + h))],
        out_specs=pl.BlockSpec((sb, LANES), lambda h, j: (j, h)),
        out_shape=jax.ShapeDtypeStruct((t, d_attn), F32),
        scratch_shapes=[pltpu.VMEM((n_pat, sb, LANES), F32), pltpu.VMEM((n_pat, sb, LANES), F32)],
        compiler_params=_cparams(("arbitrary", "arbitrary")),
        name="dilated_attention",
    )(z_attn, z_attn, z_attn)


def _out_proj_kernel(x_ref, yf_ref, yb_ref, bonus_ref, gate_ref, lnw_ref, lnb_ref,
                     ya_ref, w_ref, out_ref, yr_ref, *, d_rwkv):
    ones_bd = _head_ones()
    for hp in range(d_rwkv // LANES):
        sl = pl.ds(hp * LANES, LANES)
        y = yf_ref[:, sl] + yb_ref[:, sl]
        mean = _head_sum(y, ones_bd) * (1.0 / HEAD_DIM)
        yc = y - mean
        var = _head_sum(yc * yc, ones_bd) * (1.0 / HEAD_DIM)
        yn = yc * lax.rsqrt(var + GN_EPS) * lnw_ref[:, sl] + lnb_ref[:, sl]
        yr_ref[:, sl] = ((yn + bonus_ref[:, sl]) * gate_ref[:, sl]).astype(BF16)
    acc = jnp.dot(yr_ref[...], w_ref[0:d_rwkv, :], preferred_element_type=F32)
    acc += jnp.dot(ya_ref[...].astype(BF16), w_ref[d_rwkv:, :], preferred_element_type=F32)
    out_ref[...] = x_ref[...] + acc


def _out_proj(x, yf, yb, bonus, gate, ln_w, ln_b, y_attn, w_out, tm):
    t, d = x.shape
    d_rwkv = yf.shape[1]
    d_attn = y_attn.shape[1]
    row = lambda n: pl.BlockSpec((tm, n), lambda i: (i, 0))
    const = lambda a: pl.BlockSpec(a.shape, lambda i: (0, 0))
    return pl.pallas_call(
        functools.partial(_out_proj_kernel, d_rwkv=d_rwkv),
        grid=(t // tm,),
        in_specs=[row(d)] + [row(d_rwkv)] * 4 + [const(ln_w), const(ln_b)]
                 + [row(d_attn), const(w_out)],
        out_specs=row(d),
        out_shape=jax.ShapeDtypeStruct((t, d), F32),
        scratch_shapes=[pltpu.VMEM((tm, d_rwkv), BF16)],
        compiler_params=_cparams(("arbitrary",)),
        name="merge_out_proj",
    )(x, yf, yb, bonus, gate, ln_w, ln_b, y_attn, w_out)


def _router_kernel(x_ref, g_ref, w_ref, b_ref, h_ref, idx_ref, wt_ref):
    x = x_ref[...]
    ms = jnp.mean(x * x, axis=-1, keepdims=True)
    h = x * lax.rsqrt(ms + RMS_EPS) * g_ref[...]
    h_ref[...] = h
    h_hi, h_lo = _split2(h)
    w_hi = w_ref[0]
    w_lo = w_ref[1]
    logits = (jnp.dot(h_hi, w_hi, preferred_element_type=F32)
              + jnp.dot(h_hi, w_lo, preferred_element_type=F32)
              + jnp.dot(h_lo, w_hi, preferred_element_type=F32)) + b_ref[...]
    lane = lax.broadcasted_iota(jnp.int32, logits.shape, 1).astype(F32)
    big = jnp.float32(LANES)
    is_g = lane < N_GROUPS
    gl = jnp.where(is_g, logits, NEG_BIG)
    gmax = jnp.max(gl, axis=-1, keepdims=True)
    gsel = jnp.min(jnp.where(is_g & (gl == gmax), lane, big), axis=-1, keepdims=True)
    g1 = 1.0 / jnp.sum(jnp.where(is_g, jnp.exp(gl - gmax), 0.0), axis=-1, keepdims=True)
    lo = N_GROUPS + gsel * EXPERTS_PER_GROUP
    in_grp = (lane >= lo) & (lane < lo + EXPERTS_PER_GROUP)
    el = jnp.where(in_grp, logits, NEG_BIG)
    v1 = jnp.max(el, axis=-1, keepdims=True)
    i1 = jnp.min(jnp.where(in_grp & (el == v1), lane, big), axis=-1, keepdims=True)
    rest = in_grp & (lane != i1)
    el2 = jnp.where(rest, logits, NEG_BIG)
    v2 = jnp.max(el2, axis=-1, keepdims=True)
    i2 = jnp.min(jnp.where(rest & (el2 == v2), lane, big), axis=-1, keepdims=True)
    e2 = jnp.exp(v2 - v1)
    ww1 = 1.0 / (1.0 + e2)
    ww2 = e2 / (1.0 + e2)
    idx = jnp.where(lane == 0, i1 - N_GROUPS, jnp.where(lane == 1, i2 - N_GROUPS, 0.0))
    idx_ref[...] = idx.astype(jnp.int32)
    wt_ref[...] = jnp.where(lane == 0, g1 * ww1, jnp.where(lane == 1, g1 * ww2, 0.0))


def _router(x, g, w_split, b_pad, tm):
    t, d = x.shape
    return pl.pallas_call(
        _router_kernel,
        grid=(t // tm,),
        in_specs=[pl.BlockSpec((tm, d), lambda i: (i, 0)),
                  pl.BlockSpec((1, d), lambda i: (0, 0)),
                  pl.BlockSpec((2, d, LANES), lambda i: (0, 0, 0)),
                  pl.BlockSpec((1, LANES), lambda i: (0, 0))],
        out_specs=[pl.BlockSpec((tm, d), lambda i: (i, 0)),
                   pl.BlockSpec((tm, LANES), lambda i: (i, 0)),
                   pl.BlockSpec((tm, LANES), lambda i: (i, 0))],
        out_shape=[jax.ShapeDtypeStruct((t, d), F32),
                   jax.ShapeDtypeStruct((t, LANES), jnp.int32),
                   jax.ShapeDtypeStruct((t, LANES), F32)],
        compiler_params=_cparams(("arbitrary",)),
        name="router",
    )(x, g, w_split, b_pad)


def _moe_kernel(te_ref, nt_ref, nv_ref, rows_ref, h_hbm, wg_ref, wu_ref, wd_ref, o_ref,
                xg_ref, xb_ref, acc_ref, sem, *, tm, nf):
    i = pl.program_id(0)
    f = pl.program_id(1)
    nt = nt_ref[0]
    active = i < nt
    slot = i % 2
    n_slab = xg_ref.shape[2] // LANES

    def issue_tile(tile, buf):
        def body(g, carry):
            for u in range(GATHER_UNROLL):
                j = g * GATHER_UNROLL + u
                src = h_hbm.at[pl.ds(rows_ref[tile * tm + j], 1), :]
                pltpu.make_async_copy(src, xg_ref.at[buf, pl.ds(j, 1), :],
                                      sem.at[buf]).start(priority=1)
            return carry
        lax.fori_loop(0, nv_ref[tile] // GATHER_UNROLL, body, 0)

    def wait_tile(tile, buf):
        def body(g, carry):
            pltpu.make_async_copy(h_hbm.at[pl.ds(0, GATHER_CHUNK), :],
                                  xg_ref.at[buf, pl.ds(0, GATHER_CHUNK), :], sem.at[buf]).wait()
            return carry
        lax.fori_loop(0, nv_ref[tile] // GATHER_CHUNK, body, 0)

    @pl.when((i == 0) & (f == 0))
    def _():
        xg_ref[...] = jnp.zeros_like(xg_ref)
        issue_tile(0, 0)

    @pl.when(active & (f == 0))
    def _():
        wait_tile(i, slot)
        xb_ref[...] = xg_ref[slot].astype(BF16)
        issue_tile(i + 1, 1 - slot)

    @pl.when(active)
    def _():
        xb = xb_ref[...]
        gate = jnp.dot(xb, wg_ref[...].astype(BF16), preferred_element_type=F32)
        up = jnp.dot(xb, wu_ref[...].astype(BF16), preferred_element_type=F32)
        hid = (gate * _sigmoid(gate)) * up
        part = jnp.dot(hid.astype(BF16), wd_ref[...].astype(BF16), preferred_element_type=F32)

        @pl.when(f == 0)
        def _():
            acc_ref[...] = part

        @pl.when((f != 0) & (f != nf - 1))
        def _():
            acc_ref[...] += part

        @pl.when(f == nf - 1)
        def _():
            for c in range(n_slab):
                sl = slice(c * LANES, (c + 1) * LANES)
                o_ref[pl.ds(c, tm, stride=n_slab), :] = acc_ref[:, sl] + part[:, sl]

    @pl.when(jnp.logical_not(active) & (f == 0))
    def _():
        o_ref[...] = jnp.zeros_like(o_ref)


def _moe(h, tile_expert, n_tiles, n_valid, rows, w_gate, w_up, w_down, tm, tf):
    t, d = h.shape
    n_e, _, d_e = w_gate.shape
    n_slab = d // LANES
    nt_max = tile_expert.shape[0]
    nf = d_e // tf
    assert nf >= 2 and tm % GATHER_CHUNK == 0 and GATHER_CHUNK % GATHER_UNROLL == 0
    assert rows.shape[0] == (nt_max + 1) * tm and n_valid.shape[0] == nt_max + 1

    def fidx(i, f, nt):
        return jnp.where(i < nt[0], f, nf - 1)

    grid_spec = pltpu.PrefetchScalarGridSpec(
        num_scalar_prefetch=4,
        grid=(nt_max, d_e // tf),
        in_specs=[pl.BlockSpec(memory_space=pl.ANY),
                  pl.BlockSpec((None, d, tf), lambda i, f, te, nt, nv, rw: (te[i], 0, fidx(i, f, nt))),
                  pl.BlockSpec((None, d, tf), lambda i, f, te, nt, nv, rw: (te[i], 0, fidx(i, f, nt))),
                  pl.BlockSpec((None, tf, d), lambda i, f, te, nt, nv, rw: (te[i], fidx(i, f, nt), 0))],
        out_specs=pl.BlockSpec((tm * n_slab, LANES), lambda i, f, te, nt, nv, rw: (i, 0)),
        scratch_shapes=[pltpu.VMEM((2, tm, d), F32), pltpu.VMEM((tm, d), BF16),
                        pltpu.VMEM((tm, d), F32), pltpu.SemaphoreType.DMA((2,))],
    )
    return pl.pallas_call(
        functools.partial(_moe_kernel, tm=tm, nf=nf),
        grid_spec=grid_spec,
        out_shape=jax.ShapeDtypeStruct((nt_max * tm * n_slab, LANES), F32),
        compiler_params=_cparams(("arbitrary", "arbitrary")),
        name="grouped_expert_mlp",
    )(tile_expert, n_tiles, n_valid, rows, h, w_gate, w_up, w_down)


def _final_kernel(s0_ref, s1_ref, x_ref, wt_ref, g_ref, y_hbm, o_ref, yg_ref, sem, *, tm, nt):
    i = pl.program_id(0)
    slot = i % 2
    d = x_ref.shape[1]
    n_slab = d // LANES

    def issue(tile, buf):
        def body(j, carry):
            for k, s_ref in enumerate((s0_ref, s1_ref)):
                src = y_hbm.at[pl.ds(pl.multiple_of(s_ref[tile * tm + j] * n_slab, n_slab), n_slab), :]
                dst = yg_ref.at[buf, k, pl.ds(pl.multiple_of(j * n_slab, n_slab), n_slab), :]
                pltpu.make_async_copy(src, dst, sem.at[buf]).start(priority=k)
            return carry
        lax.fori_loop(0, tm, body, 0, unroll=8)

    @pl.when(i == 0)
    def _():
        issue(0, 0)

    @pl.when(i + 1 < nt)
    def _():
        issue(i + 1, 1 - slot)

    for k in range(2):
        pltpu.make_async_copy(y_hbm.at[pl.ds(0, tm * n_slab), :], yg_ref.at[slot, k],
                              sem.at[slot]).wait()

    wt = wt_ref[...]
    w0 = wt[:, 0:1]
    w1 = wt[:, 1:2]
    ss = jnp.zeros((tm, 1), F32)
    for c in range(n_slab):
        sl = slice(c * LANES, (c + 1) * LANES)
        rows = pl.ds(c, tm, stride=n_slab)
        xs = x_ref[:, sl] + w0 * yg_ref[slot, 0, rows, :] + w1 * yg_ref[slot, 1, rows, :]
        o_ref[:, sl] = xs
        ss = ss + jnp.sum(xs * xs, axis=-1, keepdims=True)
    o_ref[...] = o_ref[...] * lax.rsqrt(ss * (1.0 / d) + RMS_EPS) * g_ref[...]


def _final(slot0, slot1, x, wts, g, y_sorted, tm):
    t, d = x.shape
    grid_spec = pltpu.PrefetchScalarGridSpec(
        num_scalar_prefetch=2,
        grid=(t // tm,),
        in_specs=[pl.BlockSpec((tm, d), lambda i, s0, s1: (i, 0)),
                  pl.BlockSpec((tm, LANES), lambda i, s0, s1: (i, 0)),
                  pl.BlockSpec((1, d), lambda i, s0, s1: (0, 0)),
                  pl.BlockSpec(memory_space=pl.ANY)],
        out_specs=pl.BlockSpec((tm, d), lambda i, s0, s1: (i, 0)),
        scratch_shapes=[pltpu.VMEM((2, 2, tm * (d // LANES), LANES), F32),
                        pltpu.SemaphoreType.DMA((2,))],
    )
    return pl.pallas_call(
        functools.partial(_final_kernel, tm=tm, nt=t // tm),
        grid_spec=grid_spec,
        out_shape=jax.ShapeDtypeStruct((t, d), F32),
        compiler_params=_cparams(("arbitrary",)),
        name="combine_final_norm",
    )(slot0, slot1, x, wts, g, y_sorted)


def _dispatch_plan(experts, tm):
    t = experts.shape[0]
    ef = experts.T.reshape(-1)
    onehot = (ef[:, None] == jnp.arange(N_EXPERTS, dtype=jnp.int32)[None, :]).astype(jnp.int32)
    csum = jnp.cumsum(onehot, axis=0)
    rank = jnp.take_along_axis(csum, ef[:, None], axis=1)[:, 0] - 1
    counts = csum[-1]
    tiles_e = (counts + tm - 1) // tm
    tile_end = jnp.cumsum(tiles_e)
    tile_start = tile_end - tiles_e
    n_tiles = tile_end[-1]
    nt_max = (2 * t) // tm + N_EXPERTS
    slot = tile_start[ef] * tm + rank
    tid = jnp.minimum(jnp.arange(nt_max, dtype=jnp.int32), n_tiles - 1)
    tile_expert = jnp.sum((tile_end[None, :] <= tid[:, None]).astype(jnp.int32), axis=1)
    tok = jnp.tile(jnp.arange(t, dtype=jnp.int32), 2)
    rows = jnp.zeros(((nt_max + 1) * tm,), jnp.int32).at[slot].set(tok)
    tix = jnp.arange(nt_max + 1, dtype=jnp.int32)
    te_all = jnp.concatenate([tile_expert, tile_expert[-1:]])
    left = counts[te_all] - (tix - tile_start[te_all]) * tm
    n_valid = jnp.where(tix < n_tiles, jnp.clip(left, 0, tm), 0)
    n_valid = (n_valid + GATHER_CHUNK - 1) // GATHER_CHUNK * GATHER_CHUNK
    return (slot[:t], slot[t:], rows, tile_expert, n_tiles.reshape(1).astype(jnp.int32),
            n_valid.astype(jnp.int32))


def kernel(x, positions, norm_mix, w_in, mu_shift, w0, w2, a0, a2, g2, k_k, k_a, r_k, ln_x_w, ln_x_b, w_out, norm_ffn, router_group_w, router_group_b, router_expert_w, router_expert_b, w_gate, w_up, w_down, norm_final):
    bsz, seq, d = x.shape
    assert bsz == 1
    depth = w_in.shape[0]
    d_rwkv = k_k.shape[1]
    d_attn = w_out.shape[1] - d_rwkv
    rwkv_cols = mu_shift.shape[2]
    nr = -(-rwkv_cols // (2 * LANES)) * (2 * LANES)
    assert 3 * d_rwkv + 2 * LANES + GATE_LORA == rwkv_cols and nr == 3 * d_rwkv + 4 * LANES
    tm_moe = 576
    tf_moe = 512
    sb_attn = max(dil for _, dil in DILATED_PATTERNS) * Q_BLOCK
    assert seq % sb_attn == 0

    xt = x.reshape(seq, d)
    tabs = _rope_tables(positions.reshape(seq, 1), 512)
    for l in range(depth):
        w_t = jnp.swapaxes(w_in[l], 0, 1)
        w_r = jnp.pad(w_t[:rwkv_cols], ((0, nr - rwkv_cols), (0, 0))).astype(BF16)
        w_a = w_t[rwkv_cols:].astype(BF16)
        mu = jnp.pad(mu_shift[l], ((0, 0), (0, nr - rwkv_cols)))
        zl = jnp.zeros((DECAY_LORA, d_rwkv), F32)
        w2p = jnp.stack([jnp.concatenate([w2[l, 0], zl]), jnp.concatenate([zl, w2[l, 1]])]).astype(BF16)
        a2p = jnp.stack([jnp.concatenate([a2[l, 0], zl]), jnp.concatenate([zl, a2[l, 1]])]).astype(BF16)
        g2p = jnp.pad(g2[l], ((0, 2 * LANES - GATE_LORA), (0, 0))).astype(BF16)
        rk = r_k[l].reshape(1, d_rwkv)

        z_r = _norm_matmul_shift(xt, norm_mix[l][None], w_r, mu, 256)
        z_a = _norm_matmul_rope(xt, norm_mix[l][None], w_a, tabs, 256)
        yf, yb, bonus, gate = _rwkv_mixer(z_r, w0[l], a0[l], w2p, a2p, g2p,
                                          k_k[l][None], k_a[l][None], rk, d_rwkv)
        y_attn = _dilated_attention(z_a, d_attn, sb_attn)
        x2 = _out_proj(xt, yf, yb, bonus, gate, ln_x_w[l][None], ln_x_b[l][None],
                       y_attn, w_out[l].astype(BF16), 256)

        w_rt = jnp.concatenate(
            [router_group_w[l], router_expert_w[l].transpose(1, 0, 2).reshape(d, N_EXPERTS)], axis=1)
        w_rt = jnp.pad(w_rt, ((0, 0), (0, LANES - w_rt.shape[1])))
        rt_hi = w_rt.astype(BF16)
        rt_lo = (w_rt - rt_hi.astype(F32)).astype(BF16)
        b_rt = jnp.concatenate([router_group_b[l], router_expert_b[l].reshape(-1)])
        b_rt = jnp.pad(b_rt, (0, LANES - b_rt.shape[0]))[None]
        h, idx, wts = _router(x2, norm_ffn[l][None], jnp.stack([rt_hi, rt_lo]), b_rt, 256)

        slot0, slot1, rows, tile_expert, n_tiles, n_valid = _dispatch_plan(idx[:, :2], tm_moe)
        y_sorted = _moe(h, tile_expert, n_tiles, n_valid, rows, w_gate[l], w_up[l], w_down[l],
                        tm_moe, tf_moe)
        is_last = l == depth - 1
        assert is_last, "the combine kernel applies the final norm; depth must be 1"
        xt = _final(slot0, slot1, x2, wts, norm_final[None], y_sorted, 256)
    return xt.reshape(bsz, seq, d)
```

```python
import functools

import jax
import jax.numpy as jnp
from jax import lax
from jax.experimental import pallas as pl
from jax.experimental.pallas import tpu as pltpu

HEAD_DIM = 64
LANES = 128
SUBLANES = 8
DECAY_LORA = 64
ICLR_LORA = 64
GATE_LORA = 160
GN_EPS = 64e-5
RMS_EPS = 1e-6
ROPE_THETA = 500000.0
ROPE_DIM = HEAD_DIM // 4
DILATED_PATTERNS = ((128, 1), (512, 4), (2048, 16))
Q_BLOCK = 128
N_GROUPS = 4
EXPERTS_PER_GROUP = 8
N_EXPERTS = N_GROUPS * EXPERTS_PER_GROUP
NEG_BIG = -1e30
CHUNK = 64
GATHER_CHUNK = 64
GATHER_UNROLL = 8
VMEM_LIMIT = 56 * 1024 * 1024

BF16 = jnp.bfloat16
F32 = jnp.float32


def _cparams(sem):
    return pltpu.CompilerParams(dimension_semantics=sem, vmem_limit_bytes=VMEM_LIMIT)


def _dot(a, b):
    return jnp.dot(a.astype(BF16), b.astype(BF16), preferred_element_type=F32)


def _dot_nt(a, b):
    return lax.dot_general(a.astype(BF16), b.astype(BF16), (((1,), (1,)), ((), ())),
                           preferred_element_type=F32)


def _dot_tn(a, b):
    return lax.dot_general(a.astype(BF16), b.astype(BF16), (((0,), (0,)), ((), ())),
                           preferred_element_type=F32)


def _split2(x):
    hi = x.astype(BF16)
    lo = (x - hi.astype(F32)).astype(BF16)
    return hi, lo


def _split3(x):
    hi = x.astype(BF16)
    r1 = x - hi.astype(F32)
    mid = r1.astype(BF16)
    lo = (r1 - mid.astype(F32)).astype(BF16)
    return hi, mid, lo


def _head_ones():
    i = lax.broadcasted_iota(jnp.int32, (LANES, LANES), 0) // HEAD_DIM
    j = lax.broadcasted_iota(jnp.int32, (LANES, LANES), 1) // HEAD_DIM
    return (i == j).astype(BF16)


def _head_sum(x, ones_bd):
    hi, lo = _split2(x)
    return (jnp.dot(hi, ones_bd, preferred_element_type=F32)
            + jnp.dot(lo, ones_bd, preferred_element_type=F32))


def _sigmoid(x):
    return 1.0 / (1.0 + jnp.exp(-x))


def _norm_matmul_shift_kernel(x_ref, xp_ref, xn_ref, g_ref, w_ref, mu_ref, o_ref, *, nt):
    i = pl.program_id(0)
    tm = x_ref.shape[0]
    halo = xp_ref.shape[0]
    x = jnp.concatenate([xp_ref[...], x_ref[...], xn_ref[...]], axis=0)
    ms = jnp.mean(x * x, axis=-1, keepdims=True)
    h = x * lax.rsqrt(ms + RMS_EPS) * g_ref[...]
    z = lax.dot_general(h.astype(BF16), w_ref[...], (((1,), (1,)), ((), ())),
                        preferred_element_type=F32)
    rows = tm + 2 * halo
    zc = z[halo:halo + tm]
    z_prev = pltpu.roll(z, 1, 0)[halo:halo + tm]
    z_next = pltpu.roll(z, rows - 1, 0)[halo:halo + tm]
    row = lax.broadcasted_iota(jnp.int32, (tm, 1), 0)
    z_prev = jnp.where((i == 0) & (row == 0), 0.0, z_prev)
    z_next = jnp.where((i == nt - 1) & (row == tm - 1), 0.0, z_next)
    o_ref[...] = zc + mu_ref[0:1, :] * (z_prev - zc) + mu_ref[1:2, :] * (z_next - zc)


def _norm_matmul_shift(x, g, w, mu, tm):
    t, d = x.shape
    n = w.shape[0]
    nt = t // tm
    hb = tm // SUBLANES
    last = t // SUBLANES - 1
    return pl.pallas_call(
        functools.partial(_norm_matmul_shift_kernel, nt=nt),
        grid=(nt,),
        in_specs=[pl.BlockSpec((tm, d), lambda i: (i, 0)),
                  pl.BlockSpec((SUBLANES, d), lambda i: (jnp.maximum(i * hb - 1, 0), 0)),
                  pl.BlockSpec((SUBLANES, d), lambda i: (jnp.minimum((i + 1) * hb, last), 0)),
                  pl.BlockSpec((1, d), lambda i: (0, 0)),
                  pl.BlockSpec((n, d), lambda i: (0, 0)),
                  pl.BlockSpec((2, n), lambda i: (0, 0))],
        out_specs=pl.BlockSpec((tm, n), lambda i: (i, 0)),
        out_shape=jax.ShapeDtypeStruct((t, n), F32),
        compiler_params=_cparams(("arbitrary",)),
        name="norm_in_proj_shift",
    )(x, x, x, g, w, mu)


def _rwkv_kernel(zf_ref, zb_ref,
                 w0_ref, a0_ref, w2_ref, a2_ref, g2_ref, kk_ref, ka_ref, rk_ref,
                 yf_ref, yb_ref, bonus_ref, gate_ref,
                 prep_ref, h_ref, *, nc, d_rwkv):
    c = pl.program_id(0)
    C = CHUNK
    n_hp = d_rwkv // LANES

    @pl.when(c == 0)
    def _():
        h_ref[...] = jnp.zeros_like(h_ref)

    zs = (zf_ref, zb_ref)
    ones_bd = _head_ones()
    lane = lax.broadcasted_iota(jnp.int32, (1, LANES), 1)
    m_a = lane < HEAD_DIM
    ti = lax.broadcasted_iota(jnp.int32, (C, C), 0)
    tj = lax.broadcasted_iota(jnp.int32, (C, C), 1)
    bi = lax.broadcasted_iota(jnp.int32, (2 * C, 2 * C), 0)
    bj = lax.broadcasted_iota(jnp.int32, (2 * C, 2 * C), 1)
    same_blk = (bi // C) == (bj // C)
    eye = bi == bj
    cum_mat = ((tj <= ti).astype(BF16), (tj >= ti).astype(BF16))
    strict = (same_blk & ((bj % C) < (bi % C)), same_blk & ((bj % C) > (bi % C)))
    incl = (same_blk & ((bj % C) <= (bi % C)), same_blk & ((bj % C) >= (bi % C)))
    last_row = (C - 1, 0)
    off_wl = 3 * d_rwkv
    off_al = off_wl + LANES
    off_gl = off_al + LANES

    def to_rows(x):
        return jnp.concatenate([x[:, s * LANES:(s + 1) * LANES] for s in range(n_hp)], axis=0)

    def to_lanes(x):
        return jnp.concatenate([x[s * C:(s + 1) * C] for s in range(n_hp)], axis=1)

    def bd(x):
        return jnp.concatenate([jnp.where(m_a, x, 0), jnp.where(m_a, 0, x)], axis=0)

    def unbd(x):
        return x[0:C] + x[C:2 * C]

    w_c = []
    for d in range(2):
        r = zs[d][:, 0:d_rwkv]
        k = zs[d][:, d_rwkv:2 * d_rwkv]
        kkr = k * kk_ref[...]
        sums = [to_rows(kkr * kkr)]
        if d == 0:
            sums.append(to_rows(r * k * rk_ref[...]))
        hs = _head_sum(jnp.concatenate(sums, axis=0), ones_bd)
        kk = kkr * lax.rsqrt(to_lanes(hs[0:n_hp * C]) + 1e-12)
        if d == 0:
            bonus_ref[...] = to_lanes(hs[n_hp * C:2 * n_hp * C]) * zs[0][:, 2 * d_rwkv:3 * d_rwkv]
            gate_ref[...] = _dot(_sigmoid(zs[0][:, off_gl:off_gl + 2 * LANES]), g2_ref[...])
        u = w0_ref[d:d + 1, :] + _dot(jnp.tanh(zs[d][:, off_wl:off_wl + LANES]), w2_ref[d])
        w_log = -(jnp.maximum(-u, 0.0) + jnp.log1p(jnp.exp(-jnp.abs(u)))) - 0.5
        logd = -jnp.exp(w_log)
        a = _sigmoid(a0_ref[d:d + 1, :] + _dot(zs[d][:, off_al:off_al + LANES], a2_ref[d]))
        kd = k * (1.0 + (a - 1.0) * ka_ref[...])
        b = kk * a
        l_hi, l_mid, l_lo = _split3(logd)
        cm = cum_mat[d]
        cum = (jnp.dot(cm, l_hi, preferred_element_type=F32)
               + jnp.dot(cm, l_mid, preferred_element_type=F32)
               + jnp.dot(cm, l_lo, preferred_element_type=F32))
        cum_c = cum[last_row[d]:last_row[d] + 1, :]
        w_inv = jnp.exp(-cum)
        w_c.append(jnp.exp(cum_c))
        w_end = w_c[d] * w_inv
        prep_ref[d, 0] = -kk * jnp.exp(cum - logd)
        prep_ref[d, 1] = r * jnp.exp(cum)
        prep_ref[d, 2] = b * w_inv
        prep_ref[d, 3] = kd * w_inv
        prep_ref[d, 4] = b * w_end
        prep_ref[d, 5] = kd * w_end

    blocks = [(d, hp) for hp in range(n_hp) for d in range(2)]

    def slab(d, i, hp):
        return prep_ref[d, i, :, hp * LANES:(hp + 1) * LANES]

    incl2 = tuple(jnp.concatenate([m, m], axis=1) for m in incl)
    eye_f = eye.astype(F32)
    xs, aks, lows, z0s, vbds = [], [], [], [], []
    for d, hp in blocks:
        at = bd(slab(d, 0, hp).astype(BF16))
        rt = bd(slab(d, 1, hp).astype(BF16))
        bt = slab(d, 2, hp).astype(BF16)
        kt = slab(d, 3, hp).astype(BF16)
        lhs = jnp.concatenate([at, rt], axis=0)
        rhs = jnp.concatenate([bt, bt, kt, kt], axis=0)
        g = lax.dot_general(lhs, rhs, (((1,), (1,)), ((), ())), preferred_element_type=F32)
        xs.append(jnp.where(strict[d], g[0:2 * C, 0:2 * C], 0.0))
        aks.append(jnp.where(strict[d], g[0:2 * C, 2 * C:4 * C], 0.0).astype(BF16))
        lows.append(jnp.where(incl2[d], g[2 * C:4 * C, :], 0.0).astype(BF16))
    for i, (d, hp) in enumerate(blocks):
        vbd = bd(zs[d][:, 2 * d_rwkv + hp * LANES:2 * d_rwkv + (hp + 1) * LANES].astype(BF16))
        vbds.append(vbd)
        akv = jnp.dot(aks[i], vbd, preferred_element_type=F32)
        z0s.append(jnp.concatenate([bd(slab(d, 0, hp).astype(BF16)), akv.astype(BF16)], axis=1))

    n_sq = C.bit_length() - 1
    ts = [eye_f + x for x in xs]
    ps = [x.astype(BF16) for x in xs]
    for j in range(1, n_sq):
        last = j == n_sq - 1
        for i in range(len(blocks)):
            if j == 1:
                ps[i] = jnp.dot(ps[i], ps[i], preferred_element_type=F32).astype(BF16)
            tb = ts[i].astype(BF16)
            if last:
                ts[i] = ts[i] + jnp.dot(ps[i], tb, preferred_element_type=F32)
            else:
                res = jnp.dot(ps[i], jnp.concatenate([ps[i], tb], axis=1), preferred_element_type=F32)
                ts[i] = ts[i] + res[:, LANES:2 * LANES]
                ps[i] = res[:, 0:LANES].astype(BF16)

    wts = []
    for i in range(len(blocks)):
        zz = jnp.dot(ts[i].astype(BF16), z0s[i], preferred_element_type=F32)
        bottom = jnp.concatenate([jnp.zeros((2 * C, LANES), BF16), vbds[i]], axis=1)
        wts.append(jnp.concatenate([zz.astype(BF16), bottom], axis=0))
    outs = []
    for i, (d, hp) in enumerate(blocks):
        bb_t = bd(slab(d, 4, hp)).T.astype(BF16)
        kb_t = bd(slab(d, 5, hp)).T.astype(BF16)
        lhs = jnp.concatenate([lows[i], jnp.concatenate([bb_t, kb_t], axis=1)], axis=0)
        outs.append(jnp.dot(lhs, wts[i], preferred_element_type=F32))
    for i, (d, hp) in enumerate(blocks):
        sl = pl.ds(hp * LANES, LANES)
        o = outs[i]
        r_hat = bd(slab(d, 1, hp)) + o[0:2 * C, 0:LANES]
        p_mat = eye_f * w_c[d][:, hp * LANES:(hp + 1) * LANES] + o[2 * C:4 * C, 0:LANES]
        lhs = jnp.concatenate([r_hat, p_mat], axis=0).astype(BF16)
        res = jnp.dot(lhs, h_ref[d, hp].astype(BF16), preferred_element_type=F32)
        y = unbd(res[0:2 * C] + o[0:2 * C, LANES:2 * LANES])
        h_ref[d, hp] = res[2 * C:4 * C] + o[2 * C:4 * C, LANES:2 * LANES]
        if d == 0:
            yf_ref[:, sl] = y
        else:
            yb_ref[:, sl] = y


def _rwkv_mixer(z, w0, a0, w2p, a2p, g2p, k_k, k_a, r_k, d_rwkv):
    t, nr = z.shape
    C = CHUNK
    nc = t // C
    n_hp = d_rwkv // LANES

    def full(a):
        nd = a.ndim
        return pl.BlockSpec(a.shape, lambda c: (0,) * nd)

    out_spec_f = pl.BlockSpec((C, d_rwkv), lambda c: (c, 0))
    out_spec_b = pl.BlockSpec((C, d_rwkv), lambda c: (nc - 1 - c, 0))
    out_sd = jax.ShapeDtypeStruct((t, d_rwkv), F32)
    params = (w0, a0, w2p, a2p, g2p, k_k, k_a, r_k)
    return pl.pallas_call(
        functools.partial(_rwkv_kernel, nc=nc, d_rwkv=d_rwkv),
        grid=(nc,),
        in_specs=[pl.BlockSpec((C, nr), lambda c: (c, 0)),
                  pl.BlockSpec((C, nr), lambda c: (nc - 1 - c, 0))]
                 + [full(p) for p in params],
        out_specs=[out_spec_f, out_spec_b, out_spec_f, out_spec_f],
        out_shape=[out_sd, out_sd, out_sd, out_sd],
        scratch_shapes=[pltpu.VMEM((2, 6, C, d_rwkv), F32),
                        pltpu.VMEM((2, n_hp, LANES, LANES), F32)],
        compiler_params=_cparams(("arbitrary",)),
        name="rwkv7_chunk_scan",
    )(z, z, *params)


def _rope_coeffs(pos, inv_lane):
    half = ROPE_DIM // 2
    j = lax.broadcasted_iota(jnp.int32, (1, LANES), 1) % HEAD_DIM
    ang = pos.astype(F32) * inv_lane
    cs = jnp.cos(ang)
    sn = jnp.sin(ang)
    c = jnp.where(j < ROPE_DIM, cs, 1.0)
    s1 = jnp.where(j < half, -sn, 0.0)
    s2 = jnp.where((j >= half) & (j < ROPE_DIM), sn, 0.0)
    return c, s1, s2


def _rot(x, c, s1, s2):
    half = ROPE_DIM // 2
    return x * c + pltpu.roll(x, LANES - half, 1) * s1 + pltpu.roll(x, half, 1) * s2


def _norm_matmul_rope_kernel(x_ref, g_ref, w_ref, pos_ref, inv_ref, o_ref, *, n_rot):
    x = x_ref[...]
    ms = jnp.mean(x * x, axis=-1, keepdims=True)
    h = x * lax.rsqrt(ms + RMS_EPS) * g_ref[...]
    z = lax.dot_general(h.astype(BF16), w_ref[...], (((1,), (1,)), ((), ())),
                        preferred_element_type=F32)
    c, s1, s2 = _rope_coeffs(pos_ref[...], inv_ref[...])
    for s in range(z.shape[1] // LANES):
        sl = slice(s * LANES, (s + 1) * LANES)
        zs = z[:, sl]
        if s < n_rot:
            zs = _rot(zs, c, s1, s2)
            if s < n_rot // 2:
                zs = zs * (HEAD_DIM ** -0.5)
        o_ref[:, sl] = zs


def _norm_matmul_rope(x, g, w, positions, tm):
    t, d = x.shape
    n = w.shape[0]
    half = ROPE_DIM // 2
    inv_freq = jnp.power(ROPE_THETA, -jnp.arange(half, dtype=F32) * 2.0 / ROPE_DIM)
    inv_lane = jnp.tile(inv_freq, LANES // half)[None]
    return pl.pallas_call(
        functools.partial(_norm_matmul_rope_kernel, n_rot=2 * (n // 3) // LANES),
        grid=(t // tm,),
        in_specs=[pl.BlockSpec((tm, d), lambda i: (i, 0)),
                  pl.BlockSpec((1, d), lambda i: (0, 0)),
                  pl.BlockSpec((n, d), lambda i: (0, 0)),
                  pl.BlockSpec((tm, 1), lambda i: (i, 0)),
                  pl.BlockSpec((1, LANES), lambda i: (0, 0))],
        out_specs=pl.BlockSpec((tm, n), lambda i: (i, 0)),
        out_shape=jax.ShapeDtypeStruct((t, n), F32),
        compiler_params=_cparams(("arbitrary",)),
        name="norm_in_proj_rope",
    )(x, g, w, positions, inv_lane)


def _attn_kernel(q_ref, k_ref, v_ref, o_ref, op_ref, lp_ref, *, seq_t, sb):
    jsb = pl.program_id(1)
    lane = lax.broadcasted_iota(jnp.int32, (1, LANES), 1)
    m_a = lane < HEAD_DIM
    for p, (window, dil) in enumerate(DILATED_PATTERNS):
        radius = window // (2 * dil)
        win = Q_BLOCK + 2 * radius
        seq = seq_t // dil
        nq = sb // (dil * Q_BLOCK)
        qi = lax.broadcasted_iota(jnp.int32, (Q_BLOCK, win), 0)
        kj = lax.broadcasted_iota(jnp.int32, (Q_BLOCK, win), 1)

        def rows(start, size, dil=dil):
            return pl.ds(start, size) if dil == 1 else pl.ds(start, size, stride=dil)

        def body(idx, carry, dil=dil, radius=radius, win=win, seq=seq, nq=nq, qi=qi, kj=kj,
                 rows=rows, p=p):
            r = idx // nq
            jj = idx % nq
            q_l0 = (jsb * nq + jj) * Q_BLOCK
            start_l = jnp.clip(q_l0 - radius, 0, seq - win)
            q_rows = rows(r + dil * Q_BLOCK * jj, Q_BLOCK)
            k_rows = rows(r + dil * start_l, win)
            q = q_ref[q_rows, :].astype(BF16)
            kw = k_ref[k_rows, :].astype(BF16)
            vw = v_ref[k_rows, :].astype(BF16)
            valid = jnp.abs((q_l0 + qi) - (start_l + kj)) <= radius
            one = jnp.ones((), BF16)
            halves = ((jnp.where(m_a, q, 0), jnp.where(m_a, vw, one)),
                      (jnp.where(m_a, 0, q), jnp.where(m_a, one, vw)))
            res, mx = [], []
            for qh, vh in halves:
                sc = lax.dot_general(qh, kw, (((1,), (1,)), ((), ())), preferred_element_type=F32)
                sc = jnp.where(valid, sc, NEG_BIG)
                m = jnp.max(sc, axis=-1, keepdims=True)
                pexp = jnp.exp(sc - m).astype(BF16)
                res.append(jnp.dot(pexp, vh, preferred_element_type=F32))
                mx.append(m)
            num = jnp.where(m_a, res[0], res[1])
            den = pltpu.roll(jnp.where(m_a, res[1], res[0]), HEAD_DIM, 1)
            op_ref[p, q_rows, :] = num / den
            lp_ref[p, q_rows, :] = jnp.where(m_a, mx[0], mx[1]) + jnp.log(den)
            return carry

        lax.fori_loop(0, dil * nq, body, 0, unroll=8)

    l1 = lp_ref[0]
    l2 = lp_ref[1]
    l3 = lp_ref[2]
    m = jnp.maximum(jnp.maximum(l1, l2), l3)
    e1 = jnp.exp(l1 - m)
    e2 = jnp.exp(l2 - m)
    e3 = jnp.exp(l3 - m)
    o_ref[...] = (e1 * op_ref[0] + e2 * op_ref[1] + e3 * op_ref[2]) / (e1 + e2 + e3)


def _dilated_attention(z_attn, d_attn, sb):
    t = z_attn.shape[0]
    n_hp = d_attn // LANES
    n_pat = len(DILATED_PATTERNS)
    return pl.pallas_call(
        functools.partial(_attn_kernel, seq_t=t, sb=sb),
        grid=(n_hp, t // sb),
        in_specs=[pl.BlockSpec((sb, LANES), lambda h, j: (j, h)),
                  pl.BlockSpec((t, LANES), lambda h, j: (0, n_hp + h)),
                  pl.BlockSpec((t, LANES), lambda h, j: (0, 2 * n_hp + h))],
        out_specs=pl.BlockSpec((sb, LANES), lambda h, j: (j, h)),
        out_shape=jax.ShapeDtypeStruct((t, d_attn), F32),
        scratch_shapes=[pltpu.VMEM((n_pat, sb, LANES), F32), pltpu.VMEM((n_pat, sb, LANES), F32)],
        compiler_params=_cparams(("arbitrary", "arbitrary")),
        name="dilated_attention",
    )(z_attn, z_attn, z_attn)


def _out_proj_kernel(x_ref, yf_ref, yb_ref, bonus_ref, gate_ref, lnw_ref, lnb_ref,
                     ya_ref, w_ref, out_ref, yr_ref, *, d_rwkv):
    ones_bd = _head_ones()
    for hp in range(d_rwkv // LANES):
        sl = pl.ds(hp * LANES, LANES)
        y = yf_ref[:, sl] + yb_ref[:, sl]
        mean = _head_sum(y, ones_bd) * (1.0 / HEAD_DIM)
        yc = y - mean
        var = _head_sum(yc * yc, ones_bd) * (1.0 / HEAD_DIM)
        yn = yc * lax.rsqrt(var + GN_EPS) * lnw_ref[:, sl] + lnb_ref[:, sl]
        yr_ref[:, sl] = ((yn + bonus_ref[:, sl]) * gate_ref[:, sl]).astype(BF16)
    acc = jnp.dot(yr_ref[...], w_ref[0:d_rwkv, :], preferred_element_type=F32)
    acc += jnp.dot(ya_ref[...].astype(BF16), w_ref[d_rwkv:, :], preferred_element_type=F32)
    out_ref[...] = x_ref[...] + acc


def _out_proj(x, yf, yb, bonus, gate, ln_w, ln_b, y_attn, w_out, tm):
    t, d = x.shape
    d_rwkv = yf.shape[1]
    d_attn = y_attn.shape[1]
    row = lambda n: pl.BlockSpec((tm, n), lambda i: (i, 0))
    const = lambda a: pl.BlockSpec(a.shape, lambda i: (0, 0))
    return pl.pallas_call(
        functools.partial(_out_proj_kernel, d_rwkv=d_rwkv),
        grid=(t // tm,),
        in_specs=[row(d)] + [row(d_rwkv)] * 4 + [const(ln_w), const(ln_b)]
                 + [row(d_attn), const(w_out)],
        out_specs=row(d),
        out_shape=jax.ShapeDtypeStruct((t, d), F32),
        scratch_shapes=[pltpu.VMEM((tm, d_rwkv), BF16)],
        compiler_params=_cparams(("arbitrary",)),
        name="merge_out_proj",
    )(x, yf, yb, bonus, gate, ln_w, ln_b, y_attn, w_out)


def _router_kernel(x_ref, g_ref, w_ref, b_ref, h_ref, idx_ref, wt_ref):
    x = x_ref[...]
    ms = jnp.mean(x * x, axis=-1, keepdims=True)
    h = x * lax.rsqrt(ms + RMS_EPS) * g_ref[...]
    h_ref[...] = h
    h_hi, h_lo = _split2(h)
    w_hi = w_ref[0]
    w_lo = w_ref[1]
    logits = (jnp.dot(h_hi, w_hi, preferred_element_type=F32)
              + jnp.dot(h_hi, w_lo, preferred_element_type=F32)
              + jnp.dot(h_lo, w_hi, preferred_element_type=F32)) + b_ref[...]
    lane = lax.broadcasted_iota(jnp.int32, logits.shape, 1).astype(F32)
    big = jnp.float32(LANES)
    is_g = lane < N_GROUPS
    gl = jnp.where(is_g, logits, NEG_BIG)
    gmax = jnp.max(gl, axis=-1, keepdims=True)
    gsel = jnp.min(jnp.where(is_g & (gl == gmax), lane, big), axis=-1, keepdims=True)
    g1 = 1.0 / jnp.sum(jnp.where(is_g, jnp.exp(gl - gmax), 0.0), axis=-1, keepdims=True)
    lo = N_GROUPS + gsel * EXPERTS_PER_GROUP
    in_grp = (lane >= lo) & (lane < lo + EXPERTS_PER_GROUP)
    el = jnp.where(in_grp, logits, NEG_BIG)
    v1 = jnp.max(el, axis=-1, keepdims=True)
    i1 = jnp.min(jnp.where(in_grp & (el == v1), lane, big), axis=-1, keepdims=True)
    rest = in_grp & (lane != i1)
    el2 = jnp.where(rest, logits, NEG_BIG)
    v2 = jnp.max(el2, axis=-1, keepdims=True)
    i2 = jnp.min(jnp.where(rest & (el2 == v2), lane, big), axis=-1, keepdims=True)
    e2 = jnp.exp(v2 - v1)
    ww1 = 1.0 / (1.0 + e2)
    ww2 = e2 / (1.0 + e2)
    idx = jnp.where(lane == 0, i1 - N_GROUPS, jnp.where(lane == 1, i2 - N_GROUPS, 0.0))
    idx_ref[...] = idx.astype(jnp.int32)
    wt_ref[...] = jnp.where(lane == 0, g1 * ww1, jnp.where(lane == 1, g1 * ww2, 0.0))


def _router(x, g, w_split, b_pad, tm):
    t, d = x.shape
    return pl.pallas_call(
        _router_kernel,
        grid=(t // tm,),
        in_specs=[pl.BlockSpec((tm, d), lambda i: (i, 0)),
                  pl.BlockSpec((1, d), lambda i: (0, 0)),
                  pl.BlockSpec((2, d, LANES), lambda i: (0, 0, 0)),
                  pl.BlockSpec((1, LANES), lambda i: (0, 0))],
        out_specs=[pl.BlockSpec((tm, d), lambda i: (i, 0)),
                   pl.BlockSpec((tm, LANES), lambda i: (i, 0)),
                   pl.BlockSpec((tm, LANES), lambda i: (i, 0))],
        out_shape=[jax.ShapeDtypeStruct((t, d), F32),
                   jax.ShapeDtypeStruct((t, LANES), jnp.int32),
                   jax.ShapeDtypeStruct((t, LANES), F32)],
        compiler_params=_cparams(("arbitrary",)),
        name="router",
    )(x, g, w_split, b_pad)


def _moe_kernel(te_ref, nt_ref, nv_ref, rows_ref, h_hbm, wg_ref, wu_ref, wd_ref, o_ref,
                xg_ref, xb_ref, acc_ref, sem, *, tm, nf):
    i = pl.program_id(0)
    f = pl.program_id(1)
    nt = nt_ref[0]
    active = i < nt
    slot = i % 2
    n_slab = xg_ref.shape[2] // LANES

    def issue_tile(tile, buf):
        def body(g, carry):
            for u in range(GATHER_UNROLL):
                j = g * GATHER_UNROLL + u
                src = h_hbm.at[pl.ds(rows_ref[tile * tm + j], 1), :]
                pltpu.make_async_copy(src, xg_ref.at[buf, pl.ds(j, 1), :],
                                      sem.at[buf]).start(priority=1)
            return carry
        lax.fori_loop(0, nv_ref[tile] // GATHER_UNROLL, body, 0)

    def wait_tile(tile, buf):
        def body(g, carry):
            pltpu.make_async_copy(h_hbm.at[pl.ds(0, GATHER_CHUNK), :],
                                  xg_ref.at[buf, pl.ds(0, GATHER_CHUNK), :], sem.at[buf]).wait()
            return carry
        lax.fori_loop(0, nv_ref[tile] // GATHER_CHUNK, body, 0)

    @pl.when((i == 0) & (f == 0))
    def _():
        xg_ref[...] = jnp.zeros_like(xg_ref)
        issue_tile(0, 0)

    @pl.when(active & (f == 0))
    def _():
        wait_tile(i, slot)
        xb_ref[...] = xg_ref[slot].astype(BF16)
        issue_tile(i + 1, 1 - slot)

    @pl.when(active)
    def _():
        xb = xb_ref[...]
        gate = jnp.dot(xb, wg_ref[...].astype(BF16), preferred_element_type=F32)
        up = jnp.dot(xb, wu_ref[...].astype(BF16), preferred_element_type=F32)
        hid = (gate * _sigmoid(gate)) * up
        part = jnp.dot(hid.astype(BF16), wd_ref[...].astype(BF16), preferred_element_type=F32)

        @pl.when(f == 0)
        def _():
            acc_ref[...] = part

        @pl.when((f != 0) & (f != nf - 1))
        def _():
            acc_ref[...] += part

        @pl.when(f == nf - 1)
        def _():
            for c in range(n_slab):
                sl = slice(c * LANES, (c + 1) * LANES)
                o_ref[pl.ds(c, tm, stride=n_slab), :] = acc_ref[:, sl] + part[:, sl]

    @pl.when(jnp.logical_not(active) & (f == 0))
    def _():
        o_ref[...] = jnp.zeros_like(o_ref)


def _moe(h, tile_expert, n_tiles, n_valid, rows, w_gate, w_up, w_down, tm, tf):
    t, d = h.shape
    n_e, _, d_e = w_gate.shape
    n_slab = d // LANES
    nt_max = tile_expert.shape[0]
    nf = d_e // tf
    assert nf >= 2 and tm % GATHER_CHUNK == 0 and GATHER_CHUNK % GATHER_UNROLL == 0
    assert rows.shape[0] == (nt_max + 1) * tm and n_valid.shape[0] == nt_max + 1

    def fidx(i, f, nt):
        return jnp.where(i < nt[0], f, nf - 1)

    grid_spec = pltpu.PrefetchScalarGridSpec(
        num_scalar_prefetch=4,
        grid=(nt_max, d_e // tf),
        in_specs=[pl.BlockSpec(memory_space=pl.ANY),
                  pl.BlockSpec((None, d, tf), lambda i, f, te, nt, nv, rw: (te[i], 0, fidx(i, f, nt))),
                  pl.BlockSpec((None, d, tf), lambda i, f, te, nt, nv, rw: (te[i], 0, fidx(i, f, nt))),
                  pl.BlockSpec((None, tf, d), lambda i, f, te, nt, nv, rw: (te[i], fidx(i, f, nt), 0))],
        out_specs=pl.BlockSpec((tm * n_slab, LANES), lambda i, f, te, nt, nv, rw: (i, 0)),
        scratch_shapes=[pltpu.VMEM((2, tm, d), F32), pltpu.VMEM((tm, d), BF16),
                        pltpu.VMEM((tm, d), F32), pltpu.SemaphoreType.DMA((2,))],
    )
    return pl.pallas_call(
        functools.partial(_moe_kernel, tm=tm, nf=nf),
        grid_spec=grid_spec,
        out_shape=jax.ShapeDtypeStruct((nt_max * tm * n_slab, LANES), F32),
        compiler_params=_cparams(("arbitrary", "arbitrary")),
        name="grouped_expert_mlp",
    )(tile_expert, n_tiles, n_valid, rows, h, w_gate, w_up, w_down)


def _final_kernel(s0_ref, s1_ref, x_ref, wt_ref, g_ref, y_hbm, o_ref, yg_ref, sem, *, tm, nt):
    i = pl.program_id(0)
    slot = i % 2
    d = x_ref.shape[1]
    n_slab = d // LANES

    def issue(tile, buf):
        def body(j, carry):
            for k, s_ref in enumerate((s0_ref, s1_ref)):
                src = y_hbm.at[pl.ds(pl.multiple_of(s_ref[tile * tm + j] * n_slab, n_slab), n_slab), :]
                dst = yg_ref.at[buf, k, pl.ds(pl.multiple_of(j * n_slab, n_slab), n_slab), :]
                pltpu.make_async_copy(src, dst, sem.at[buf]).start(priority=k)
            return carry
        lax.fori_loop(0, tm, body, 0, unroll=8)

    @pl.when(i == 0)
    def _():
        issue(0, 0)

    @pl.when(i + 1 < nt)
    def _():
        issue(i + 1, 1 - slot)

    for k in range(2):
        pltpu.make_async_copy(y_hbm.at[pl.ds(0, tm * n_slab), :], yg_ref.at[slot, k],
                              sem.at[slot]).wait()

    wt = wt_ref[...]
    w0 = wt[:, 0:1]
    w1 = wt[:, 1:2]
    ss = jnp.zeros((tm, 1), F32)
    for c in range(n_slab):
        sl = slice(c * LANES, (c + 1) * LANES)
        rows = pl.ds(c, tm, stride=n_slab)
        xs = x_ref[:, sl] + w0 * yg_ref[slot, 0, rows, :] + w1 * yg_ref[slot, 1, rows, :]
        o_ref[:, sl] = xs
        ss = ss + jnp.sum(xs * xs, axis=-1, keepdims=True)
    o_ref[...] = o_ref[...] * lax.rsqrt(ss * (1.0 / d) + RMS_EPS) * g_ref[...]


def _final(slot0, slot1, x, wts, g, y_sorted, tm):
    t, d = x.shape
    grid_spec = pltpu.PrefetchScalarGridSpec(
        num_scalar_prefetch=2,
        grid=(t // tm,),
        in_specs=[pl.BlockSpec((tm, d), lambda i, s0, s1: (i, 0)),
                  pl.BlockSpec((tm, LANES), lambda i, s0, s1: (i, 0)),
                  pl.BlockSpec((1, d), lambda i, s0, s1: (0, 0)),
                  pl.BlockSpec(memory_space=pl.ANY)],
        out_specs=pl.BlockSpec((tm, d), lambda i, s0, s1: (i, 0)),
        scratch_shapes=[pltpu.VMEM((2, 2, tm * (d // LANES), LANES), F32),
                        pltpu.SemaphoreType.DMA((2,))],
    )
    return pl.pallas_call(
        functools.partial(_final_kernel, tm=tm, nt=t // tm),
        grid_spec=grid_spec,
        out_shape=jax.ShapeDtypeStruct((t, d), F32),
        compiler_params=_cparams(("arbitrary",)),
        name="combine_final_norm",
    )(slot0, slot1, x, wts, g, y_sorted)


def _dispatch_plan(experts, tm):
    t = experts.shape[0]
    ef = experts.T.reshape(-1)
    onehot = (ef[:, None] == jnp.arange(N_EXPERTS, dtype=jnp.int32)[None, :]).astype(jnp.int32)
    csum = jnp.cumsum(onehot, axis=0)
    rank = jnp.take_along_axis(csum, ef[:, None], axis=1)[:, 0] - 1
    counts = csum[-1]
    tiles_e = (counts + tm - 1) // tm
    tile_end = jnp.cumsum(tiles_e)
    tile_start = tile_end - tiles_e
    n_tiles = tile_end[-1]
    nt_max = (2 * t) // tm + N_EXPERTS
    slot = tile_start[ef] * tm + rank
    tid = jnp.minimum(jnp.arange(nt_max, dtype=jnp.int32), n_tiles - 1)
    tile_expert = jnp.sum((tile_end[None, :] <= tid[:, None]).astype(jnp.int32), axis=1)
    tok = jnp.tile(jnp.arange(t, dtype=jnp.int32), 2)
    rows = jnp.zeros(((nt_max + 1) * tm,), jnp.int32).at[slot].set(tok)
    tix = jnp.arange(nt_max + 1, dtype=jnp.int32)
    te_all = jnp.concatenate([tile_expert, tile_expert[-1:]])
    left = counts[te_all] - (tix - tile_start[te_all]) * tm
    n_valid = jnp.where(tix < n_tiles, jnp.clip(left, 0, tm), 0)
    n_valid = (n_valid + GATHER_CHUNK - 1) // GATHER_CHUNK * GATHER_CHUNK
    return (slot[:t], slot[t:], rows, tile_expert, n_tiles.reshape(1).astype(jnp.int32),
            n_valid.astype(jnp.int32))


def kernel(x, positions, norm_mix, w_in, mu_shift, w0, w2, a0, a2, g2, k_k, k_a, r_k, ln_x_w, ln_x_b, w_out, norm_ffn, router_group_w, router_group_b, router_expert_w, router_expert_b, w_gate, w_up, w_down, norm_final):
    bsz, seq, d = x.shape
    assert bsz == 1
    depth = w_in.shape[0]
    d_rwkv = k_k.shape[1]
    d_attn = w_out.shape[1] - d_rwkv
    rwkv_cols = mu_shift.shape[2]
    nr = -(-rwkv_cols // (2 * LANES)) * (2 * LANES)
    assert 3 * d_rwkv + 2 * LANES + GATE_LORA == rwkv_cols and nr == 3 * d_rwkv + 4 * LANES
    tm_moe = 576
    tf_moe = 512
    sb_attn = max(dil for _, dil in DILATED_PATTERNS) * Q_BLOCK
    assert seq % sb_attn == 0

    xt = x.reshape(seq, d)
    pos = positions.reshape(seq, 1)
    for l in range(depth):
        w_t = jnp.swapaxes(w_in[l], 0, 1)
        w_r = jnp.pad(w_t[:rwkv_cols], ((0, nr - rwkv_cols), (0, 0))).astype(BF16)
        w_a = w_t[rwkv_cols:].astype(BF16)
        mu = jnp.pad(mu_shift[l], ((0, 0), (0, nr - rwkv_cols)))
        zl = jnp.zeros((DECAY_LORA, d_rwkv), F32)
        w2p = jnp.stack([jnp.concatenate([w2[l, 0], zl]), jnp.concatenate([zl, w2[l, 1]])]).astype(BF16)
        a2p = jnp.stack([jnp.concatenate([a2[l, 0], zl]), jnp.concatenate([zl, a2[l, 1]])]).astype(BF16)
        g2p = jnp.pad(g2[l], ((0, 2 * LANES - GATE_LORA), (0, 0))).astype(BF16)
        rk = r_k[l].reshape(1, d_rwkv)

        z_r = _norm_matmul_shift(xt, norm_mix[l][None], w_r, mu, 256)
        z_a = _norm_matmul_rope(xt, norm_mix[l][None], w_a, pos, 256)
        yf, yb, bonus, gate = _rwkv_mixer(z_r, w0[l], a0[l], w2p, a2p, g2p,
                                          k_k[l][None], k_a[l][None], rk, d_rwkv)
        y_attn = _dilated_attention(z_a, d_attn, sb_attn)
        x2 = _out_proj(xt, yf, yb, bonus, gate, ln_x_w[l][None], ln_x_b[l][None],
                       y_attn, w_out[l].astype(BF16), 256)

        w_rt = jnp.concatenate(
            [router_group_w[l], router_expert_w[l].transpose(1, 0, 2).reshape(d, N_EXPERTS)], axis=1)
        w_rt = jnp.pad(w_rt, ((0, 0), (0, LANES - w_rt.shape[1])))
        rt_hi = w_rt.astype(BF16)
        rt_lo = (w_rt - rt_hi.astype(F32)).astype(BF16)
        b_rt = jnp.concatenate([router_group_b[l], router_expert_b[l].reshape(-1)])
        b_rt = jnp.pad(b_rt, (0, LANES - b_rt.shape[0]))[None]
        h, idx, wts = _router(x2, norm_ffn[l][None], jnp.stack([rt_hi, rt_lo]), b_rt, 256)

        slot0, slot1, rows, tile_expert, n_tiles, n_valid = _dispatch_plan(idx[:, :2], tm_moe)
        y_sorted = _moe(h, tile_expert, n_tiles, n_valid, rows, w_gate[l], w_up[l], w_down[l],
                        tm_moe, tf_moe)
        is_last = l == depth - 1
        assert is_last, "the combine kernel applies the final norm; depth must be 1"
        xt = _final(slot0, slot1, x2, wts, norm_final[None], y_sorted, 256)
    return xt.reshape(bsz, seq, d)
```

```python
import functools

import jax
import jax.numpy as jnp
from jax import lax
from jax.experimental import pallas as pl
from jax.experimental.pallas import tpu as pltpu

HEAD_DIM = 64
LANES = 128
SUBLANES = 8
DECAY_LORA = 64
ICLR_LORA = 64
GATE_LORA = 160
GN_EPS = 64e-5
RMS_EPS = 1e-6
ROPE_THETA = 500000.0
ROPE_DIM = HEAD_DIM // 4
DILATED_PATTERNS = ((128, 1), (512, 4), (2048, 16))
Q_BLOCK = 128
N_GROUPS = 4
EXPERTS_PER_GROUP = 8
N_EXPERTS = N_GROUPS * EXPERTS_PER_GROUP
NEG_BIG = -1e30
CHUNK = 64
GATHER_CHUNK = 64
GATHER_UNROLL = 8
VMEM_LIMIT = 56 * 1024 * 1024

BF16 = jnp.bfloat16
F32 = jnp.float32


def _cparams(sem):
    return pltpu.CompilerParams(dimension_semantics=sem, vmem_limit_bytes=VMEM_LIMIT)


def _dot(a, b):
    return jnp.dot(a.astype(BF16), b.astype(BF16), preferred_element_type=F32)


def _dot_nt(a, b):
    return lax.dot_general(a.astype(BF16), b.astype(BF16), (((1,), (1,)), ((), ())),
                           preferred_element_type=F32)


def _dot_tn(a, b):
    return lax.dot_general(a.astype(BF16), b.astype(BF16), (((0,), (0,)), ((), ())),
                           preferred_element_type=F32)


def _split2(x):
    hi = x.astype(BF16)
    lo = (x - hi.astype(F32)).astype(BF16)
    return hi, lo


def _split3(x):
    hi = x.astype(BF16)
    r1 = x - hi.astype(F32)
    mid = r1.astype(BF16)
    lo = (r1 - mid.astype(F32)).astype(BF16)
    return hi, mid, lo


def _head_ones():
    i = lax.broadcasted_iota(jnp.int32, (LANES, LANES), 0) // HEAD_DIM
    j = lax.broadcasted_iota(jnp.int32, (LANES, LANES), 1) // HEAD_DIM
    return (i == j).astype(BF16)


def _head_sum(x, ones_bd):
    hi, lo = _split2(x)
    return (jnp.dot(hi, ones_bd, preferred_element_type=F32)
            + jnp.dot(lo, ones_bd, preferred_element_type=F32))


def _sigmoid(x):
    return 1.0 / (1.0 + jnp.exp(-x))


def _norm_matmul_shift_kernel(x_ref, xp_ref, xn_ref, g_ref, w_ref, mu_ref, o_ref, *, nt):
    i = pl.program_id(0)
    tm = x_ref.shape[0]
    halo = xp_ref.shape[0]
    x = jnp.concatenate([xp_ref[...], x_ref[...], xn_ref[...]], axis=0)
    ms = jnp.mean(x * x, axis=-1, keepdims=True)
    h = x * lax.rsqrt(ms + RMS_EPS) * g_ref[...]
    z = lax.dot_general(h.astype(BF16), w_ref[...], (((1,), (1,)), ((), ())),
                        preferred_element_type=F32)
    rows = tm + 2 * halo
    zc = z[halo:halo + tm]
    z_prev = pltpu.roll(z, 1, 0)[halo:halo + tm]
    z_next = pltpu.roll(z, rows - 1, 0)[halo:halo + tm]
    row = lax.broadcasted_iota(jnp.int32, (tm, 1), 0)
    z_prev = jnp.where((i == 0) & (row == 0), 0.0, z_prev)
    z_next = jnp.where((i == nt - 1) & (row == tm - 1), 0.0, z_next)
    o_ref[...] = zc + mu_ref[0:1, :] * (z_prev - zc) + mu_ref[1:2, :] * (z_next - zc)


def _norm_matmul_shift(x, g, w, mu, tm):
    t, d = x.shape
    n = w.shape[0]
    nt = t // tm
    hb = tm // SUBLANES
    last = t // SUBLANES - 1
    return pl.pallas_call(
        functools.partial(_norm_matmul_shift_kernel, nt=nt),
        grid=(nt,),
        in_specs=[pl.BlockSpec((tm, d), lambda i: (i, 0)),
                  pl.BlockSpec((SUBLANES, d), lambda i: (jnp.maximum(i * hb - 1, 0), 0)),
                  pl.BlockSpec((SUBLANES, d), lambda i: (jnp.minimum((i + 1) * hb, last), 0)),
                  pl.BlockSpec((1, d), lambda i: (0, 0)),
                  pl.BlockSpec((n, d), lambda i: (0, 0)),
                  pl.BlockSpec((2, n), lambda i: (0, 0))],
        out_specs=pl.BlockSpec((tm, n), lambda i: (i, 0)),
        out_shape=jax.ShapeDtypeStruct((t, n), F32),
        compiler_params=_cparams(("arbitrary",)),
        name="norm_in_proj_shift",
    )(x, x, x, g, w, mu)


def _rwkv_kernel(zf_ref, zb_ref,
                 w0_ref, a0_ref, w2_ref, a2_ref, g2_ref, kk_ref, ka_ref, rk_ref,
                 yf_ref, yb_ref, bonus_ref, gate_ref,
                 prep_ref, h_ref, *, nc, d_rwkv):
    c = pl.program_id(0)
    C = CHUNK
    n_hp = d_rwkv // LANES

    @pl.when(c == 0)
    def _():
        h_ref[...] = jnp.zeros_like(h_ref)

    zs = (zf_ref, zb_ref)
    ones_bd = _head_ones()
    lane = lax.broadcasted_iota(jnp.int32, (1, LANES), 1)
    m_a = lane < HEAD_DIM
    ti = lax.broadcasted_iota(jnp.int32, (C, C), 0)
    tj = lax.broadcasted_iota(jnp.int32, (C, C), 1)
    bi = lax.broadcasted_iota(jnp.int32, (2 * C, 2 * C), 0)
    bj = lax.broadcasted_iota(jnp.int32, (2 * C, 2 * C), 1)
    same_blk = (bi // C) == (bj // C)
    eye = bi == bj
    cum_mat = ((tj <= ti).astype(BF16), (tj >= ti).astype(BF16))
    strict = (same_blk & ((bj % C) < (bi % C)), same_blk & ((bj % C) > (bi % C)))
    incl = (same_blk & ((bj % C) <= (bi % C)), same_blk & ((bj % C) >= (bi % C)))
    last_row = (C - 1, 0)
    off_wl = 3 * d_rwkv
    off_al = off_wl + LANES
    off_gl = off_al + LANES

    def to_rows(x):
        return jnp.concatenate([x[:, s * LANES:(s + 1) * LANES] for s in range(n_hp)], axis=0)

    def to_lanes(x):
        return jnp.concatenate([x[s * C:(s + 1) * C] for s in range(n_hp)], axis=1)

    def bd(x):
        return jnp.concatenate([jnp.where(m_a, x, 0), jnp.where(m_a, 0, x)], axis=0)

    def unbd(x):
        return x[0:C] + x[C:2 * C]

    w_c = []
    for d in range(2):
        r = zs[d][:, 0:d_rwkv]
        k = zs[d][:, d_rwkv:2 * d_rwkv]
        kkr = k * kk_ref[...]
        sums = [to_rows(kkr * kkr)]
        if d == 0:
            sums.append(to_rows(r * k * rk_ref[...]))
        hs = _head_sum(jnp.concatenate(sums, axis=0), ones_bd)
        kk = kkr * lax.rsqrt(to_lanes(hs[0:n_hp * C]) + 1e-12)
        if d == 0:
            bonus_ref[...] = to_lanes(hs[n_hp * C:2 * n_hp * C]) * zs[0][:, 2 * d_rwkv:3 * d_rwkv]
            gate_ref[...] = _dot(_sigmoid(zs[0][:, off_gl:off_gl + 2 * LANES]), g2_ref[...])
        u = w0_ref[d:d + 1, :] + _dot(jnp.tanh(zs[d][:, off_wl:off_wl + LANES]), w2_ref[d])
        w_log = -(jnp.maximum(-u, 0.0) + jnp.log1p(jnp.exp(-jnp.abs(u)))) - 0.5
        logd = -jnp.exp(w_log)
        a = _sigmoid(a0_ref[d:d + 1, :] + _dot(zs[d][:, off_al:off_al + LANES], a2_ref[d]))
        kd = k * (1.0 + (a - 1.0) * ka_ref[...])
        b = kk * a
        l_hi, l_mid, l_lo = _split3(logd)
        cm = cum_mat[d]
        cum = (jnp.dot(cm, l_hi, preferred_element_type=F32)
               + jnp.dot(cm, l_mid, preferred_element_type=F32)
               + jnp.dot(cm, l_lo, preferred_element_type=F32))
        cum_c = cum[last_row[d]:last_row[d] + 1, :]
        w_inv = jnp.exp(-cum)
        w_c.append(jnp.exp(cum_c))
        w_end = w_c[d] * w_inv
        prep_ref[d, 0] = -kk * jnp.exp(cum - logd)
        prep_ref[d, 1] = r * jnp.exp(cum)
        prep_ref[d, 2] = b * w_inv
        prep_ref[d, 3] = kd * w_inv
        prep_ref[d, 4] = b * w_end
        prep_ref[d, 5] = kd * w_end

    blocks = [(d, hp) for hp in range(n_hp) for d in range(2)]

    def slab(d, i, hp):
        return prep_ref[d, i, :, hp * LANES:(hp + 1) * LANES]

    incl2 = tuple(jnp.concatenate([m, m], axis=1) for m in incl)
    eye_f = eye.astype(F32)
    xs, aks, lows, z0s, vbds = [], [], [], [], []
    for d, hp in blocks:
        at = bd(slab(d, 0, hp).astype(BF16))
        rt = bd(slab(d, 1, hp).astype(BF16))
        bt = slab(d, 2, hp).astype(BF16)
        kt = slab(d, 3, hp).astype(BF16)
        lhs = jnp.concatenate([at, rt], axis=0)
        rhs = jnp.concatenate([bt, bt, kt, kt], axis=0)
        g = lax.dot_general(lhs, rhs, (((1,), (1,)), ((), ())), preferred_element_type=F32)
        xs.append(jnp.where(strict[d], g[0:2 * C, 0:2 * C], 0.0))
        aks.append(jnp.where(strict[d], g[0:2 * C, 2 * C:4 * C], 0.0).astype(BF16))
        lows.append(jnp.where(incl2[d], g[2 * C:4 * C, :], 0.0).astype(BF16))
    for i, (d, hp) in enumerate(blocks):
        vbd = bd(zs[d][:, 2 * d_rwkv + hp * LANES:2 * d_rwkv + (hp + 1) * LANES].astype(BF16))
        vbds.append(vbd)
        akv = jnp.dot(aks[i], vbd, preferred_element_type=F32)
        z0s.append(jnp.concatenate([bd(slab(d, 0, hp).astype(BF16)), akv.astype(BF16)], axis=1))

    n_sq = C.bit_length() - 1
    ts = [eye_f + x for x in xs]
    ps = [x.astype(BF16) for x in xs]
    for j in range(1, n_sq):
        last = j == n_sq - 1
        for i in range(len(blocks)):
            if j == 1:
                ps[i] = jnp.dot(ps[i], ps[i], preferred_element_type=F32).astype(BF16)
            tb = ts[i].astype(BF16)
            if last:
                ts[i] = ts[i] + jnp.dot(ps[i], tb, preferred_element_type=F32)
            else:
                res = jnp.dot(ps[i], jnp.concatenate([ps[i], tb], axis=1), preferred_element_type=F32)
                ts[i] = ts[i] + res[:, LANES:2 * LANES]
                ps[i] = res[:, 0:LANES].astype(BF16)

    wts = []
    for i in range(len(blocks)):
        zz = jnp.dot(ts[i].astype(BF16), z0s[i], preferred_element_type=F32)
        bottom = jnp.concatenate([jnp.zeros((2 * C, LANES), BF16), vbds[i]], axis=1)
        wts.append(jnp.concatenate([zz.astype(BF16), bottom], axis=0))
    outs = []
    for i, (d, hp) in enumerate(blocks):
        bb_t = bd(slab(d, 4, hp)).T.astype(BF16)
        kb_t = bd(slab(d, 5, hp)).T.astype(BF16)
        lhs = jnp.concatenate([lows[i], jnp.concatenate([bb_t, kb_t], axis=1)], axis=0)
        outs.append(jnp.dot(lhs, wts[i], preferred_element_type=F32))
    for i, (d, hp) in enumerate(blocks):
        sl = pl.ds(hp * LANES, LANES)
        o = outs[i]
        r_hat = bd(slab(d, 1, hp)) + o[0:2 * C, 0:LANES]
        p_mat = eye_f * w_c[d][:, hp * LANES:(hp + 1) * LANES] + o[2 * C:4 * C, 0:LANES]
        lhs = jnp.concatenate([r_hat, p_mat], axis=0).astype(BF16)
        res = jnp.dot(lhs, h_ref[d, hp].astype(BF16), preferred_element_type=F32)
        y = unbd(res[0:2 * C] + o[0:2 * C, LANES:2 * LANES])
        h_ref[d, hp] = res[2 * C:4 * C] + o[2 * C:4 * C, LANES:2 * LANES]
        if d == 0:
            yf_ref[:, sl] = y
        else:
            yb_ref[:, sl] = y


def _rwkv_mixer(z, w0, a0, w2p, a2p, g2p, k_k, k_a, r_k, d_rwkv):
    t, nr = z.shape
    C = CHUNK
    nc = t // C
    n_hp = d_rwkv // LANES

    def full(a):
        nd = a.ndim
        return pl.BlockSpec(a.shape, lambda c: (0,) * nd)

    out_spec_f = pl.BlockSpec((C, d_rwkv), lambda c: (c, 0))
    out_spec_b = pl.BlockSpec((C, d_rwkv), lambda c: (nc - 1 - c, 0))
    out_sd = jax.ShapeDtypeStruct((t, d_rwkv), F32)
    params = (w0, a0, w2p, a2p, g2p, k_k, k_a, r_k)
    return pl.pallas_call(
        functools.partial(_rwkv_kernel, nc=nc, d_rwkv=d_rwkv),
        grid=(nc,),
        in_specs=[pl.BlockSpec((C, nr), lambda c: (c, 0)),
                  pl.BlockSpec((C, nr), lambda c: (nc - 1 - c, 0))]
                 + [full(p) for p in params],
        out_specs=[out_spec_f, out_spec_b, out_spec_f, out_spec_f],
        out_shape=[out_sd, out_sd, out_sd, out_sd],
        scratch_shapes=[pltpu.VMEM((2, 6, C, d_rwkv), F32),
                        pltpu.VMEM((2, n_hp, LANES, LANES), F32)],
        compiler_params=_cparams(("arbitrary",)),
        name="rwkv7_chunk_scan",
    )(z, z, *params)


def _rope_coeffs(pos, inv_lane):
    half = ROPE_DIM // 2
    j = lax.broadcasted_iota(jnp.int32, (1, LANES), 1) % HEAD_DIM
    ang = pos.astype(F32) * inv_lane
    cs = jnp.cos(ang)
    sn = jnp.sin(ang)
    c = jnp.where(j < ROPE_DIM, cs, 1.0)
    s1 = jnp.where(j < half, -sn, 0.0)
    s2 = jnp.where((j >= half) & (j < ROPE_DIM), sn, 0.0)
    return c, s1, s2


def _rot(x, c, s1, s2):
    half = ROPE_DIM // 2
    return x * c + pltpu.roll(x, LANES - half, 1) * s1 + pltpu.roll(x, half, 1) * s2


def _norm_matmul_rope_kernel(x_ref, g_ref, w_ref, pos_ref, inv_ref, o_ref, *, n_rot):
    x = x_ref[...]
    ms = jnp.mean(x * x, axis=-1, keepdims=True)
    h = x * lax.rsqrt(ms + RMS_EPS) * g_ref[...]
    z = lax.dot_general(h.astype(BF16), w_ref[...], (((1,), (1,)), ((), ())),
                        preferred_element_type=F32)
    c, s1, s2 = _rope_coeffs(pos_ref[...], inv_ref[...])
    for s in range(z.shape[1] // LANES):
        sl = slice(s * LANES, (s + 1) * LANES)
        zs = z[:, sl]
        if s < n_rot:
            zs = _rot(zs, c, s1, s2)
            if s < n_rot // 2:
                zs = zs * (HEAD_DIM ** -0.5)
        o_ref[:, sl] = zs


def _norm_matmul_rope(x, g, w, positions, tm):
    t, d = x.shape
    n = w.shape[0]
    half = ROPE_DIM // 2
    inv_freq = jnp.power(ROPE_THETA, -jnp.arange(half, dtype=F32) * 2.0 / ROPE_DIM)
    inv_lane = jnp.tile(inv_freq, LANES // half)[None]
    return pl.pallas_call(
        functools.partial(_norm_matmul_rope_kernel, n_rot=2 * (n // 3) // LANES),
        grid=(t // tm,),
        in_specs=[pl.BlockSpec((tm, d), lambda i: (i, 0)),
                  pl.BlockSpec((1, d), lambda i: (0, 0)),
                  pl.BlockSpec((n, d), lambda i: (0, 0)),
                  pl.BlockSpec((tm, 1), lambda i: (i, 0)),
                  pl.BlockSpec((1, LANES), lambda i: (0, 0))],
        out_specs=pl.BlockSpec((tm, n), lambda i: (i, 0)),
        out_shape=jax.ShapeDtypeStruct((t, n), F32),
        compiler_params=_cparams(("arbitrary",)),
        name="norm_in_proj_rope",
    )(x, g, w, positions, inv_lane)


def _attn_kernel(q_ref, k_ref, v_ref, o_ref, op_ref, lp_ref, *, seq_t, sb):
    jsb = pl.program_id(1)
    lane = lax.broadcasted_iota(jnp.int32, (1, LANES), 1)
    m_a = lane < HEAD_DIM
    for p, (window, dil) in enumerate(DILATED_PATTERNS):
        radius = window // (2 * dil)
        win = Q_BLOCK + 2 * radius
        seq = seq_t // dil
        nq = sb // (dil * Q_BLOCK)
        qi = lax.broadcasted_iota(jnp.int32, (Q_BLOCK, win), 0)
        kj = lax.broadcasted_iota(jnp.int32, (Q_BLOCK, win), 1)

        def rows(start, size, dil=dil):
            return pl.ds(start, size) if dil == 1 else pl.ds(start, size, stride=dil)

        def body(idx, carry, dil=dil, radius=radius, win=win, seq=seq, nq=nq, qi=qi, kj=kj,
                 rows=rows, p=p):
            r = idx // nq
            jj = idx % nq
            q_l0 = (jsb * nq + jj) * Q_BLOCK
            start_l = jnp.clip(q_l0 - radius, 0, seq - win)
            q_rows = rows(r + dil * Q_BLOCK * jj, Q_BLOCK)
            k_rows = rows(r + dil * start_l, win)
            q = q_ref[q_rows, :].astype(BF16)
            kw = k_ref[k_rows, :].astype(BF16)
            vw = v_ref[k_rows, :].astype(BF16)
            valid = jnp.abs((q_l0 + qi) - (start_l + kj)) <= radius
            one = jnp.ones((), BF16)
            halves = ((jnp.where(m_a, q, 0), jnp.where(m_a, vw, one)),
                      (jnp.where(m_a, 0, q), jnp.where(m_a, one, vw)))
            res, mx = [], []
            for qh, vh in halves:
                sc = lax.dot_general(qh, kw, (((1,), (1,)), ((), ())), preferred_element_type=F32)
                sc = jnp.where(valid, sc, NEG_BIG)
                m = jnp.max(sc, axis=-1, keepdims=True)
                pexp = jnp.exp(sc - m).astype(BF16)
                res.append(jnp.dot(pexp, vh, preferred_element_type=F32))
                mx.append(m)
            num = jnp.where(m_a, res[0], res[1])
            den = pltpu.roll(jnp.where(m_a, res[1], res[0]), HEAD_DIM, 1)
            op_ref[p, q_rows, :] = num / den
            lp_ref[p, q_rows, :] = jnp.where(m_a, mx[0], mx[1]) + jnp.log(den)
            return carry

        lax.fori_loop(0, dil * nq, body, 0, unroll=8)

    l1 = lp_ref[0]
    l2 = lp_ref[1]
    l3 = lp_ref[2]
    m = jnp.maximum(jnp.maximum(l1, l2), l3)
    e1 = jnp.exp(l1 - m)
    e2 = jnp.exp(l2 - m)
    e3 = jnp.exp(l3 - m)
    o_ref[...] = (e1 * op_ref[0] + e2 * op_ref[1] + e3 * op_ref[2]) / (e1 + e2 + e3)


def _dilated_attention(z_attn, d_attn, sb):
    t = z_attn.shape[0]
    n_hp = d_attn // LANES
    n_pat = len(DILATED_PATTERNS)
    return pl.pallas_call(
        functools.partial(_attn_kernel, seq_t=t, sb=sb),
        grid=(n_hp, t // sb),
        in_specs=[pl.BlockSpec((sb, LANES), lambda h, j: (j, h)),
                  pl.BlockSpec((t, LANES), lambda h, j: (0, n_hp + h)),
                  pl.BlockSpec((t, LANES), lambda h, j: (0, 2 * n_hp + h))],
        out_specs=pl.BlockSpec((sb, LANES), lambda h, j: (j, h)),
        out_shape=jax.ShapeDtypeStruct((t, d_attn), F32),
        scratch_shapes=[pltpu.VMEM((n_pat, sb, LANES), F32), pltpu.VMEM((n_pat, sb, LANES), F32)],
        compiler_params=_cparams(("arbitrary", "arbitrary")),
        name="dilated_attention",
    )(z_attn, z_attn, z_attn)


def _out_proj_kernel(x_ref, yf_ref, yb_ref, bonus_ref, gate_ref, lnw_ref, lnb_ref,
                     ya_ref, w_ref, out_ref, yr_ref, *, d_rwkv):
    ones_bd = _head_ones()
    for hp in range(d_rwkv // LANES):
        sl = pl.ds(hp * LANES, LANES)
        y = yf_ref[:, sl] + yb_ref[:, sl]
        mean = _head_sum(y, ones_bd) * (1.0 / HEAD_DIM)
        yc = y - mean
        var = _head_sum(yc * yc, ones_bd) * (1.0 / HEAD_DIM)
        yn = yc * lax.rsqrt(var + GN_EPS) * lnw_ref[:, sl] + lnb_ref[:, sl]
        yr_ref[:, sl] = ((yn + bonus_ref[:, sl]) * gate_ref[:, sl]).astype(BF16)
    acc = jnp.dot(yr_ref[...], w_ref[0:d_rwkv, :], preferred_element_type=F32)
    acc += jnp.dot(ya_ref[...].astype(BF16), w_ref[d_rwkv:, :], preferred_element_type=F32)
    out_ref[...] = x_ref[...] + acc


def _out_proj(x, yf, yb, bonus, gate, ln_w, ln_b, y_attn, w_out, tm):
    t, d = x.shape
    d_rwkv = yf.shape[1]
    d_attn = y_attn.shape[1]
    row = lambda n: pl.BlockSpec((tm, n), lambda i: (i, 0))
    const = lambda a: pl.BlockSpec(a.shape, lambda i: (0, 0))
    return pl.pallas_call(
        functools.partial(_out_proj_kernel, d_rwkv=d_rwkv),
        grid=(t // tm,),
        in_specs=[row(d)] + [row(d_rwkv)] * 4 + [const(ln_w), const(ln_b)]
                 + [row(d_attn), const(w_out)],
        out_specs=row(d),
        out_shape=jax.ShapeDtypeStruct((t, d), F32),
        scratch_shapes=[pltpu.VMEM((tm, d_rwkv), BF16)],
        compiler_params=_cparams(("arbitrary",)),
        name="merge_out_proj",
    )(x, yf, yb, bonus, gate, ln_w, ln_b, y_attn, w_out)


def _router_kernel(x_ref, g_ref, w_ref, b_ref, h_ref, idx_ref, wt_ref):
    x = x_ref[...]
    ms = jnp.mean(x * x, axis=-1, keepdims=True)
    h = x * lax.rsqrt(ms + RMS_EPS) * g_ref[...]
    h_ref[...] = h
    h_hi, h_lo = _split2(h)
    w_hi = w_ref[0]
    w_lo = w_ref[1]
    logits = (jnp.dot(h_hi, w_hi, preferred_element_type=F32)
              + jnp.dot(h_hi, w_lo, preferred_element_type=F32)
              + jnp.dot(h_lo, w_hi, preferred_element_type=F32)) + b_ref[...]
    lane = lax.broadcasted_iota(jnp.int32, logits.shape, 1).astype(F32)
    big = jnp.float32(LANES)
    is_g = lane < N_GROUPS
    gl = jnp.where(is_g, logits, NEG_BIG)
    gmax = jnp.max(gl, axis=-1, keepdims=True)
    gsel = jnp.min(jnp.where(is_g & (gl == gmax), lane, big), axis=-1, keepdims=True)
    g1 = 1.0 / jnp.sum(jnp.where(is_g, jnp.exp(gl - gmax), 0.0), axis=-1, keepdims=True)
    lo = N_GROUPS + gsel * EXPERTS_PER_GROUP
    in_grp = (lane >= lo) & (lane < lo + EXPERTS_PER_GROUP)
    el = jnp.where(in_grp, logits, NEG_BIG)
    v1 = jnp.max(el, axis=-1, keepdims=True)
    i1 = jnp.min(jnp.where(in_grp & (el == v1), lane, big), axis=-1, keepdims=True)
    rest = in_grp & (lane != i1)
    el2 = jnp.where(rest, logits, NEG_BIG)
    v2 = jnp.max(el2, axis=-1, keepdims=True)
    i2 = jnp.min(jnp.where(rest & (el2 == v2), lane, big), axis=-1, keepdims=True)
    e2 = jnp.exp(v2 - v1)
    ww1 = 1.0 / (1.0 + e2)
    ww2 = e2 / (1.0 + e2)
    idx = jnp.where(lane == 0, i1 - N_GROUPS, jnp.where(lane == 1, i2 - N_GROUPS, 0.0))
    idx_ref[...] = idx.astype(jnp.int32)
    wt_ref[...] = jnp.where(lane == 0, g1 * ww1, jnp.where(lane == 1, g1 * ww2, 0.0))


def _router(x, g, w_split, b_pad, tm):
    t, d = x.shape
    return pl.pallas_call(
        _router_kernel,
        grid=(t // tm,),
        in_specs=[pl.BlockSpec((tm, d), lambda i: (i, 0)),
                  pl.BlockSpec((1, d), lambda i: (0, 0)),
                  pl.BlockSpec((2, d, LANES), lambda i: (0, 0, 0)),
                  pl.BlockSpec((1, LANES), lambda i: (0, 0))],
        out_specs=[pl.BlockSpec((tm, d), lambda i: (i, 0)),
                   pl.BlockSpec((tm, LANES), lambda i: (i, 0)),
                   pl.BlockSpec((tm, LANES), lambda i: (i, 0))],
        out_shape=[jax.ShapeDtypeStruct((t, d), F32),
                   jax.ShapeDtypeStruct((t, LANES), jnp.int32),
                   jax.ShapeDtypeStruct((t, LANES), F32)],
        compiler_params=_cparams(("arbitrary",)),
        name="router",
    )(x, g, w_split, b_pad)


def _moe_kernel(te_ref, nt_ref, nv_ref, rows_ref, h_hbm, wg_ref, wu_ref, wd_ref, o_ref,
                xg_ref, xb_ref, acc_ref, sem, *, tm, nf):
    i = pl.program_id(0)
    f = pl.program_id(1)
    nt = nt_ref[0]
    active = i < nt
    slot = i % 2
    n_slab = xg_ref.shape[2] // LANES

    def issue_tile(tile, buf):
        def body(g, carry):
            for u in range(GATHER_UNROLL):
                j = g * GATHER_UNROLL + u
                src = h_hbm.at[pl.ds(rows_ref[tile * tm + j], 1), :]
                pltpu.make_async_copy(src, xg_ref.at[buf, pl.ds(j, 1), :],
                                      sem.at[buf]).start(priority=1)
            return carry
        lax.fori_loop(0, nv_ref[tile] // GATHER_UNROLL, body, 0)

    def wait_tile(tile, buf):
        def body(g, carry):
            pltpu.make_async_copy(h_hbm.at[pl.ds(0, GATHER_CHUNK), :],
                                  xg_ref.at[buf, pl.ds(0, GATHER_CHUNK), :], sem.at[buf]).wait()
            return carry
        lax.fori_loop(0, nv_ref[tile] // GATHER_CHUNK, body, 0)

    @pl.when((i == 0) & (f == 0))
    def _():
        xg_ref[...] = jnp.zeros_like(xg_ref)
        issue_tile(0, 0)

    @pl.when(active & (f == 0))
    def _():
        wait_tile(i, slot)
        xb_ref[...] = xg_ref[slot].astype(BF16)
        issue_tile(i + 1, 1 - slot)

    @pl.when(active)
    def _():
        xb = xb_ref[...]
        gate = jnp.dot(xb, wg_ref[...].astype(BF16), preferred_element_type=F32)
        up = jnp.dot(xb, wu_ref[...].astype(BF16), preferred_element_type=F32)
        hid = (gate * _sigmoid(gate)) * up
        part = jnp.dot(hid.astype(BF16), wd_ref[...].astype(BF16), preferred_element_type=F32)

        @pl.when(f == 0)
        def _():
            acc_ref[...] = part

        @pl.when((f != 0) & (f != nf - 1))
        def _():
            acc_ref[...] += part

        @pl.when(f == nf - 1)
        def _():
            for c in range(n_slab):
                sl = slice(c * LANES, (c + 1) * LANES)
                o_ref[pl.ds(c, tm, stride=n_slab), :] = acc_ref[:, sl] + part[:, sl]

    @pl.when(jnp.logical_not(active) & (f == 0))
    def _():
        o_ref[...] = jnp.zeros_like(o_ref)


def _moe(h, tile_expert, n_tiles, n_valid, rows, w_gate, w_up, w_down, tm, tf):
    t, d = h.shape
    n_e, _, d_e = w_gate.shape
    n_slab = d // LANES
    nt_max = tile_expert.shape[0]
    nf = d_e // tf
    assert nf >= 2 and tm % GATHER_CHUNK == 0 and GATHER_CHUNK % GATHER_UNROLL == 0
    assert rows.shape[0] == (nt_max + 1) * tm and n_valid.shape[0] == nt_max + 1

    def fidx(i, f, nt):
        return jnp.where(i < nt[0], f, nf - 1)

    grid_spec = pltpu.PrefetchScalarGridSpec(
        num_scalar_prefetch=4,
        grid=(nt_max, d_e // tf),
        in_specs=[pl.BlockSpec(memory_space=pl.ANY),
                  pl.BlockSpec((None, d, tf), lambda i, f, te, nt, nv, rw: (te[i], 0, fidx(i, f, nt))),
                  pl.BlockSpec((None, d, tf), lambda i, f, te, nt, nv, rw: (te[i], 0, fidx(i, f, nt))),
                  pl.BlockSpec((None, tf, d), lambda i, f, te, nt, nv, rw: (te[i], fidx(i, f, nt), 0))],
        out_specs=pl.BlockSpec((tm * n_slab, LANES), lambda i, f, te, nt, nv, rw: (i, 0)),
        scratch_shapes=[pltpu.VMEM((2, tm, d), F32), pltpu.VMEM((tm, d), BF16),
                        pltpu.VMEM((tm, d), F32), pltpu.SemaphoreType.DMA((2,))],
    )
    return pl.pallas_call(
        functools.partial(_moe_kernel, tm=tm, nf=nf),
        grid_spec=grid_spec,
        out_shape=jax.ShapeDtypeStruct((nt_max * tm * n_slab, LANES), F32),
        compiler_params=_cparams(("arbitrary", "arbitrary")),
        name="grouped_expert_mlp",
    )(tile_expert, n_tiles, n_valid, rows, h, w_gate, w_up, w_down)


def _final_kernel(s0_ref, s1_ref, x_ref, wt_ref, g_ref, y_hbm, o_ref, yg_ref, sem, *, tm, nt):
    i = pl.program_id(0)
    slot = i % 2
    d = x_ref.shape[1]
    n_slab = d // LANES

    def issue(tile, buf):
        def body(j, carry):
            for k, s_ref in enumerate((s0_ref, s1_ref)):
                src = y_hbm.at[pl.ds(pl.multiple_of(s_ref[tile * tm + j] * n_slab, n_slab), n_slab), :]
                dst = yg_ref.at[buf, k, pl.ds(pl.multiple_of(j * n_slab, n_slab), n_slab), :]
                pltpu.make_async_copy(src, dst, sem.at[buf]).start(priority=k)
            return carry
        lax.fori_loop(0, tm, body, 0, unroll=8)

    @pl.when(i == 0)
    def _():
        issue(0, 0)

    @pl.when(i + 1 < nt)
    def _():
        issue(i + 1, 1 - slot)

    for k in range(2):
        pltpu.make_async_copy(y_hbm.at[pl.ds(0, tm * n_slab), :], yg_ref.at[slot, k],
                              sem.at[slot]).wait()

    wt = wt_ref[...]
    w0 = wt[:, 0:1]
    w1 = wt[:, 1:2]
    ss = jnp.zeros((tm, 1), F32)
    for c in range(n_slab):
        sl = slice(c * LANES, (c + 1) * LANES)
        rows = pl.ds(c, tm, stride=n_slab)
        xs = x_ref[:, sl] + w0 * yg_ref[slot, 0, rows, :] + w1 * yg_ref[slot, 1, rows, :]
        o_ref[:, sl] = xs
        ss = ss + jnp.sum(xs * xs, axis=-1, keepdims=True)
    o_ref[...] = o_ref[...] * lax.rsqrt(ss * (1.0 / d) + RMS_EPS) * g_ref[...]


def _final(slot0, slot1, x, wts, g, y_sorted, tm):
    t, d = x.shape
    grid_spec = pltpu.PrefetchScalarGridSpec(
        num_scalar_prefetch=2,
        grid=(t // tm,),
        in_specs=[pl.BlockSpec((tm, d), lambda i, s0, s1: (i, 0)),
                  pl.BlockSpec((tm, LANES), lambda i, s0, s1: (i, 0)),
                  pl.BlockSpec((1, d), lambda i, s0, s1: (0, 0)),
                  pl.BlockSpec(memory_space=pl.ANY)],
        out_specs=pl.BlockSpec((tm, d), lambda i, s0, s1: (i, 0)),
        scratch_shapes=[pltpu.VMEM((2, 2, tm * (d // LANES), LANES), F32),
                        pltpu.SemaphoreType.DMA((2,))],
    )
    return pl.pallas_call(
        functools.partial(_final_kernel, tm=tm, nt=t // tm),
        grid_spec=grid_spec,
        out_shape=jax.ShapeDtypeStruct((t, d), F32),
        compiler_params=_cparams(("arbitrary",)),
        name="combine_final_norm",
    )(slot0, slot1, x, wts, g, y_sorted)


def _dispatch_plan(experts, tm):
    t = experts.shape[0]
    ef = experts.T.reshape(-1)
    onehot = (ef[:, None] == jnp.arange(N_EXPERTS, dtype=jnp.int32)[None, :]).astype(jnp.int32)
    csum = jnp.cumsum(onehot, axis=0)
    rank = jnp.take_along_axis(csum, ef[:, None], axis=1)[:, 0] - 1
    counts = csum[-1]
    tiles_e = (counts + tm - 1) // tm
    tile_end = jnp.cumsum(tiles_e)
    tile_start = tile_end - tiles_e
    n_tiles = tile_end[-1]
    nt_max = (2 * t) // tm + N_EXPERTS
    slot = tile_start[ef] * tm + rank
    tid = jnp.minimum(jnp.arange(nt_max, dtype=jnp.int32), n_tiles - 1)
    tile_expert = jnp.sum((tile_end[None, :] <= tid[:, None]).astype(jnp.int32), axis=1)
    tok = jnp.tile(jnp.arange(t, dtype=jnp.int32), 2)
    rows = jnp.zeros(((nt_max + 1) * tm,), jnp.int32).at[slot].set(tok)
    tix = jnp.arange(nt_max + 1, dtype=jnp.int32)
    te_all = jnp.concatenate([tile_expert, tile_expert[-1:]])
    left = counts[te_all] - (tix - tile_start[te_all]) * tm
    n_valid = jnp.where(tix < n_tiles, jnp.clip(left, 0, tm), 0)
    n_valid = (n_valid + GATHER_CHUNK - 1) // GATHER_CHUNK * GATHER_CHUNK
    return (slot[:t], slot[t:], rows, tile_expert, n_tiles.reshape(1).astype(jnp.int32),
            n_valid.astype(jnp.int32))


def kernel(x, positions, norm_mix, w_in, mu_shift, w0, w2, a0, a2, g2, k_k, k_a, r_k, ln_x_w, ln_x_b, w_out, norm_ffn, router_group_w, router_group_b, router_expert_w, router_expert_b, w_gate, w_up, w_down, norm_final):
    bsz, seq, d = x.shape
    assert bsz == 1
    depth = w_in.shape[0]
    d_rwkv = k_k.shape[1]
    d_attn = w_out.shape[1] - d_rwkv
    rwkv_cols = mu_shift.shape[2]
    nr = -(-rwkv_cols // (2 * LANES)) * (2 * LANES)
    assert 3 * d_rwkv + 2 * LANES + GATE_LORA == rwkv_cols and nr == 3 * d_rwkv + 4 * LANES
    tm_moe = 576
    tf_moe = 512
    sb_attn = max(dil for _, dil in DILATED_PATTERNS) * Q_BLOCK
    assert seq % sb_attn == 0

    xt = x.reshape(seq, d)
    pos = positions.reshape(seq, 1)
    for l in range(depth):
        w_t = jnp.swapaxes(w_in[l], 0, 1)
        w_r = jnp.pad(w_t[:rwkv_cols], ((0, nr - rwkv_cols), (0, 0))).astype(BF16)
        w_a = w_t[rwkv_cols:].astype(BF16)
        mu = jnp.pad(mu_shift[l], ((0, 0), (0, nr - rwkv_cols)))
        zl = jnp.zeros((DECAY_LORA, d_rwkv), F32)
        w2p = jnp.stack([jnp.concatenate([w2[l, 0], zl]), jnp.concatenate([zl, w2[l, 1]])]).astype(BF16)
        a2p = jnp.stack([jnp.concatenate([a2[l, 0], zl]), jnp.concatenate([zl, a2[l, 1]])]).astype(BF16)
        g2p = jnp.pad(g2[l], ((0, 2 * LANES - GATE_LORA), (0, 0))).astype(BF16)
        rk = r_k[l].reshape(1, d_rwkv)

        z_r = _norm_matmul_shift(xt, norm_mix[l][None], w_r, mu, 256)
        z_a = _norm_matmul_rope(xt, norm_mix[l][None], w_a, pos, 256)
        yf, yb, bonus, gate = _rwkv_mixer(z_r, w0[l], a0[l], w2p, a2p, g2p,
                                          k_k[l][None], k_a[l][None], rk, d_rwkv)
        y_attn = _dilated_attention(z_a, d_attn, sb_attn)
        x2 = _out_proj(xt, yf, yb, bonus, gate, ln_x_w[l][None], ln_x_b[l][None],
                       y_attn, w_out[l].astype(BF16), 256)

        w_rt = jnp.concatenate(
            [router_group_w[l], router_expert_w[l].transpose(1, 0, 2).reshape(d, N_EXPERTS)], axis=1)
        w_rt = jnp.pad(w_rt, ((0, 0), (0, LANES - w_rt.shape[1])))
        rt_hi = w_rt.astype(BF16)
        rt_lo = (w_rt - rt_hi.astype(F32)).astype(BF16)
        b_rt = jnp.concatenate([router_group_b[l], router_expert_b[l].reshape(-1)])
        b_rt = jnp.pad(b_rt, (0, LANES - b_rt.shape[0]))[None]
        h, idx, wts = _router(x2, norm_ffn[l][None], jnp.stack([rt_hi, rt_lo]), b_rt, 256)

        slot0, slot1, rows, tile_expert, n_tiles, n_valid = _dispatch_plan(idx[:, :2], tm_moe)
        y_sorted = _moe(h, tile_expert, n_tiles, n_valid, rows, w_gate[l], w_up[l], w_down[l],
                        tm_moe, tf_moe)
        is_last = l == depth - 1
        assert is_last, "the combine kernel applies the final norm; depth must be 1"
        xt = _final(slot0, slot1, x2, wts, norm_final[None], y_sorted, 512)
    return xt.reshape(bsz, seq, d)
```
